```python
import jax, jax.numpy as jnp
from jax import lax
import numpy as np

D_MODEL = 1024
BATCH = 4
SEQ = 4096
DEPTH = 2

CHUNK = 64
N_A_LAYERS = DEPTH // 2
N_B_LAYERS = DEPTH - N_A_LAYERS
MLA_HEADS = 8
Q_LORA = 384
KV_LORA = 256
NOPE_DIM = 128
ROPE_DIM = 64
V_DIM = 128
ROPE_THETA = 10000.0
Q_BLOCK = 128
B_HEADS = 16
B_HEAD_DIM = 64
LEFT_CHUNKS = 8
BAND = (LEFT_CHUNKS + 1) * CHUNK
MAX_REL = 256
REL_TABLE = MAX_REL + CHUNK
D_FF = -(-8 * D_MODEL // (3 * 256)) * 256
EPS = 1e-6

kernel_name = "yoco_mla_chunked_relbias_swiglu_sandwich"


def rms_norm(x, g):
    xf = x.astype(jnp.float32)
    y = xf * lax.rsqrt(jnp.mean(xf * xf, axis=-1, keepdims=True) + EPS)
    return (y * g.astype(jnp.float32)).astype(x.dtype)


def rope_tables(positions, dtype):
    inv = 1.0 / (ROPE_THETA ** (jnp.arange(0, ROPE_DIM, 2, dtype=jnp.float32) / ROPE_DIM))
    ang = positions.astype(jnp.float32)[..., None] * inv
    return jnp.cos(ang).astype(dtype), jnp.sin(ang).astype(dtype)


def apply_rope(t, cos, sin):
    t1, t2 = jnp.split(t, 2, axis=-1)
    return jnp.concatenate([t1 * cos - t2 * sin, t2 * cos + t1 * sin], axis=-1)


def swiglu(h, w_gate, w_up, w_down):
    return (jax.nn.silu(h @ w_gate) * (h @ w_up)) @ w_down


def mla_mixer(h, cos, sin, w_a, g_q, w_uq, g_kv, w_ukv, w_o):
    B, S, _ = h.shape
    a = h @ w_a
    cq, ckv, kr = jnp.split(a, [Q_LORA, Q_LORA + KV_LORA], axis=-1)
    q = (rms_norm(cq, g_q) @ w_uq).reshape(B, S, MLA_HEADS, NOPE_DIM + ROPE_DIM)
    q_nope = q[..., :NOPE_DIM]
    q_rope = apply_rope(q[..., NOPE_DIM:], cos[:, :, None], sin[:, :, None])
    kv = (rms_norm(ckv, g_kv) @ w_ukv).reshape(B, S, MLA_HEADS, NOPE_DIM + V_DIM)
    k_nope, v = kv[..., :NOPE_DIM], kv[..., NOPE_DIM:]
    k_rope = apply_rope(kr, cos, sin)
    scale = (NOPE_DIM + ROPE_DIM) ** -0.5
    n_blk = S // Q_BLOCK
    key_chunk = jnp.arange(S) // CHUNK

    def to_blocks(t):
        return jnp.moveaxis(t.reshape(B, n_blk, Q_BLOCK, *t.shape[2:]), 1, 0)

    def block_attn(args):
        qn, qr, blk = args
        s = (jnp.einsum('bqhd,bkhd->bhqk', qn, k_nope, preferred_element_type=jnp.float32)
             + jnp.einsum('bqhr,bkr->bhqk', qr, k_rope, preferred_element_type=jnp.float32)) * scale
        q_chunk = (blk * Q_BLOCK + jnp.arange(Q_BLOCK)) // CHUNK
        mask = key_chunk[None, :] <= q_chunk[:, None]
        s = jnp.where(mask[None, None], s, -jnp.inf)
        p = jax.nn.softmax(s, axis=-1).astype(v.dtype)
        return jnp.einsum('bhqk,bkhd->bqhd', p, v)

    o = lax.map(block_attn, (to_blocks(q_nope), to_blocks(q_rope), jnp.arange(n_blk)))
    o = jnp.moveaxis(o, 0, 1).reshape(B, S, MLA_HEADS * V_DIM)
    return o @ w_o


def gather_band(t):
    B, S = t.shape[:2]
    n_c = S // CHUNK
    tp = jnp.pad(t, ((0, 0), (LEFT_CHUNKS * CHUNK, 0), (0, 0), (0, 0)))
    tp = tp.reshape(B, n_c + LEFT_CHUNKS, CHUNK, *t.shape[2:])
    band = jnp.stack([tp[:, j:j + n_c] for j in range(LEFT_CHUNKS + 1)], axis=2)
    return band.reshape(B, n_c, BAND, *t.shape[2:])


def shared_kv(x, g_src, w_kv):
    B, S, _ = x.shape
    kv = (rms_norm(x, g_src) @ w_kv).reshape(B, S, 2, B_HEADS, B_HEAD_DIM)
    return gather_band(kv[:, :, 0]), gather_band(kv[:, :, 1])


def chunked_mixer(h, k_band, v_band, w_q, rel_table, w_o):
    B, S, _ = h.shape
    n_c = S // CHUNK
    q = (h @ w_q).reshape(B, n_c, CHUNK, B_HEADS, B_HEAD_DIM)
    dist = LEFT_CHUNKS * CHUNK + jnp.arange(CHUNK)[:, None] - jnp.arange(BAND)[None, :]
    idx = jnp.clip(dist, -(CHUNK - 1), MAX_REL) + (CHUNK - 1)
    bias = rel_table.astype(jnp.float32)[:, idx]
    valid = (jnp.arange(n_c)[:, None] - LEFT_CHUNKS + jnp.arange(BAND)[None, :] // CHUNK) >= 0
    s = jnp.einsum('bnqhd,bnkhd->bnhqk', q, k_band, preferred_element_type=jnp.float32)
    s = s * (B_HEAD_DIM ** -0.5) + bias[None, None]
    s = jnp.where(valid[None, :, None, None, :], s, -jnp.inf)
    p = jax.nn.softmax(s, axis=-1).astype(v_band.dtype)
    o = jnp.einsum('bnhqk,bnkhd->bnqhd', p, v_band).reshape(B, S, B_HEADS * B_HEAD_DIM)
    return o @ w_o


def setup_inputs(seed: int = 0) -> dict:
    key = jax.random.key(seed)
    ks = iter(jax.random.split(key, 32))
    f32 = jnp.float32

    def w(shape, fan_in):
        return jax.random.normal(next(ks), shape, f32) * (fan_in ** -0.5)

    def gain(shape):
        return 1.0 + 0.05 * jax.random.normal(next(ks), shape, f32)

    x = jax.random.normal(next(ks), (BATCH, SEQ, D_MODEL), f32)
    offsets = jax.random.randint(next(ks), (BATCH, 1), 0, 4096, dtype=jnp.int32)
    positions = (jnp.arange(SEQ, dtype=jnp.int32)[None, :] + offsets).astype(jnp.int32)
    return {
        "x": x,
        "positions": positions,
        "attn_pre_g": gain((DEPTH, D_MODEL)),
        "attn_post_g": gain((DEPTH, D_MODEL)),
        "ffn_pre_g": gain((DEPTH, D_MODEL)),
        "ffn_post_g": gain((DEPTH, D_MODEL)),
        "ffn_w_gate": w((DEPTH, D_MODEL, D_FF), D_MODEL),
        "ffn_w_up": w((DEPTH, D_MODEL, D_FF), D_MODEL),
        "ffn_w_down": w((DEPTH, D_FF, D_MODEL), D_FF),
        "mla_w_a": w((N_A_LAYERS, D_MODEL, Q_LORA + KV_LORA + ROPE_DIM), D_MODEL),
        "mla_g_q": gain((N_A_LAYERS, Q_LORA)),
        "mla_w_uq": w((N_A_LAYERS, Q_LORA, MLA_HEADS * (NOPE_DIM + ROPE_DIM)), Q_LORA),
        "mla_g_kv": gain((N_A_LAYERS, KV_LORA)),
        "mla_w_ukv": w((N_A_LAYERS, KV_LORA, MLA_HEADS * (NOPE_DIM + V_DIM)), KV_LORA),
        "mla_w_o": w((N_A_LAYERS, MLA_HEADS * V_DIM, D_MODEL), MLA_HEADS * V_DIM),
        "kv_src_g": gain((D_MODEL,)),
        "w_kv_shared": w((D_MODEL, 2 * B_HEADS * B_HEAD_DIM), D_MODEL),
        "b_w_q": w((N_B_LAYERS, D_MODEL, B_HEADS * B_HEAD_DIM), D_MODEL),
        "b_rel_table": 0.2 * jax.random.normal(next(ks), (N_B_LAYERS, B_HEADS, REL_TABLE), f32),
        "b_w_o": w((N_B_LAYERS, B_HEADS * B_HEAD_DIM, D_MODEL), B_HEADS * B_HEAD_DIM),
    }


def reference(x, positions, attn_pre_g, attn_post_g, ffn_pre_g, ffn_post_g,
              ffn_w_gate, ffn_w_up, ffn_w_down,
              mla_w_a, mla_g_q, mla_w_uq, mla_g_kv, mla_w_ukv, mla_w_o,
              kv_src_g, w_kv_shared, b_w_q, b_rel_table, b_w_o):
    cos, sin = rope_tables(positions, x.dtype)
    k_band = v_band = None
    for layer in range(DEPTH):
        h = rms_norm(x, attn_pre_g[layer])
        if layer < N_A_LAYERS:
            a = layer
            y = mla_mixer(h, cos, sin, mla_w_a[a], mla_g_q[a], mla_w_uq[a],
                          mla_g_kv[a], mla_w_ukv[a], mla_w_o[a])
        else:
            b = layer - N_A_LAYERS
            y = chunked_mixer(h, k_band, v_band, b_w_q[b], b_rel_table[b], b_w_o[b])
        x = x + rms_norm(y, attn_post_g[layer])
        h = rms_norm(x, ffn_pre_g[layer])
        f = swiglu(h, ffn_w_gate[layer], ffn_w_up[layer], ffn_w_down[layer])
        x = x + rms_norm(f, ffn_post_g[layer])
        if layer == N_A_LAYERS - 1:
            k_band, v_band = shared_kv(x, kv_src_g, w_kv_shared)
    return x
```

```python
import functools

import jax
import jax.numpy as jnp
from jax import lax
from jax.experimental import pallas as pl
from jax.experimental.pallas import tpu as pltpu

F32 = jnp.float32
BF16 = jnp.bfloat16

CHUNK = 64
MLA_HEADS = 8
Q_LORA = 384
KV_LORA = 256
NOPE_DIM = 128
ROPE_DIM = 64
V_DIM = 128
ROPE_THETA = 10000.0
B_HEADS = 16
B_HEAD_DIM = 64
LEFT_CHUNKS = 8
BAND = (LEFT_CHUNKS + 1) * CHUNK
MAX_REL = 256
REL_TABLE = MAX_REL + CHUNK
EPS = 1e-6

LANES = 128
MLA_HEAD_PAD = 2 * LANES
NEG = -1e30

PROJ_ROWS = 512
TAIL_ROWS = 256
MLA_TQ = 256
MLA_TK = 256
PAIR = 2 * CHUNK
PAIR_BAND = BAND + CHUNK
GROUP = 512
HEAD_GROUP = 4
VMEM_LIMIT = 56 * 1024 * 1024


def _rms(x, g):
    ms = jnp.mean(x * x, axis=-1, keepdims=True)
    return x * lax.rsqrt(ms + EPS) * g


def _dot(a, b):
    return jnp.dot(a, b, preferred_element_type=F32)


def _dot_nt(a, b):
    return lax.dot_general(a, b, (((1,), (1,)), ((), ())), preferred_element_type=F32)


def _mla_proj_kernel(x_ref, pos_ref, inv_ref, gpre_ref, wa_ref, gq_ref, wuq_ref,
                     gkv_ref, wuk_ref, wuvT_ref, q_ref, k_ref, vT_ref, *, scale):
    h = _rms(x_ref[...], gpre_ref[...]).astype(BF16)
    a = _dot(h, wa_ref[...])
    cqn = _rms(a[:, :Q_LORA], gq_ref[...]).astype(BF16)
    ckvn = _rms(a[:, Q_LORA:Q_LORA + KV_LORA], gkv_ref[...]).astype(BF16)
    kr = a[:, Q_LORA + KV_LORA:]

    ang = pos_ref[...] * inv_ref[...]
    cos, sin = jnp.cos(ang), jnp.sin(ang)
    lane = lax.broadcasted_iota(jnp.int32, (1, LANES), 1)
    half = ROPE_DIM // 2
    c_tab = jnp.where(lane < ROPE_DIM, cos, 0.0)
    s_lo = jnp.where(lane < half, -sin, 0.0)
    s_hi = jnp.where((lane >= half) & (lane < ROPE_DIM), sin, 0.0)

    def rope(t):
        return (t * c_tab + pltpu.roll(t, LANES - half, 1) * s_lo
                + pltpu.roll(t, half, 1) * s_hi)

    q = _dot(cqn, wuq_ref[...])
    kn = _dot(ckvn, wuk_ref[...])
    kr_rot = rope(kr).astype(BF16)
    for hd in range(MLA_HEADS):
        lo = hd * MLA_HEAD_PAD
        q_ref[:, lo:lo + LANES] = (q[:, lo:lo + LANES] * scale).astype(BF16)
        q_ref[:, lo + LANES:lo + 2 * LANES] = (
            rope(q[:, lo + LANES:lo + 2 * LANES]) * scale).astype(BF16)
        k_ref[:, lo:lo + LANES] = kn[:, hd * LANES:(hd + 1) * LANES].astype(BF16)
        k_ref[:, lo + LANES:lo + 2 * LANES] = kr_rot

    vT = _dot_nt(wuvT_ref[...], ckvn).astype(BF16)
    for t in range(vT_ref.shape[0]):
        vT_ref[t] = vT[:, t * MLA_TK:(t + 1) * MLA_TK]


def _mla_proj(x2, pos, inv128, gpre, wa, gq, wuq, gkv, wuk, wuvT):
    T, D = x2.shape
    tm = PROJ_ROWS
    const = lambda i: (0, 0)
    full = lambda a: pl.BlockSpec(a.shape, const)
    scale = float((NOPE_DIM + ROPE_DIM) ** -0.5)
    return pl.pallas_call(
        functools.partial(_mla_proj_kernel, scale=scale),
        grid=(T // tm,),
        in_specs=[pl.BlockSpec((tm, D), lambda i: (i, 0)),
                  pl.BlockSpec((tm, 1), lambda i: (i, 0)),
                  full(inv128), full(gpre), full(wa), full(gq), full(wuq),
                  full(gkv), full(wuk), full(wuvT)],
        out_specs=[pl.BlockSpec((tm, MLA_HEADS * MLA_HEAD_PAD), lambda i: (i, 0)),
                   pl.BlockSpec((tm, MLA_HEADS * MLA_HEAD_PAD), lambda i: (i, 0)),
                   pl.BlockSpec((tm // MLA_TK, MLA_HEADS * V_DIM, MLA_TK), lambda i: (i, 0, 0))],
        out_shape=[jax.ShapeDtypeStruct((T, MLA_HEADS * MLA_HEAD_PAD), BF16),
                   jax.ShapeDtypeStruct((T, MLA_HEADS * MLA_HEAD_PAD), BF16),
                   jax.ShapeDtypeStruct((T // MLA_TK, MLA_HEADS * V_DIM, MLA_TK), BF16)],
        compiler_params=pltpu.CompilerParams(
            dimension_semantics=("arbitrary",), vmem_limit_bytes=VMEM_LIMIT),
        name="mla_proj",
    )(x2, pos, inv128, gpre, wa, gq, wuq, gkv, wuk, wuvT)


def _mla_attn_kernel(q_ref, k_ref, vT_ref, o_ref):
    qi = pl.program_id(2)
    q = q_ref[...]

    def tile(j, carry, diagonal):
        m, l, acc = carry
        k = k_ref[pl.ds(pl.multiple_of(j * MLA_TK, MLA_TK), MLA_TK), :]
        sT = _dot_nt(k, q)
        if diagonal:
            kc = lax.broadcasted_iota(jnp.int32, sT.shape, 0) // CHUNK
            qc = lax.broadcasted_iota(jnp.int32, sT.shape, 1) // CHUNK
            sT = jnp.where(kc <= qc, sT, NEG)
        m_new = jnp.maximum(m, jnp.max(sT, axis=0, keepdims=True))
        p = jnp.exp(sT - m_new)
        alpha = jnp.exp(m - m_new)
        l = alpha * l + jnp.sum(p, axis=0, keepdims=True)
        acc = alpha * acc + _dot(vT_ref[j], p.astype(BF16))
        return m_new, l, acc

    init = (jnp.full((1, MLA_TQ), NEG, F32), jnp.zeros((1, MLA_TQ), F32),
            jnp.zeros((V_DIM, MLA_TQ), F32))
    carry = lax.fori_loop(0, qi, lambda j, c: tile(j, c, False), init)
    _, l, acc = tile(qi, carry, True)
    o_ref[...] = (acc * (1.0 / l)).T.astype(BF16)


def _mla_attn(q, k, vT3, batch, seq):
    assert MLA_TQ == MLA_TK and MLA_TQ % CHUNK == 0
    T = q.shape[0]
    nq = seq // MLA_TQ
    nkt = seq // MLA_TK
    return pl.pallas_call(
        _mla_attn_kernel,
        grid=(batch, MLA_HEADS, nq),
        in_specs=[pl.BlockSpec((MLA_TQ, MLA_HEAD_PAD), lambda b, h, i: (b * nq + i, h)),
                  pl.BlockSpec((seq, MLA_HEAD_PAD), lambda b, h, i: (b, h)),
                  pl.BlockSpec((nkt, V_DIM, MLA_TK), lambda b, h, i: (b, h, 0))],
        out_specs=pl.BlockSpec((MLA_TQ, V_DIM), lambda b, h, i: (b * nq + i, h)),
        out_shape=jax.ShapeDtypeStruct((T, MLA_HEADS * V_DIM), BF16),
        compiler_params=pltpu.CompilerParams(
            dimension_semantics=("arbitrary", "arbitrary", "arbitrary"),
            vmem_limit_bytes=VMEM_LIMIT),
        name="mla_attn",
    )(q, k, vT3)


def _block_tail_kernel(o_ref, x_ref, wo_ref, gpost_ref, gfpre_ref, wg_ref, wu_ref,
                       wd_ref, gfpost_ref, out_ref):
    y = _dot(o_ref[...], wo_ref[...])
    x1 = x_ref[...] + _rms(y, gpost_ref[...])
    h = _rms(x1, gfpre_ref[...]).astype(BF16)
    g = _dot(h, wg_ref[...])
    u = _dot(h, wu_ref[...])
    a = (g * (1.0 / (1.0 + jnp.exp(-g))) * u).astype(BF16)
    f = _dot(a, wd_ref[...])
    out_ref[...] = x1 + _rms(f, gfpost_ref[...])


def _block_tail(o, x2, wo, gpost, gfpre, wg, wu, wd, gfpost):
    T, D = x2.shape
    tm = TAIL_ROWS
    const = lambda i: (0, 0)
    full = lambda a: pl.BlockSpec(a.shape, const, pipeline_mode=pl.Buffered(1))
    row = lambda a: pl.BlockSpec((tm, a.shape[1]), lambda i: (i, 0))
    return pl.pallas_call(
        _block_tail_kernel,
        grid=(T // tm,),
        in_specs=[row(o), row(x2), full(wo), full(gpost), full(gfpre), full(wg),
                  full(wu), full(wd), full(gfpost)],
        out_specs=pl.BlockSpec((tm, D), lambda i: (i, 0)),
        out_shape=jax.ShapeDtypeStruct((T, D), F32),
        compiler_params=pltpu.CompilerParams(
            dimension_semantics=("arbitrary",), vmem_limit_bytes=VMEM_LIMIT),
        name="block_tail",
    )(o, x2, wo, gpost, gfpre, wg, wu, wd, gfpost)


def _kvq_proj_kernel(x_ref, gsrc_ref, gpre_ref, wq_ref, wkT_ref, wv_ref,
                     q_ref, kT_ref, v_ref, *, scale):
    x = x_ref[...]
    xr = x * lax.rsqrt(jnp.mean(x * x, axis=-1, keepdims=True) + EPS)
    xs = (xr * gsrc_ref[...]).astype(BF16)
    xq = (xr * gpre_ref[...]).astype(BF16)
    q_ref[...] = (_dot(xq, wq_ref[...]) * scale).astype(BF16)
    kT_ref[...] = _dot_nt(wkT_ref[...], xs).astype(BF16)
    v_ref[...] = _dot(xs, wv_ref[...]).astype(BF16)


def _kvq_proj(x2, gsrc, gpre, wq, wkT, wv):
    T, D = x2.shape
    tm = PROJ_ROWS
    hd = B_HEADS * B_HEAD_DIM
    const = lambda i: (0, 0)
    full = lambda a: pl.BlockSpec(a.shape, const)
    return pl.pallas_call(
        functools.partial(_kvq_proj_kernel, scale=float(B_HEAD_DIM ** -0.5)),
        grid=(T // tm,),
        in_specs=[pl.BlockSpec((tm, D), lambda i: (i, 0)),
                  full(gsrc), full(gpre), full(wq), full(wkT), full(wv)],
        out_specs=[pl.BlockSpec((tm, hd), lambda i: (i, 0)),
                   pl.BlockSpec((hd, tm), lambda i: (0, i)),
                   pl.BlockSpec((tm, hd), lambda i: (i, 0))],
        out_shape=[jax.ShapeDtypeStruct((T, hd), BF16),
                   jax.ShapeDtypeStruct((hd, T), BF16),
                   jax.ShapeDtypeStruct((T, hd), BF16)],
        compiler_params=pltpu.CompilerParams(
            dimension_semantics=("arbitrary",), vmem_limit_bytes=VMEM_LIMIT),
        name="kvq_proj",
    )(x2, gsrc, gpre, wq, wkT, wv)


def _rel_bias_kernel(tab_ref, out_ref):
    tab = tab_ref[...]
    t_hi = tab.astype(BF16)
    rem = tab - t_hi.astype(F32)
    t_mid = rem.astype(BF16)
    t_lo = (rem - t_mid.astype(F32)).astype(BF16)
    row = lax.broadcasted_iota(jnp.int32, (REL_TABLE, PAIR_BAND), 0)
    col = lax.broadcasted_iota(jnp.int32, (REL_TABLE, PAIR_BAND), 1)

    def body(r, carry):
        kj = col - (r // CHUNK) * CHUNK
        valid = (kj >= 0) & (kj < BAND)
        dist = LEFT_CHUNKS * CHUNK + (r % CHUNK) - kj
        idx = jnp.clip(dist, -(CHUNK - 1), MAX_REL) + (CHUNK - 1)
        onehot = jnp.where((idx == row) & valid, 1.0, 0.0).astype(BF16)
        res = _dot(t_hi, onehot) + _dot(t_mid, onehot) + _dot(t_lo, onehot)
        out_ref[r] = jnp.where(valid[0:1, :], res, NEG)
        return carry

    lax.fori_loop(0, PAIR, body, 0)


def _rel_bias(table):
    return pl.pallas_call(
        _rel_bias_kernel,
        out_shape=jax.ShapeDtypeStruct((PAIR, B_HEADS, PAIR_BAND), F32),
        name="rel_bias",
    )(table)


def _chunk_attn_kernel(q_ref, kTp_ref, kTc_ref, vp_ref, vc_ref, bias_ref, o_ref,
                       kT_s, v_s):
    g = pl.program_id(1)
    kT_s[:, :GROUP] = kTp_ref[...]
    kT_s[:, GROUP:] = kTc_ref[...]
    v_s[:GROUP, :] = vp_ref[...]
    v_s[GROUP:, :] = vc_ref[...]
    slab = HEAD_GROUP * B_HEAD_DIM
    lane = lax.broadcasted_iota(jnp.int32, (1, slab), 1)
    col = lax.broadcasted_iota(jnp.int32, (1, PAIR_BAND), 1)
    for p in range(GROUP // PAIR):
        first_key = jnp.where(g == 0, GROUP - p * PAIR, 0)
        in_seq = col >= first_key
        rows = slice(p * PAIR, (p + 1) * PAIR)
        band = slice(p * PAIR, p * PAIR + PAIR_BAND)
        for hg in range(B_HEADS // HEAD_GROUP):
            lanes = slice(hg * slab, (hg + 1) * slab)
            q4 = q_ref[rows, lanes]
            kT4 = kT_s[lanes, band]
            v4 = v_s[band, lanes]
            out4 = jnp.zeros((PAIR, slab), F32)
            for hh in range(HEAD_GROUP):
                own = (lane >= hh * B_HEAD_DIM) & (lane < (hh + 1) * B_HEAD_DIM)
                qm = jnp.where(own, q4, jnp.zeros_like(q4))
                s = _dot(qm, kT4) + bias_ref[hg * HEAD_GROUP + hh]
                s = jnp.where(in_seq, s, NEG)
                m = jnp.max(s, axis=-1, keepdims=True)
                e = jnp.exp(s - m)
                l = jnp.sum(e, axis=-1, keepdims=True)
                r = _dot(e.astype(BF16), v4)
                out4 = out4 + jnp.where(own, r * (1.0 / l), 0.0)
            o_ref[rows, lanes] = out4.astype(BF16)


def _chunk_attn(q, kT, v, bias, batch, seq):
    T, hd = q.shape
    ng = seq // GROUP
    prev = lambda b, g: b * ng + jnp.maximum(g - 1, 0)
    cur = lambda b, g: b * ng + g
    return pl.pallas_call(
        _chunk_attn_kernel,
        grid=(batch, ng),
        in_specs=[pl.BlockSpec((GROUP, hd), lambda b, g: (cur(b, g), 0)),
                  pl.BlockSpec((hd, GROUP), lambda b, g: (0, prev(b, g))),
                  pl.BlockSpec((hd, GROUP), lambda b, g: (0, cur(b, g))),
                  pl.BlockSpec((GROUP, hd), lambda b, g: (prev(b, g), 0)),
                  pl.BlockSpec((GROUP, hd), lambda b, g: (cur(b, g), 0)),
                  pl.BlockSpec(bias.shape, lambda b, g: (0, 0, 0))],
        out_specs=pl.BlockSpec((GROUP, hd), lambda b, g: (cur(b, g), 0)),
        out_shape=jax.ShapeDtypeStruct((T, hd), BF16),
        scratch_shapes=[pltpu.VMEM((hd, 2 * GROUP), BF16),
                        pltpu.VMEM((2 * GROUP, hd), BF16)],
        compiler_params=pltpu.CompilerParams(
            dimension_semantics=("arbitrary", "arbitrary"), vmem_limit_bytes=VMEM_LIMIT),
        name="chunk_attn",
    )(q, kT, kT, v, v, bias)


def kernel(x, positions, attn_pre_g, attn_post_g, ffn_pre_g, ffn_post_g, ffn_w_gate,
           ffn_w_up, ffn_w_down, mla_w_a, mla_g_q, mla_w_uq, mla_g_kv, mla_w_ukv, mla_w_o,
           kv_src_g, w_kv_shared, b_w_q, b_rel_table, b_w_o):
    batch, seq, d = x.shape
    T = batch * seq
    assert attn_pre_g.shape[0] == 2 and mla_w_a.shape[0] == 1 and b_w_q.shape[0] == 1
    assert seq % GROUP == 0 and seq % MLA_TQ == 0 and T % PROJ_ROWS == 0
    row = lambda g: g.reshape(1, -1).astype(F32)

    x2 = x.reshape(T, d)
    pos = positions.reshape(T, 1).astype(F32)
    inv = 1.0 / (ROPE_THETA ** (jnp.arange(0, ROPE_DIM, 2, dtype=F32) / ROPE_DIM))
    inv128 = jnp.tile(inv, LANES // (ROPE_DIM // 2)).reshape(1, LANES)

    wa = jnp.pad(mla_w_a[0], ((0, 0), (0, LANES - ROPE_DIM))).astype(BF16)
    wuq = mla_w_uq[0].reshape(Q_LORA, MLA_HEADS, NOPE_DIM + ROPE_DIM)
    wuq = jnp.pad(wuq, ((0, 0), (0, 0), (0, MLA_HEAD_PAD - NOPE_DIM - ROPE_DIM)))
    wuq = wuq.reshape(Q_LORA, MLA_HEADS * MLA_HEAD_PAD).astype(BF16)
    wukv = mla_w_ukv[0].reshape(KV_LORA, MLA_HEADS, NOPE_DIM + V_DIM)
    wuk = wukv[:, :, :NOPE_DIM].reshape(KV_LORA, MLA_HEADS * NOPE_DIM).astype(BF16)
    wuvT = wukv[:, :, NOPE_DIM:].reshape(KV_LORA, MLA_HEADS * V_DIM).T.astype(BF16)
    hd = B_HEADS * B_HEAD_DIM
    wkT = w_kv_shared[:, :hd].T.astype(BF16)
    wv = w_kv_shared[:, hd:].astype(BF16)

    q0, k0, vT0 = _mla_proj(x2, pos, inv128, row(attn_pre_g[0]), wa, row(mla_g_q[0]), wuq,
                            row(mla_g_kv[0]), wuk, wuvT)
    o0 = _mla_attn(q0, k0, vT0, batch, seq)
    x2 = _block_tail(o0, x2, mla_w_o[0].astype(BF16), row(attn_post_g[0]), row(ffn_pre_g[0]),
                     ffn_w_gate[0].astype(BF16), ffn_w_up[0].astype(BF16),
                     ffn_w_down[0].astype(BF16), row(ffn_post_g[0]))

    q1, kT1, v1 = _kvq_proj(x2, row(kv_src_g), row(attn_pre_g[1]), b_w_q[0].astype(BF16), wkT, wv)
    bias = jnp.transpose(_rel_bias(b_rel_table[0]), (1, 0, 2))
    o1 = _chunk_attn(q1, kT1, v1, bias, batch, seq)
    x2 = _block_tail(o1, x2, b_w_o[0].astype(BF16), row(attn_post_g[1]), row(ffn_pre_g[1]),
                     ffn_w_gate[1].astype(BF16), ffn_w_up[1].astype(BF16),
                     ffn_w_down[1].astype(BF16), row(ffn_post_g[1]))
    return x2.reshape(batch, seq, d)
```

```python
import functools
import math

import jax
import jax.numpy as jnp
from jax import lax
from jax.experimental import pallas as pl
from jax.experimental.pallas import tpu as pltpu

F32 = jnp.float32
BF16 = jnp.bfloat16

CHUNK = 64
MLA_HEADS = 8
Q_LORA = 384
KV_LORA = 256
NOPE_DIM = 128
ROPE_DIM = 64
V_DIM = 128
ROPE_THETA = 10000.0
B_HEADS = 16
B_HEAD_DIM = 64
LEFT_CHUNKS = 8
BAND = (LEFT_CHUNKS + 1) * CHUNK
MAX_REL = 256
REL_TABLE = MAX_REL + CHUNK
EPS = 1e-6

LANES = 128
MLA_HEAD_PAD = 2 * LANES
NEG = -1e30

PROJ_ROWS = 512
TAIL_ROWS = 256
MLA_TQ = 256
MLA_TK = 256
PAIR = 2 * CHUNK
PAIR_BAND = BAND + CHUNK
GROUP = 512
HEAD_GROUP = 4
VMEM_LIMIT = 56 * 1024 * 1024


def _rms(x, g):
    ms = jnp.mean(x * x, axis=-1, keepdims=True)
    return x * lax.rsqrt(ms + EPS) * g


def _dot(a, b):
    return jnp.dot(a, b, preferred_element_type=F32)


def _dot_nt(a, b):
    return lax.dot_general(a, b, (((1,), (1,)), ((), ())), preferred_element_type=F32)


def _mla_proj_kernel(x_ref, pos_ref, inv_ref, gpre_ref, wa_ref, gq_ref, wuq_ref,
                     gkv_ref, wuk_ref, wuvT_ref, q_ref, k_ref, vT_ref, *, scale):
    h = _rms(x_ref[...], gpre_ref[...]).astype(BF16)
    a = _dot(h, wa_ref[...])
    cqn = _rms(a[:, :Q_LORA], gq_ref[...]).astype(BF16)
    ckvn = _rms(a[:, Q_LORA:Q_LORA + KV_LORA], gkv_ref[...]).astype(BF16)
    kr = a[:, Q_LORA + KV_LORA:]

    ang = pos_ref[...] * inv_ref[...]
    cos, sin = jnp.cos(ang), jnp.sin(ang)
    lane = lax.broadcasted_iota(jnp.int32, (1, LANES), 1)
    half = ROPE_DIM // 2
    c_tab = jnp.where(lane < ROPE_DIM, cos, 0.0)
    s_lo = jnp.where(lane < half, -sin, 0.0)
    s_hi = jnp.where((lane >= half) & (lane < ROPE_DIM), sin, 0.0)

    def rope(t):
        return (t * c_tab + pltpu.roll(t, LANES - half, 1) * s_lo
                + pltpu.roll(t, half, 1) * s_hi)

    q = _dot(cqn, wuq_ref[...])
    kn = _dot(ckvn, wuk_ref[...])
    kr_rot = rope(kr).astype(BF16)
    for hd in range(MLA_HEADS):
        lo = hd * MLA_HEAD_PAD
        q_ref[:, lo:lo + LANES] = (q[:, lo:lo + LANES] * scale).astype(BF16)
        q_ref[:, lo + LANES:lo + 2 * LANES] = (
            rope(q[:, lo + LANES:lo + 2 * LANES]) * scale).astype(BF16)
        k_ref[:, lo:lo + LANES] = kn[:, hd * LANES:(hd + 1) * LANES].astype(BF16)
        k_ref[:, lo + LANES:lo + 2 * LANES] = kr_rot

    vT = _dot_nt(wuvT_ref[...], ckvn).astype(BF16)
    for t in range(vT_ref.shape[0]):
        vT_ref[t] = vT[:, t * MLA_TK:(t + 1) * MLA_TK]


def _mla_proj(x2, pos, inv128, gpre, wa, gq, wuq, gkv, wuk, wuvT):
    T, D = x2.shape
    tm = PROJ_ROWS
    const = lambda i: (0, 0)
    full = lambda a: pl.BlockSpec(a.shape, const)
    scale = float((NOPE_DIM + ROPE_DIM) ** -0.5 * math.log2(math.e))
    return pl.pallas_call(
        functools.partial(_mla_proj_kernel, scale=scale),
        grid=(T // tm,),
        in_specs=[pl.BlockSpec((tm, D), lambda i: (i, 0)),
                  pl.BlockSpec((tm, 1), lambda i: (i, 0)),
                  full(inv128), full(gpre), full(wa), full(gq), full(wuq),
                  full(gkv), full(wuk), full(wuvT)],
        out_specs=[pl.BlockSpec((tm, MLA_HEADS * MLA_HEAD_PAD), lambda i: (i, 0)),
                   pl.BlockSpec((tm, MLA_HEADS * MLA_HEAD_PAD), lambda i: (i, 0)),
                   pl.BlockSpec((tm // MLA_TK, MLA_HEADS * V_DIM, MLA_TK), lambda i: (i, 0, 0))],
        out_shape=[jax.ShapeDtypeStruct((T, MLA_HEADS * MLA_HEAD_PAD), BF16),
                   jax.ShapeDtypeStruct((T, MLA_HEADS * MLA_HEAD_PAD), BF16),
                   jax.ShapeDtypeStruct((T // MLA_TK, MLA_HEADS * V_DIM, MLA_TK), BF16)],
        compiler_params=pltpu.CompilerParams(
            dimension_semantics=("arbitrary",), vmem_limit_bytes=VMEM_LIMIT),
        name="mla_proj",
    )(x2, pos, inv128, gpre, wa, gq, wuq, gkv, wuk, wuvT)


def _mla_attn_kernel(q_ref, k_ref, vT_ref, o_ref, qT_s, m_s, l_s, a_s, acc_s, s_s, p_s):
    qi = pl.program_id(1)
    m_s[...] = jnp.full(m_s.shape, NEG, F32)
    l_s[...] = jnp.zeros(l_s.shape, F32)
    acc_s[...] = jnp.zeros(acc_s.shape, F32)
    for hd in range(MLA_HEADS):
        qT_s[hd] = q_ref[:, hd * MLA_HEAD_PAD:(hd + 1) * MLA_HEAD_PAD].T

    def tile(j, diagonal):
        rows = pl.ds(pl.multiple_of(j * MLA_TK, MLA_TK), MLA_TK)
        for hd in range(MLA_HEADS):
            k = k_ref[rows, hd * MLA_HEAD_PAD:(hd + 1) * MLA_HEAD_PAD]
            sT = _dot(k, qT_s[hd])
            if diagonal:
                kc = lax.broadcasted_iota(jnp.int32, sT.shape, 0) // CHUNK
                qc = lax.broadcasted_iota(jnp.int32, sT.shape, 1) // CHUNK
                sT = jnp.where(kc <= qc, sT, NEG)
            s_s[hd] = sT
            m = m_s[hd]
            m_new = jnp.maximum(m, jnp.max(sT, axis=0, keepdims=True))
            a_s[hd] = jnp.exp2(m - m_new)
            m_s[hd] = m_new
        for hd in range(MLA_HEADS):
            p = jnp.exp2(s_s[hd] - m_s[hd])
            l_s[hd] = a_s[hd] * l_s[hd] + jnp.sum(p, axis=0, keepdims=True)
            p_s[hd] = p.astype(BF16)
        for hd in range(MLA_HEADS):
            pv = _dot(vT_ref[j, hd * V_DIM:(hd + 1) * V_DIM, :], p_s[hd])
            acc_s[hd] = a_s[hd] * acc_s[hd] + pv

    def full_tile(j, carry):
        tile(j, False)
        return carry

    lax.fori_loop(0, qi, full_tile, 0)
    tile(qi, True)
    for hd in range(MLA_HEADS):
        o = acc_s[hd] * (1.0 / l_s[hd])
        o_ref[:, hd * V_DIM:(hd + 1) * V_DIM] = o.T.astype(BF16)


def _mla_attn(q, k, vT3, batch, seq):
    assert MLA_TQ == MLA_TK and MLA_TQ % CHUNK == 0
    T = q.shape[0]
    nq = seq // MLA_TQ
    nkt = seq // MLA_TK
    once = pl.Buffered(1)
    return pl.pallas_call(
        _mla_attn_kernel,
        grid=(batch, nq),
        in_specs=[pl.BlockSpec((MLA_TQ, MLA_HEADS * MLA_HEAD_PAD), lambda b, i: (b * nq + i, 0)),
                  pl.BlockSpec((seq, MLA_HEADS * MLA_HEAD_PAD), lambda b, i: (b, 0),
                               pipeline_mode=once),
                  pl.BlockSpec((nkt, MLA_HEADS * V_DIM, MLA_TK), lambda b, i: (b, 0, 0),
                               pipeline_mode=once)],
        out_specs=pl.BlockSpec((MLA_TQ, MLA_HEADS * V_DIM), lambda b, i: (b * nq + i, 0)),
        out_shape=jax.ShapeDtypeStruct((T, MLA_HEADS * V_DIM), BF16),
        scratch_shapes=[pltpu.VMEM((MLA_HEADS, MLA_HEAD_PAD, MLA_TQ), BF16),
                        pltpu.VMEM((MLA_HEADS, 1, MLA_TQ), F32),
                        pltpu.VMEM((MLA_HEADS, 1, MLA_TQ), F32),
                        pltpu.VMEM((MLA_HEADS, 1, MLA_TQ), F32),
                        pltpu.VMEM((MLA_HEADS, V_DIM, MLA_TQ), F32),
                        pltpu.VMEM((MLA_HEADS, MLA_TK, MLA_TQ), F32),
                        pltpu.VMEM((MLA_HEADS, MLA_TK, MLA_TQ), BF16)],
        compiler_params=pltpu.CompilerParams(
            dimension_semantics=("arbitrary", "arbitrary"),
            vmem_limit_bytes=VMEM_LIMIT),
        name="mla_attn",
    )(q, k, vT3)


def _block_tail_kernel(o_ref, x_ref, wo_ref, gpost_ref, gfpre_ref, wg_ref, wu_ref,
                       wd_ref, gfpost_ref, out_ref):
    y = _dot(o_ref[...], wo_ref[...])
    x1 = x_ref[...] + _rms(y, gpost_ref[...])
    h = _rms(x1, gfpre_ref[...]).astype(BF16)
    g = _dot(h, wg_ref[...])
    u = _dot(h, wu_ref[...])
    a = (g * (1.0 / (1.0 + jnp.exp(-g))) * u).astype(BF16)
    f = _dot(a, wd_ref[...])
    out_ref[...] = x1 + _rms(f, gfpost_ref[...])


def _block_tail(o, x2, wo, gpost, gfpre, wg, wu, wd, gfpost):
    T, D = x2.shape
    tm = TAIL_ROWS
    const = lambda i: (0, 0)
    full = lambda a: pl.BlockSpec(a.shape, const, pipeline_mode=pl.Buffered(1))
    row = lambda a: pl.BlockSpec((tm, a.shape[1]), lambda i: (i, 0))
    return pl.pallas_call(
        _block_tail_kernel,
        grid=(T // tm,),
        in_specs=[row(o), row(x2), full(wo), full(gpost), full(gfpre), full(wg),
                  full(wu), full(wd), full(gfpost)],
        out_specs=pl.BlockSpec((tm, D), lambda i: (i, 0)),
        out_shape=jax.ShapeDtypeStruct((T, D), F32),
        compiler_params=pltpu.CompilerParams(
            dimension_semantics=("arbitrary",), vmem_limit_bytes=VMEM_LIMIT),
        name="block_tail",
    )(o, x2, wo, gpost, gfpre, wg, wu, wd, gfpost)


def _kvq_proj_kernel(x_ref, gsrc_ref, gpre_ref, wq_ref, wkT_ref, wv_ref,
                     q_ref, kT_ref, v_ref, *, scale):
    x = x_ref[...]
    xr = x * lax.rsqrt(jnp.mean(x * x, axis=-1, keepdims=True) + EPS)
    xs = (xr * gsrc_ref[...]).astype(BF16)
    xq = (xr * gpre_ref[...]).astype(BF16)
    q_ref[...] = (_dot(xq, wq_ref[...]) * scale).astype(BF16)
    kT_ref[...] = _dot_nt(wkT_ref[...], xs).astype(BF16)
    v_ref[...] = _dot(xs, wv_ref[...]).astype(BF16)


def _kvq_proj(x2, gsrc, gpre, wq, wkT, wv):
    T, D = x2.shape
    tm = PROJ_ROWS
    hd = B_HEADS * B_HEAD_DIM
    const = lambda i: (0, 0)
    full = lambda a: pl.BlockSpec(a.shape, const)
    return pl.pallas_call(
        functools.partial(_kvq_proj_kernel, scale=float(B_HEAD_DIM ** -0.5)),
        grid=(T // tm,),
        in_specs=[pl.BlockSpec((tm, D), lambda i: (i, 0)),
                  full(gsrc), full(gpre), full(wq), full(wkT), full(wv)],
        out_specs=[pl.BlockSpec((tm, hd), lambda i: (i, 0)),
                   pl.BlockSpec((hd, tm), lambda i: (0, i)),
                   pl.BlockSpec((tm, hd), lambda i: (i, 0))],
        out_shape=[jax.ShapeDtypeStruct((T, hd), BF16),
                   jax.ShapeDtypeStruct((hd, T), BF16),
                   jax.ShapeDtypeStruct((T, hd), BF16)],
        compiler_params=pltpu.CompilerParams(
            dimension_semantics=("arbitrary",), vmem_limit_bytes=VMEM_LIMIT),
        name="kvq_proj",
    )(x2, gsrc, gpre, wq, wkT, wv)


def _rel_bias_kernel(tab_ref, out_ref):
    tab = tab_ref[...]
    t_hi = tab.astype(BF16)
    rem = tab - t_hi.astype(F32)
    t_mid = rem.astype(BF16)
    t_lo = (rem - t_mid.astype(F32)).astype(BF16)
    row = lax.broadcasted_iota(jnp.int32, (REL_TABLE, PAIR_BAND), 0)
    col = lax.broadcasted_iota(jnp.int32, (REL_TABLE, PAIR_BAND), 1)

    def body(r, carry):
        kj = col - (r // CHUNK) * CHUNK
        valid = (kj >= 0) & (kj < BAND)
        dist = LEFT_CHUNKS * CHUNK + (r % CHUNK) - kj
        idx = jnp.clip(dist, -(CHUNK - 1), MAX_REL) + (CHUNK - 1)
        onehot = jnp.where((idx == row) & valid, 1.0, 0.0).astype(BF16)
        res = _dot(t_hi, onehot) + _dot(t_mid, onehot) + _dot(t_lo, onehot)
        out_ref[r] = jnp.where(valid[0:1, :], res, NEG)
        return carry

    lax.fori_loop(0, PAIR, body, 0)


def _rel_bias(table):
    return pl.pallas_call(
        _rel_bias_kernel,
        out_shape=jax.ShapeDtypeStruct((PAIR, B_HEADS, PAIR_BAND), F32),
        name="rel_bias",
    )(table)


def _chunk_attn_kernel(q_ref, kTp_ref, kTc_ref, vp_ref, vc_ref, bias_ref, o_ref,
                       kT_s, v_s):
    g = pl.program_id(1)
    kT_s[:, :GROUP] = kTp_ref[...]
    kT_s[:, GROUP:] = kTc_ref[...]
    v_s[:GROUP, :] = vp_ref[...]
    v_s[GROUP:, :] = vc_ref[...]
    slab = HEAD_GROUP * B_HEAD_DIM
    lane = lax.broadcasted_iota(jnp.int32, (1, slab), 1)
    col = lax.broadcasted_iota(jnp.int32, (1, PAIR_BAND), 1)
    for p in range(GROUP // PAIR):
        first_key = jnp.where(g == 0, GROUP - p * PAIR, 0)
        in_seq = col >= first_key
        rows = slice(p * PAIR, (p + 1) * PAIR)
        band = slice(p * PAIR, p * PAIR + PAIR_BAND)
        for hg in range(B_HEADS // HEAD_GROUP):
            lanes = slice(hg * slab, (hg + 1) * slab)
            q4 = q_ref[rows, lanes]
            kT4 = kT_s[lanes, band]
            v4 = v_s[band, lanes]
            out4 = jnp.zeros((PAIR, slab), F32)
            for hh in range(HEAD_GROUP):
                own = (lane >= hh * B_HEAD_DIM) & (lane < (hh + 1) * B_HEAD_DIM)
                qm = jnp.where(own, q4, jnp.zeros_like(q4))
                s = _dot(qm, kT4) + bias_ref[hg * HEAD_GROUP + hh]
                s = jnp.where(in_seq, s, NEG)
                m = jnp.max(s, axis=-1, keepdims=True)
                e = jnp.exp(s - m)
                l = jnp.sum(e, axis=-1, keepdims=True)
                r = _dot(e.astype(BF16), v4)
                out4 = out4 + jnp.where(own, r * (1.0 / l), 0.0)
            o_ref[rows, lanes] = out4.astype(BF16)


def _chunk_attn(q, kT, v, bias, batch, seq):
    T, hd = q.shape
    ng = seq // GROUP
    prev = lambda b, g: b * ng + jnp.maximum(g - 1, 0)
    cur = lambda b, g: b * ng + g
    return pl.pallas_call(
        _chunk_attn_kernel,
        grid=(batch, ng),
        in_specs=[pl.BlockSpec((GROUP, hd), lambda b, g: (cur(b, g), 0)),
                  pl.BlockSpec((hd, GROUP), lambda b, g: (0, prev(b, g))),
                  pl.BlockSpec((hd, GROUP), lambda b, g: (0, cur(b, g))),
                  pl.BlockSpec((GROUP, hd), lambda b, g: (prev(b, g), 0)),
                  pl.BlockSpec((GROUP, hd), lambda b, g: (cur(b, g), 0)),
                  pl.BlockSpec(bias.shape, lambda b, g: (0, 0, 0))],
        out_specs=pl.BlockSpec((GROUP, hd), lambda b, g: (cur(b, g), 0)),
        out_shape=jax.ShapeDtypeStruct((T, hd), BF16),
        scratch_shapes=[pltpu.VMEM((hd, 2 * GROUP), BF16),
                        pltpu.VMEM((2 * GROUP, hd), BF16)],
        compiler_params=pltpu.CompilerParams(
            dimension_semantics=("arbitrary", "arbitrary"), vmem_limit_bytes=VMEM_LIMIT),
        name="chunk_attn",
    )(q, kT, kT, v, v, bias)


def kernel(x, positions, attn_pre_g, attn_post_g, ffn_pre_g, ffn_post_g, ffn_w_gate,
           ffn_w_up, ffn_w_down, mla_w_a, mla_g_q, mla_w_uq, mla_g_kv, mla_w_ukv, mla_w_o,
           kv_src_g, w_kv_shared, b_w_q, b_rel_table, b_w_o):
    batch, seq, d = x.shape
    T = batch * seq
    assert attn_pre_g.shape[0] == 2 and mla_w_a.shape[0] == 1 and b_w_q.shape[0] == 1
    assert seq % GROUP == 0 and seq % MLA_TQ == 0 and T % PROJ_ROWS == 0
    row = lambda g: g.reshape(1, -1).astype(F32)

    x2 = x.reshape(T, d)
    pos = positions.reshape(T, 1).astype(F32)
    inv = 1.0 / (ROPE_THETA ** (jnp.arange(0, ROPE_DIM, 2, dtype=F32) / ROPE_DIM))
    inv128 = jnp.tile(inv, LANES // (ROPE_DIM // 2)).reshape(1, LANES)

    wa = jnp.pad(mla_w_a[0], ((0, 0), (0, LANES - ROPE_DIM))).astype(BF16)
    wuq = mla_w_uq[0].reshape(Q_LORA, MLA_HEADS, NOPE_DIM + ROPE_DIM)
    wuq = jnp.pad(wuq, ((0, 0), (0, 0), (0, MLA_HEAD_PAD - NOPE_DIM - ROPE_DIM)))
    wuq = wuq.reshape(Q_LORA, MLA_HEADS * MLA_HEAD_PAD).astype(BF16)
    wukv = mla_w_ukv[0].reshape(KV_LORA, MLA_HEADS, NOPE_DIM + V_DIM)
    wuk = wukv[:, :, :NOPE_DIM].reshape(KV_LORA, MLA_HEADS * NOPE_DIM).astype(BF16)
    wuvT = wukv[:, :, NOPE_DIM:].reshape(KV_LORA, MLA_HEADS * V_DIM).T.astype(BF16)
    hd = B_HEADS * B_HEAD_DIM
    wkT = w_kv_shared[:, :hd].T.astype(BF16)
    wv = w_kv_shared[:, hd:].astype(BF16)

    q0, k0, vT0 = _mla_proj(x2, pos, inv128, row(attn_pre_g[0]), wa, row(mla_g_q[0]), wuq,
                            row(mla_g_kv[0]), wuk, wuvT)
    o0 = _mla_attn(q0, k0, vT0, batch, seq)
    x2 = _block_tail(o0, x2, mla_w_o[0].astype(BF16), row(attn_post_g[0]), row(ffn_pre_g[0]),
                     ffn_w_gate[0].astype(BF16), ffn_w_up[0].astype(BF16),
                     ffn_w_down[0].astype(BF16), row(ffn_post_g[0]))

    q1, kT1, v1 = _kvq_proj(x2, row(kv_src_g), row(attn_pre_g[1]), b_w_q[0].astype(BF16), wkT, wv)
    bias = jnp.transpose(_rel_bias(b_rel_table[0]), (1, 0, 2))
    o1 = _chunk_attn(q1, kT1, v1, bias, batch, seq)
    x2 = _block_tail(o1, x2, b_w_o[0].astype(BF16), row(attn_post_g[1]), row(ffn_pre_g[1]),
                     ffn_w_gate[1].astype(BF16), ffn_w_up[1].astype(BF16),
                     ffn_w_down[1].astype(BF16), row(ffn_post_g[1]))
    return x2.reshape(batch, seq, d)
```

```python
import functools
import math

import jax
import jax.numpy as jnp
from jax import lax
from jax.experimental import pallas as pl
from jax.experimental.pallas import tpu as pltpu

F32 = jnp.float32
BF16 = jnp.bfloat16

CHUNK = 64
MLA_HEADS = 8
Q_LORA = 384
KV_LORA = 256
NOPE_DIM = 128
ROPE_DIM = 64
V_DIM = 128
ROPE_THETA = 10000.0
B_HEADS = 16
B_HEAD_DIM = 64
LEFT_CHUNKS = 8
BAND = (LEFT_CHUNKS + 1) * CHUNK
MAX_REL = 256
REL_TABLE = MAX_REL + CHUNK
EPS = 1e-6

LANES = 128
MLA_HEAD_PAD = 2 * LANES
NEG = -1e30

PROJ_ROWS = 512
TAIL_ROWS = 256
MLA_TQ = 256
MLA_TK = 256
SUM_ROWS = 16
PAIR = 2 * CHUNK
PAIR_BAND = BAND + CHUNK
GROUP = 512
HEAD_GROUP = 4
VMEM_LIMIT = 56 * 1024 * 1024


def _rms(x, g):
    ms = jnp.mean(x * x, axis=-1, keepdims=True)
    return x * lax.rsqrt(ms + EPS) * g


def _dot(a, b):
    return jnp.dot(a, b, preferred_element_type=F32)


def _dot_nt(a, b):
    return lax.dot_general(a, b, (((1,), (1,)), ((), ())), preferred_element_type=F32)


def _mla_proj_kernel(x_ref, pos_ref, inv_ref, gpre_ref, wa_ref, gq_ref, wuq_ref,
                     gkv_ref, wuk_ref, wuvT_ref, q_ref, k_ref, vT_ref, *, scale):
    h = _rms(x_ref[...], gpre_ref[...]).astype(BF16)
    a = _dot(h, wa_ref[...])
    cqn = _rms(a[:, :Q_LORA], gq_ref[...]).astype(BF16)
    ckvn = _rms(a[:, Q_LORA:Q_LORA + KV_LORA], gkv_ref[...]).astype(BF16)
    kr = a[:, Q_LORA + KV_LORA:]

    ang = pos_ref[...] * inv_ref[...]
    cos, sin = jnp.cos(ang), jnp.sin(ang)
    lane = lax.broadcasted_iota(jnp.int32, (1, LANES), 1)
    half = ROPE_DIM // 2
    c_tab = jnp.where(lane < ROPE_DIM, cos, 0.0)
    s_lo = jnp.where(lane < half, -sin, 0.0)
    s_hi = jnp.where((lane >= half) & (lane < ROPE_DIM), sin, 0.0)

    def rope(t):
        return (t * c_tab + pltpu.roll(t, LANES - half, 1) * s_lo
                + pltpu.roll(t, half, 1) * s_hi)

    q = _dot(cqn, wuq_ref[...])
    kn = _dot(ckvn, wuk_ref[...])
    kr_packed = pltpu.bitcast(rope(kr).astype(BF16), jnp.uint32)
    for hd in range(MLA_HEADS):
        lo = hd * MLA_HEAD_PAD
        q_ref[:, lo:lo + LANES] = (q[:, lo:lo + LANES] * scale).astype(BF16)
        q_ref[:, lo + LANES:lo + 2 * LANES] = (
            rope(q[:, lo + LANES:lo + 2 * LANES]) * scale).astype(BF16)
        k_ref[:, lo:lo + LANES] = pltpu.bitcast(
            kn[:, hd * LANES:(hd + 1) * LANES].astype(BF16), jnp.uint32)
        k_ref[:, lo + LANES:lo + 2 * LANES] = kr_packed

    vT = _dot_nt(wuvT_ref[...], ckvn).astype(BF16)
    for t in range(vT_ref.shape[0]):
        vT_ref[t] = vT[:, t * MLA_TK:(t + 1) * MLA_TK]


def _mla_proj(x2, pos, inv128, gpre, wa, gq, wuq, gkv, wuk, wuvT):
    T, D = x2.shape
    tm = PROJ_ROWS
    const = lambda i: (0, 0)
    full = lambda a: pl.BlockSpec(a.shape, const)
    scale = float((NOPE_DIM + ROPE_DIM) ** -0.5 * math.log2(math.e))
    return pl.pallas_call(
        functools.partial(_mla_proj_kernel, scale=scale),
        grid=(T // tm,),
        in_specs=[pl.BlockSpec((tm, D), lambda i: (i, 0)),
                  pl.BlockSpec((tm, 1), lambda i: (i, 0)),
                  full(inv128), full(gpre), full(wa), full(gq), full(wuq),
                  full(gkv), full(wuk), full(wuvT)],
        out_specs=[pl.BlockSpec((tm, MLA_HEADS * MLA_HEAD_PAD), lambda i: (i, 0)),
                   pl.BlockSpec((tm // 2, MLA_HEADS * MLA_HEAD_PAD), lambda i: (i, 0)),
                   pl.BlockSpec((tm // MLA_TK, MLA_HEADS * V_DIM, MLA_TK), lambda i: (i, 0, 0))],
        out_shape=[jax.ShapeDtypeStruct((T, MLA_HEADS * MLA_HEAD_PAD), BF16),
                   jax.ShapeDtypeStruct((T // 2, MLA_HEADS * MLA_HEAD_PAD), jnp.uint32),
                   jax.ShapeDtypeStruct((T // MLA_TK, MLA_HEADS * V_DIM, MLA_TK), BF16)],
        compiler_params=pltpu.CompilerParams(
            dimension_semantics=("arbitrary",), vmem_limit_bytes=VMEM_LIMIT),
        name="mla_proj",
    )(x2, pos, inv128, gpre, wa, gq, wuq, gkv, wuk, wuvT)


def _mla_attn_kernel(q_ref, k_ref, vT_ref, o_ref, qT_s, m_s, l_s, a_s, acc_s, s_s):
    qi = pl.program_id(1)
    m_s[...] = jnp.full(m_s.shape, NEG, F32)
    l_s[...] = jnp.zeros(l_s.shape, F32)
    a_s[...] = jnp.ones(a_s.shape, F32)
    acc_s[...] = jnp.zeros(acc_s.shape, F32)
    s_s[MLA_HEADS:] = jnp.full((MLA_HEADS, MLA_TK, MLA_TQ), -jnp.inf, F32)
    for hd in range(MLA_HEADS):
        qT_s[hd] = q_ref[:, hd * MLA_HEAD_PAD:(hd + 1) * MLA_HEAD_PAD].T
    ones = jnp.ones((SUM_ROWS, MLA_TK), BF16)

    def score_pass(j, diagonal):
        slot = (j & 1) * MLA_HEADS
        half = MLA_TK // 2
        rows = pl.ds(pl.multiple_of(j * half, half), half)
        for hd in range(MLA_HEADS):
            k = pltpu.bitcast(k_ref[rows, hd * MLA_HEAD_PAD:(hd + 1) * MLA_HEAD_PAD], BF16)
            sT = _dot(k, qT_s[hd])
            if diagonal:
                kc = lax.broadcasted_iota(jnp.int32, sT.shape, 0) // CHUNK
                qc = lax.broadcasted_iota(jnp.int32, sT.shape, 1) // CHUNK
                sT = jnp.where(kc <= qc, sT, NEG)
            s_s[slot + hd] = sT
            m = m_s[hd]
            m_new = jnp.maximum(m, jnp.max(sT, axis=0, keepdims=True))
            a_s[hd] = jnp.exp2(m - m_new)
            m_s[hd] = m_new

    def value_pass(j):
        slot = (j & 1) * MLA_HEADS
        jv = jnp.maximum(j, 0)
        for hd in range(MLA_HEADS):
            p = jnp.exp2(s_s[slot + hd] - m_s[hd]).astype(BF16)
            lhs = jnp.concatenate([vT_ref[jv, hd * V_DIM:(hd + 1) * V_DIM, :], ones], axis=0)
            pv = _dot(lhs, p)
            a = a_s[hd]
            acc_s[hd] = a * acc_s[hd] + pv[:V_DIM]
            l_s[hd] = a * l_s[hd] + pv[V_DIM:V_DIM + 1]

    def steady(j, carry):
        value_pass(j - 1)
        score_pass(j, False)
        return carry

    lax.fori_loop(0, qi, steady, 0)
    value_pass(qi - 1)
    score_pass(qi, True)
    value_pass(qi)
    for hd in range(MLA_HEADS):
        o = acc_s[hd] * (1.0 / l_s[hd])
        o_ref[:, hd * V_DIM:(hd + 1) * V_DIM] = o.T.astype(BF16)


def _mla_attn(q, k, vT3, batch, seq):
    assert MLA_TQ == MLA_TK and MLA_TQ % CHUNK == 0
    T = q.shape[0]
    nq = seq // MLA_TQ
    nkt = seq // MLA_TK
    once = pl.Buffered(1)
    return pl.pallas_call(
        _mla_attn_kernel,
        grid=(batch, nq),
        in_specs=[pl.BlockSpec((MLA_TQ, MLA_HEADS * MLA_HEAD_PAD), lambda b, i: (b * nq + i, 0)),
                  pl.BlockSpec((seq // 2, MLA_HEADS * MLA_HEAD_PAD), lambda b, i: (b, 0),
                               pipeline_mode=once),
                  pl.BlockSpec((nkt, MLA_HEADS * V_DIM, MLA_TK), lambda b, i: (b, 0, 0),
                               pipeline_mode=once)],
        out_specs=pl.BlockSpec((MLA_TQ, MLA_HEADS * V_DIM), lambda b, i: (b * nq + i, 0)),
        out_shape=jax.ShapeDtypeStruct((T, MLA_HEADS * V_DIM), BF16),
        scratch_shapes=[pltpu.VMEM((MLA_HEADS, MLA_HEAD_PAD, MLA_TQ), BF16),
                        pltpu.VMEM((MLA_HEADS, 1, MLA_TQ), F32),
                        pltpu.VMEM((MLA_HEADS, 1, MLA_TQ), F32),
                        pltpu.VMEM((MLA_HEADS, 1, MLA_TQ), F32),
                        pltpu.VMEM((MLA_HEADS, V_DIM, MLA_TQ), F32),
                        pltpu.VMEM((2 * MLA_HEADS, MLA_TK, MLA_TQ), F32)],
        compiler_params=pltpu.CompilerParams(
            dimension_semantics=("arbitrary", "arbitrary"),
            vmem_limit_bytes=VMEM_LIMIT),
        name="mla_attn",
    )(q, k, vT3)


def _block_tail_kernel(o_ref, x_ref, wo_ref, gpost_ref, gfpre_ref, wg_ref, wu_ref,
                       wd_ref, gfpost_ref, out_ref):
    y = _dot(o_ref[...], wo_ref[...])
    x1 = x_ref[...] + _rms(y, gpost_ref[...])
    h = _rms(x1, gfpre_ref[...]).astype(BF16)
    g = _dot(h, wg_ref[...])
    u = _dot(h, wu_ref[...])
    a = (g * (1.0 / (1.0 + jnp.exp(-g))) * u).astype(BF16)
    f = _dot(a, wd_ref[...])
    out_ref[...] = x1 + _rms(f, gfpost_ref[...])


def _block_tail(o, x2, wo, gpost, gfpre, wg, wu, wd, gfpost):
    T, D = x2.shape
    tm = TAIL_ROWS
    const = lambda i: (0, 0)
    full = lambda a: pl.BlockSpec(a.shape, const, pipeline_mode=pl.Buffered(1))
    row = lambda a: pl.BlockSpec((tm, a.shape[1]), lambda i: (i, 0))
    return pl.pallas_call(
        _block_tail_kernel,
        grid=(T // tm,),
        in_specs=[row(o), row(x2), full(wo), full(gpost), full(gfpre), full(wg),
                  full(wu), full(wd), full(gfpost)],
        out_specs=pl.BlockSpec((tm, D), lambda i: (i, 0)),
        out_shape=jax.ShapeDtypeStruct((T, D), F32),
        compiler_params=pltpu.CompilerParams(
            dimension_semantics=("arbitrary",), vmem_limit_bytes=VMEM_LIMIT),
        name="block_tail",
    )(o, x2, wo, gpost, gfpre, wg, wu, wd, gfpost)


def _kvq_proj_kernel(x_ref, gsrc_ref, gpre_ref, wq_ref, wkT_ref, wv_ref,
                     q_ref, kT_ref, v_ref, *, scale):
    x = x_ref[...]
    xr = x * lax.rsqrt(jnp.mean(x * x, axis=-1, keepdims=True) + EPS)
    xs = (xr * gsrc_ref[...]).astype(BF16)
    xq = (xr * gpre_ref[...]).astype(BF16)
    q_ref[...] = (_dot(xq, wq_ref[...]) * scale).astype(BF16)
    kT_ref[...] = _dot_nt(wkT_ref[...], xs).astype(BF16)
    v_ref[...] = _dot(xs, wv_ref[...]).astype(BF16)


def _kvq_proj(x2, gsrc, gpre, wq, wkT, wv):
    T, D = x2.shape
    tm = PROJ_ROWS
    hd = B_HEADS * B_HEAD_DIM
    const = lambda i: (0, 0)
    full = lambda a: pl.BlockSpec(a.shape, const)
    return pl.pallas_call(
        functools.partial(_kvq_proj_kernel, scale=float(B_HEAD_DIM ** -0.5)),
        grid=(T // tm,),
        in_specs=[pl.BlockSpec((tm, D), lambda i: (i, 0)),
                  full(gsrc), full(gpre), full(wq), full(wkT), full(wv)],
        out_specs=[pl.BlockSpec((tm, hd), lambda i: (i, 0)),
                   pl.BlockSpec((hd, tm), lambda i: (0, i)),
                   pl.BlockSpec((tm, hd), lambda i: (i, 0))],
        out_shape=[jax.ShapeDtypeStruct((T, hd), BF16),
                   jax.ShapeDtypeStruct((hd, T), BF16),
                   jax.ShapeDtypeStruct((T, hd), BF16)],
        compiler_params=pltpu.CompilerParams(
            dimension_semantics=("arbitrary",), vmem_limit_bytes=VMEM_LIMIT),
        name="kvq_proj",
    )(x2, gsrc, gpre, wq, wkT, wv)


def _rel_bias_kernel(tab_ref, out_ref):
    tab = tab_ref[...]
    t_hi = tab.astype(BF16)
    rem = tab - t_hi.astype(F32)
    t_mid = rem.astype(BF16)
    t_lo = (rem - t_mid.astype(F32)).astype(BF16)
    row = lax.broadcasted_iota(jnp.int32, (REL_TABLE, PAIR_BAND), 0)
    col = lax.broadcasted_iota(jnp.int32, (REL_TABLE, PAIR_BAND), 1)

    def body(r, carry):
        kj = col - (r // CHUNK) * CHUNK
        valid = (kj >= 0) & (kj < BAND)
        dist = LEFT_CHUNKS * CHUNK + (r % CHUNK) - kj
        idx = jnp.clip(dist, -(CHUNK - 1), MAX_REL) + (CHUNK - 1)
        onehot = jnp.where((idx == row) & valid, 1.0, 0.0).astype(BF16)
        res = _dot(t_hi, onehot) + _dot(t_mid, onehot) + _dot(t_lo, onehot)
        out_ref[r] = jnp.where(valid[0:1, :], res, NEG)
        return carry

    lax.fori_loop(0, PAIR, body, 0)


def _rel_bias(table):
    return pl.pallas_call(
        _rel_bias_kernel,
        out_shape=jax.ShapeDtypeStruct((PAIR, B_HEADS, PAIR_BAND), F32),
        name="rel_bias",
    )(table)


def _chunk_attn_kernel(q_ref, kTp_ref, kTc_ref, vp_ref, vc_ref, bias_ref, o_ref,
                       kT_s, v_s):
    g = pl.program_id(1)
    kT_s[:, :GROUP] = kTp_ref[...]
    kT_s[:, GROUP:] = kTc_ref[...]
    v_s[:GROUP, :] = vp_ref[...]
    v_s[GROUP:, :] = vc_ref[...]
    slab = HEAD_GROUP * B_HEAD_DIM
    lane = lax.broadcasted_iota(jnp.int32, (1, slab), 1)
    col = lax.broadcasted_iota(jnp.int32, (1, PAIR_BAND), 1)
    for p in range(GROUP // PAIR):
        first_key = jnp.where(g == 0, GROUP - p * PAIR, 0)
        in_seq = col >= first_key
        rows = slice(p * PAIR, (p + 1) * PAIR)
        band = slice(p * PAIR, p * PAIR + PAIR_BAND)
        for hg in range(B_HEADS // HEAD_GROUP):
            lanes = slice(hg * slab, (hg + 1) * slab)
            q4 = q_ref[rows, lanes]
            kT4 = kT_s[lanes, band]
            v4 = v_s[band, lanes]
            out4 = jnp.zeros((PAIR, slab), F32)
            for hh in range(HEAD_GROUP):
                own = (lane >= hh * B_HEAD_DIM) & (lane < (hh + 1) * B_HEAD_DIM)
                qm = jnp.where(own, q4, jnp.zeros_like(q4))
                s = _dot(qm, kT4) + bias_ref[hg * HEAD_GROUP + hh]
                s = jnp.where(in_seq, s, NEG)
                m = jnp.max(s, axis=-1, keepdims=True)
                e = jnp.exp(s - m)
                l = jnp.sum(e, axis=-1, keepdims=True)
                r = _dot(e.astype(BF16), v4)
                out4 = out4 + jnp.where(own, r * (1.0 / l), 0.0)
            o_ref[rows, lanes] = out4.astype(BF16)


def _chunk_attn(q, kT, v, bias, batch, seq):
    T, hd = q.shape
    ng = seq // GROUP
    prev = lambda b, g: b * ng + jnp.maximum(g - 1, 0)
    cur = lambda b, g: b * ng + g
    return pl.pallas_call(
        _chunk_attn_kernel,
        grid=(batch, ng),
        in_specs=[pl.BlockSpec((GROUP, hd), lambda b, g: (cur(b, g), 0)),
                  pl.BlockSpec((hd, GROUP), lambda b, g: (0, prev(b, g))),
                  pl.BlockSpec((hd, GROUP), lambda b, g: (0, cur(b, g))),
                  pl.BlockSpec((GROUP, hd), lambda b, g: (prev(b, g), 0)),
                  pl.BlockSpec((GROUP, hd), lambda b, g: (cur(b, g), 0)),
                  pl.BlockSpec(bias.shape, lambda b, g: (0, 0, 0))],
        out_specs=pl.BlockSpec((GROUP, hd), lambda b, g: (cur(b, g), 0)),
        out_shape=jax.ShapeDtypeStruct((T, hd), BF16),
        scratch_shapes=[pltpu.VMEM((hd, 2 * GROUP), BF16),
                        pltpu.VMEM((2 * GROUP, hd), BF16)],
        compiler_params=pltpu.CompilerParams(
            dimension_semantics=("arbitrary", "arbitrary"), vmem_limit_bytes=VMEM_LIMIT),
        name="chunk_attn",
    )(q, kT, kT, v, v, bias)


def kernel(x, positions, attn_pre_g, attn_post_g, ffn_pre_g, ffn_post_g, ffn_w_gate,
           ffn_w_up, ffn_w_down, mla_w_a, mla_g_q, mla_w_uq, mla_g_kv, mla_w_ukv, mla_w_o,
           kv_src_g, w_kv_shared, b_w_q, b_rel_table, b_w_o):
    batch, seq, d = x.shape
    T = batch * seq
    assert attn_pre_g.shape[0] == 2 and mla_w_a.shape[0] == 1 and b_w_q.shape[0] == 1
    assert seq % GROUP == 0 and seq % MLA_TQ == 0 and T % PROJ_ROWS == 0
    row = lambda g: g.reshape(1, -1).astype(F32)

    x2 = x.reshape(T, d)
    pos = positions.reshape(T, 1).astype(F32)
    inv = 1.0 / (ROPE_THETA ** (jnp.arange(0, ROPE_DIM, 2, dtype=F32) / ROPE_DIM))
    inv128 = jnp.tile(inv, LANES // (ROPE_DIM // 2)).reshape(1, LANES)

    wa = jnp.pad(mla_w_a[0], ((0, 0), (0, LANES - ROPE_DIM))).astype(BF16)
    wuq = mla_w_uq[0].reshape(Q_LORA, MLA_HEADS, NOPE_DIM + ROPE_DIM)
    wuq = jnp.pad(wuq, ((0, 0), (0, 0), (0, MLA_HEAD_PAD - NOPE_DIM - ROPE_DIM)))
    wuq = wuq.reshape(Q_LORA, MLA_HEADS * MLA_HEAD_PAD).astype(BF16)
    wukv = mla_w_ukv[0].reshape(KV_LORA, MLA_HEADS, NOPE_DIM + V_DIM)
    wuk = wukv[:, :, :NOPE_DIM].reshape(KV_LORA, MLA_HEADS * NOPE_DIM).astype(BF16)
    wuvT = wukv[:, :, NOPE_DIM:].reshape(KV_LORA, MLA_HEADS * V_DIM).T.astype(BF16)
    hd = B_HEADS * B_HEAD_DIM
    wkT = w_kv_shared[:, :hd].T.astype(BF16)
    wv = w_kv_shared[:, hd:].astype(BF16)

    q0, k0, vT0 = _mla_proj(x2, pos, inv128, row(attn_pre_g[0]), wa, row(mla_g_q[0]), wuq,
                            row(mla_g_kv[0]), wuk, wuvT)
    o0 = _mla_attn(q0, k0, vT0, batch, seq)
    x2 = _block_tail(o0, x2, mla_w_o[0].astype(BF16), row(attn_post_g[0]), row(ffn_pre_g[0]),
                     ffn_w_gate[0].astype(BF16), ffn_w_up[0].astype(BF16),
                     ffn_w_down[0].astype(BF16), row(ffn_post_g[0]))

    q1, kT1, v1 = _kvq_proj(x2, row(kv_src_g), row(attn_pre_g[1]), b_w_q[0].astype(BF16), wkT, wv)
    bias = jnp.transpose(_rel_bias(b_rel_table[0]), (1, 0, 2))
    o1 = _chunk_attn(q1, kT1, v1, bias, batch, seq)
    x2 = _block_tail(o1, x2, b_w_o[0].astype(BF16), row(attn_post_g[1]), row(ffn_pre_g[1]),
                     ffn_w_gate[1].astype(BF16), ffn_w_up[1].astype(BF16),
                     ffn_w_down[1].astype(BF16), row(ffn_post_g[1]))
    return x2.reshape(batch, seq, d)
```

```python
import functools
import math

import jax
import jax.numpy as jnp
from jax import lax
from jax.experimental import pallas as pl
from jax.experimental.pallas import tpu as pltpu

F32 = jnp.float32
BF16 = jnp.bfloat16

CHUNK = 64
MLA_HEADS = 8
Q_LORA = 384
KV_LORA = 256
NOPE_DIM = 128
ROPE_DIM = 64
V_DIM = 128
ROPE_THETA = 10000.0
B_HEADS = 16
B_HEAD_DIM = 64
LEFT_CHUNKS = 8
BAND = (LEFT_CHUNKS + 1) * CHUNK
MAX_REL = 256
REL_TABLE = MAX_REL + CHUNK
EPS = 1e-6

LANES = 128
MLA_HEAD_PAD = 2 * LANES
NEG = -1e30

PROJ_ROWS = 512
TAIL_ROWS = 256
MLA_TQ = 256
MLA_TK = 256
SUM_ROWS = 16
QUAD = 4 * CHUNK
QUAD_BAND = BAND + 3 * CHUNK
GROUP = 2 * QUAD
HEAD_GROUP = 2
PHASE_HEADS = 8
BIAS_ROLL = 1024
VMEM_LIMIT = 56 * 1024 * 1024


def _rms(x, g):
    ms = jnp.mean(x * x, axis=-1, keepdims=True)
    return x * lax.rsqrt(ms + EPS) * g


def _dot(a, b):
    return jnp.dot(a, b, preferred_element_type=F32)


def _dot_nt(a, b):
    return lax.dot_general(a, b, (((1,), (1,)), ((), ())), preferred_element_type=F32)


def _mla_proj_kernel(x_ref, pos_ref, posr_ref, inv_ref, invc_ref, gpre_ref, wa_ref, gq_ref,
                     wuqT_ref, gkv_ref, wuk_ref, wuvT_ref, qT_ref, k_ref, vT_ref, *, scale):
    h = _rms(x_ref[...], gpre_ref[...]).astype(BF16)
    a = _dot(h, wa_ref[...])
    cqn = _rms(a[:, :Q_LORA], gq_ref[...]).astype(BF16)
    ckvn = _rms(a[:, Q_LORA:Q_LORA + KV_LORA], gkv_ref[...]).astype(BF16)
    kr = a[:, Q_LORA + KV_LORA:]
    half = ROPE_DIM // 2

    ang = pos_ref[...] * inv_ref[...]
    cos, sin = jnp.cos(ang), jnp.sin(ang)
    lane = lax.broadcasted_iota(jnp.int32, (1, LANES), 1)
    c_tab = jnp.where(lane < ROPE_DIM, cos, 0.0)
    s_lo = jnp.where(lane < half, -sin, 0.0)
    s_hi = jnp.where((lane >= half) & (lane < ROPE_DIM), sin, 0.0)
    kr_rot = (kr * c_tab + pltpu.roll(kr, LANES - half, 1) * s_lo
              + pltpu.roll(kr, half, 1) * s_hi)
    kr_packed = pltpu.bitcast(kr_rot.astype(BF16), jnp.uint32)
    kn = _dot(ckvn, wuk_ref[...])
    for hd in range(MLA_HEADS):
        lo = hd * MLA_HEAD_PAD
        k_ref[:, lo:lo + LANES] = pltpu.bitcast(
            kn[:, hd * LANES:(hd + 1) * LANES].astype(BF16), jnp.uint32)
        k_ref[:, lo + LANES:lo + 2 * LANES] = kr_packed

    angT = invc_ref[...] * posr_ref[...]
    cosT, sinT = jnp.cos(angT) * scale, jnp.sin(angT) * scale
    qT = _dot_nt(wuqT_ref[...], cqn)
    for hd in range(MLA_HEADS):
        lo = hd * MLA_HEAD_PAD
        t1 = qT[lo + NOPE_DIM:lo + NOPE_DIM + half]
        t2 = qT[lo + NOPE_DIM + half:lo + NOPE_DIM + ROPE_DIM]
        qT_ref[lo:lo + NOPE_DIM] = (qT[lo:lo + NOPE_DIM] * scale).astype(BF16)
        qT_ref[lo + NOPE_DIM:lo + NOPE_DIM + half] = (t1 * cosT - t2 * sinT).astype(BF16)
        qT_ref[lo + NOPE_DIM + half:lo + NOPE_DIM + ROPE_DIM] = (t2 * cosT + t1 * sinT).astype(BF16)
        qT_ref[lo + NOPE_DIM + ROPE_DIM:lo + MLA_HEAD_PAD] = jnp.zeros(
            (MLA_HEAD_PAD - NOPE_DIM - ROPE_DIM, qT.shape[1]), BF16)

    vT = _dot_nt(wuvT_ref[...], ckvn).astype(BF16)
    for t in range(vT_ref.shape[0]):
        vT_ref[t] = vT[:, t * MLA_TK:(t + 1) * MLA_TK]


def _mla_proj(x2, pos, pos_row, inv128, inv_col, gpre, wa, gq, wuqT, gkv, wuk, wuvT):
    T, D = x2.shape
    tm = PROJ_ROWS
    const = lambda i: (0, 0)
    full = lambda a: pl.BlockSpec(a.shape, const)
    scale = float((NOPE_DIM + ROPE_DIM) ** -0.5 * math.log2(math.e))
    return pl.pallas_call(
        functools.partial(_mla_proj_kernel, scale=scale),
        grid=(T // tm,),
        in_specs=[pl.BlockSpec((tm, D), lambda i: (i, 0)),
                  pl.BlockSpec((tm, 1), lambda i: (i, 0)),
                  pl.BlockSpec((1, tm), lambda i: (0, i)),
                  full(inv128), full(inv_col), full(gpre), full(wa), full(gq), full(wuqT),
                  full(gkv), full(wuk), full(wuvT)],
        out_specs=[pl.BlockSpec((MLA_HEADS * MLA_HEAD_PAD, tm), lambda i: (0, i)),
                   pl.BlockSpec((tm // 2, MLA_HEADS * MLA_HEAD_PAD), lambda i: (i, 0)),
                   pl.BlockSpec((tm // MLA_TK, MLA_HEADS * V_DIM, MLA_TK), lambda i: (i, 0, 0))],
        out_shape=[jax.ShapeDtypeStruct((MLA_HEADS * MLA_HEAD_PAD, T), BF16),
                   jax.ShapeDtypeStruct((T // 2, MLA_HEADS * MLA_HEAD_PAD), jnp.uint32),
                   jax.ShapeDtypeStruct((T // MLA_TK, MLA_HEADS * V_DIM, MLA_TK), BF16)],
        compiler_params=pltpu.CompilerParams(
            dimension_semantics=("arbitrary",), vmem_limit_bytes=VMEM_LIMIT),
        name="mla_proj",
    )(x2, pos, pos_row, inv128, inv_col, gpre, wa, gq, wuqT, gkv, wuk, wuvT)


def _mla_attn_kernel(qT_ref, k_ref, vT_ref, oT_ref, m_s, l_s, a_s, acc_s, s_s):
    qi = pl.program_id(1)
    m_s[...] = jnp.full(m_s.shape, NEG, F32)
    l_s[...] = jnp.zeros(l_s.shape, F32)
    a_s[...] = jnp.ones(a_s.shape, F32)
    acc_s[...] = jnp.zeros(acc_s.shape, F32)
    s_s[MLA_HEADS:] = jnp.full((MLA_HEADS, MLA_TK, MLA_TQ), -jnp.inf, F32)
    ones = jnp.ones((SUM_ROWS, MLA_TK), BF16)

    def score_pass(j, diagonal):
        slot = (j & 1) * MLA_HEADS
        half = MLA_TK // 2
        rows = pl.ds(pl.multiple_of(j * half, half), half)
        for hd in range(MLA_HEADS):
            k = pltpu.bitcast(k_ref[rows, hd * MLA_HEAD_PAD:(hd + 1) * MLA_HEAD_PAD], BF16)
            qT = qT_ref[hd * MLA_HEAD_PAD:(hd + 1) * MLA_HEAD_PAD, :]
            sT = _dot(k, qT)
            if diagonal:
                kc = lax.broadcasted_iota(jnp.int32, sT.shape, 0) // CHUNK
                qc = lax.broadcasted_iota(jnp.int32, sT.shape, 1) // CHUNK
                sT = jnp.where(kc <= qc, sT, NEG)
            s_s[slot + hd] = sT
            m = m_s[hd]
            m_new = jnp.maximum(m, jnp.max(sT, axis=0, keepdims=True))
            a_s[hd] = jnp.exp2(m - m_new)
            m_s[hd] = m_new

    def value_pass(j):
        slot = (j & 1) * MLA_HEADS
        jv = jnp.maximum(j, 0)
        for hd in range(MLA_HEADS):
            p = jnp.exp2(s_s[slot + hd] - m_s[hd]).astype(BF16)
            lhs = jnp.concatenate([vT_ref[jv, hd * V_DIM:(hd + 1) * V_DIM, :], ones], axis=0)
            pv = _dot(lhs, p)
            a = a_s[hd]
            acc_s[hd] = a * acc_s[hd] + pv[:V_DIM]
            l_s[hd] = a * l_s[hd] + pv[V_DIM:V_DIM + 1]

    def steady(j, carry):
        value_pass(j - 1)
        score_pass(j, False)
        return carry

    lax.fori_loop(0, qi, steady, 0)
    value_pass(qi - 1)
    score_pass(qi, True)
    value_pass(qi)
    for hd in range(MLA_HEADS):
        oT_ref[hd * V_DIM:(hd + 1) * V_DIM, :] = (acc_s[hd] * (1.0 / l_s[hd])).astype(BF16)


def _mla_attn(qT, k, vT3, batch, seq):
    assert MLA_TQ == MLA_TK and MLA_TQ % CHUNK == 0
    T = qT.shape[1]
    nq = seq // MLA_TQ
    nkt = seq // MLA_TK
    once = pl.Buffered(1)
    return pl.pallas_call(
        _mla_attn_kernel,
        grid=(batch, nq),
        in_specs=[pl.BlockSpec((MLA_HEADS * MLA_HEAD_PAD, MLA_TQ), lambda b, i: (0, b * nq + i)),
                  pl.BlockSpec((seq // 2, MLA_HEADS * MLA_HEAD_PAD), lambda b, i: (b, 0),
                               pipeline_mode=once),
                  pl.BlockSpec((nkt, MLA_HEADS * V_DIM, MLA_TK), lambda b, i: (b, 0, 0),
                               pipeline_mode=once)],
        out_specs=pl.BlockSpec((MLA_HEADS * V_DIM, MLA_TQ), lambda b, i: (0, b * nq + i)),
        out_shape=jax.ShapeDtypeStruct((MLA_HEADS * V_DIM, T), BF16),
        scratch_shapes=[pltpu.VMEM((MLA_HEADS, 1, MLA_TQ), F32),
                        pltpu.VMEM((MLA_HEADS, 1, MLA_TQ), F32),
                        pltpu.VMEM((MLA_HEADS, 1, MLA_TQ), F32),
                        pltpu.VMEM((MLA_HEADS, V_DIM, MLA_TQ), F32),
                        pltpu.VMEM((2 * MLA_HEADS, MLA_TK, MLA_TQ), F32)],
        compiler_params=pltpu.CompilerParams(
            dimension_semantics=("arbitrary", "arbitrary"),
            vmem_limit_bytes=VMEM_LIMIT),
        name="mla_attn",
    )(qT, k, vT3)


def _block_tail_kernel(oT_ref, x_ref, wo_ref, gpost_ref, gfpre_ref, wg_ref, wu_ref,
                       wd_ref, gfpost_ref, out_ref):
    y = lax.dot_general(oT_ref[...], wo_ref[...], (((0,), (0,)), ((), ())),
                        preferred_element_type=F32)
    x1 = x_ref[...] + _rms(y, gpost_ref[...])
    h = _rms(x1, gfpre_ref[...]).astype(BF16)
    g = _dot(h, wg_ref[...])
    u = _dot(h, wu_ref[...])
    a = (g * (1.0 / (1.0 + jnp.exp(-g))) * u).astype(BF16)
    f = _dot(a, wd_ref[...])
    out_ref[...] = x1 + _rms(f, gfpost_ref[...])


def _block_tail(oT, x2, wo, gpost, gfpre, wg, wu, wd, gfpost):
    T, D = x2.shape
    tm = TAIL_ROWS
    const = lambda i: (0, 0)
    full = lambda a: pl.BlockSpec(a.shape, const, pipeline_mode=pl.Buffered(1))
    row = lambda a: pl.BlockSpec((tm, a.shape[1]), lambda i: (i, 0))
    return pl.pallas_call(
        _block_tail_kernel,
        grid=(T // tm,),
        in_specs=[pl.BlockSpec((oT.shape[0], tm), lambda i: (0, i)), row(x2), full(wo),
                  full(gpost), full(gfpre), full(wg), full(wu), full(wd), full(gfpost)],
        out_specs=pl.BlockSpec((tm, D), lambda i: (i, 0)),
        out_shape=jax.ShapeDtypeStruct((T, D), F32),
        compiler_params=pltpu.CompilerParams(
            dimension_semantics=("arbitrary",), vmem_limit_bytes=VMEM_LIMIT),
        name="block_tail",
    )(oT, x2, wo, gpost, gfpre, wg, wu, wd, gfpost)


def _kvq_proj_kernel(x_ref, gsrc_ref, gpre_ref, wqT_ref, wk_ref, wvT_ref,
                     qT_ref, k_ref, vT_ref, *, scale):
    x = x_ref[...]
    xr = x * lax.rsqrt(jnp.mean(x * x, axis=-1, keepdims=True) + EPS)
    xs = (xr * gsrc_ref[...]).astype(BF16)
    xq = (xr * gpre_ref[...]).astype(BF16)
    qT_ref[...] = (_dot_nt(wqT_ref[...], xq) * scale).astype(BF16)
    k_ref[...] = pltpu.bitcast(_dot(xs, wk_ref[...]).astype(BF16), jnp.uint32)
    vT_ref[...] = _dot_nt(wvT_ref[...], xs).astype(BF16)


def _kvq_proj(x2, gsrc, gpre, wqT, wk, wvT):
    T, D = x2.shape
    tm = PROJ_ROWS
    hd = B_HEADS * B_HEAD_DIM
    const = lambda i: (0, 0)
    full = lambda a: pl.BlockSpec(a.shape, const)
    scale = float(B_HEAD_DIM ** -0.5 * math.log2(math.e))
    return pl.pallas_call(
        functools.partial(_kvq_proj_kernel, scale=scale),
        grid=(T // tm,),
        in_specs=[pl.BlockSpec((tm, D), lambda i: (i, 0)),
                  full(gsrc), full(gpre), full(wqT), full(wk), full(wvT)],
        out_specs=[pl.BlockSpec((hd, tm), lambda i: (0, i)),
                   pl.BlockSpec((tm // 2, hd), lambda i: (i, 0)),
                   pl.BlockSpec((hd, tm), lambda i: (0, i))],
        out_shape=[jax.ShapeDtypeStruct((hd, T), BF16),
                   jax.ShapeDtypeStruct((T // 2, hd), jnp.uint32),
                   jax.ShapeDtypeStruct((hd, T), BF16)],
        compiler_params=pltpu.CompilerParams(
            dimension_semantics=("arbitrary",), vmem_limit_bytes=VMEM_LIMIT),
        name="kvq_proj",
    )(x2, gsrc, gpre, wqT, wk, wvT)


def _rel_bias_kernel(tab_ref, out_ref, g_s):
    tab = tab_ref[...]
    t_hi = tab.astype(BF16)
    rem = tab - t_hi.astype(F32)
    t_mid = rem.astype(BF16)
    t_lo = (rem - t_mid.astype(F32)).astype(BF16)
    row = lax.broadcasted_iota(jnp.int32, (REL_TABLE, BIAS_ROLL), 0)
    lane = lax.broadcasted_iota(jnp.int32, (REL_TABLE, BIAS_ROLL), 1)
    dist = lane - (QUAD - 1)
    idx = jnp.clip(dist, -(CHUNK - 1), MAX_REL) + (CHUNK - 1)
    onehot = jnp.where(idx == row, 1.0, 0.0).astype(BF16)
    g_s[...] = (_dot(t_hi, onehot) + _dot(t_mid, onehot) + _dot(t_lo, onehot)) * math.log2(math.e)
    q_chunk = lax.broadcasted_iota(jnp.int32, (1, QUAD), 1) // CHUNK + LEFT_CHUNKS

    def body(c, carry):
        rolled = pltpu.roll(g_s[...], (c + (BIAS_ROLL - QUAD_BAND + 1)) % BIAS_ROLL, 1)
        back = q_chunk - c // CHUNK
        valid = (back >= 0) & (back <= LEFT_CHUNKS)
        out_ref[c] = jnp.where(valid, rolled[:, :QUAD], NEG)
        return carry

    lax.fori_loop(0, QUAD_BAND, body, 0, unroll=16)


def _rel_bias(table):
    return pl.pallas_call(
        _rel_bias_kernel,
        out_shape=jax.ShapeDtypeStruct((QUAD_BAND, B_HEADS, QUAD), F32),
        scratch_shapes=[pltpu.VMEM((B_HEADS, BIAS_ROLL), F32)],
        name="rel_bias",
    )(table)


def _chunk_attn_kernel(qT_ref, kp_ref, kc_ref, vTp_ref, vTc_ref, bias_ref, o_ref,
                       k4_s, vTq_s, qTm_s, s_s, m_s, oT_s):
    g = pl.program_id(1)
    slab = HEAD_GROUP * B_HEAD_DIM
    n_hg = B_HEADS // HEAD_GROUP
    n_phase = (GROUP // QUAD) * (B_HEADS // PHASE_HEADS)
    half = GROUP // 2
    for hg in range(n_hg):
        lanes = slice(hg * slab, (hg + 1) * slab)
        k4_s[hg, :half, :] = kp_ref[:, lanes]
        k4_s[hg, half:, :] = kc_ref[:, lanes]
    vTq_s[0, :, :GROUP] = vTp_ref[...]
    vTq_s[0, :, GROUP:] = vTc_ref[:, :QUAD]
    vTq_s[1, :, :QUAD] = vTp_ref[:, QUAD:]
    vTq_s[1, :, QUAD:] = vTc_ref[...]
    @pl.when(g == 0)
    def _():
        qTm_s[...] = jnp.zeros(qTm_s.shape, BF16)

    for h in range(B_HEADS):
        own = slice((h % HEAD_GROUP) * B_HEAD_DIM, (h % HEAD_GROUP + 1) * B_HEAD_DIM)
        for qd in range(GROUP // QUAD):
            qTm_s[h, qd, own, :] = qT_ref[h * B_HEAD_DIM:(h + 1) * B_HEAD_DIM,
                                          qd * QUAD:(qd + 1) * QUAD]
    ones = jnp.ones((SUM_ROWS, QUAD_BAND), BF16)

    def phase_ids(t):
        per_quad = B_HEADS // PHASE_HEADS
        slot, qd, part = (t & 1) * PHASE_HEADS, t >> (per_quad.bit_length() - 1), t & (per_quad - 1)
        return slot, qd, part * PHASE_HEADS, part * (PHASE_HEADS // HEAD_GROUP)

    def score_phase(t, first_group):
        slot, qd, head0, hg0 = phase_ids(t)
        start = qd * (QUAD // 2)
        if not isinstance(start, int):
            start = pl.multiple_of(start, QUAD // 2)
        for i in range(PHASE_HEADS):
            h = head0 + i
            kb = pltpu.bitcast(
                k4_s[hg0 + i // HEAD_GROUP, pl.ds(start, QUAD_BAND // 2), :], BF16)
            s = _dot(kb, qTm_s[h, qd]) + bias_ref[h]
            if first_group:
                key = lax.broadcasted_iota(jnp.int32, (QUAD_BAND, 1), 0)
                s = jnp.where(key >= GROUP - qd * QUAD, s, NEG)
            s_s[slot + i] = s
            m_s[slot + i] = jnp.max(s, axis=0, keepdims=True)

    def value_phase(t):
        slot, qd, head0, _ = phase_ids(t)
        for i in range(PHASE_HEADS):
            row0 = (head0 + i) * B_HEAD_DIM
            if not isinstance(row0, int):
                row0 = pl.multiple_of(row0, B_HEAD_DIM)
            p = jnp.exp2(s_s[slot + i] - m_s[slot + i]).astype(BF16)
            lhs = jnp.concatenate([vTq_s[qd, pl.ds(row0, B_HEAD_DIM), :], ones], axis=0)
            r = _dot(lhs, p)
            oT_s[qd, pl.ds(row0, B_HEAD_DIM), :] = (
                r[:B_HEAD_DIM] * (1.0 / r[B_HEAD_DIM:B_HEAD_DIM + 1]))

    def run(first_group):
        score_phase(0, first_group)

        def steady(t, carry):
            value_phase(t - 1)
            score_phase(t, first_group)
            return carry

        lax.fori_loop(1, n_phase, steady, 0)
        value_phase(n_phase - 1)

    pl.when(g == 0)(lambda: run(True))
    pl.when(g != 0)(lambda: run(False))
    for qd in range(GROUP // QUAD):
        o_ref[:, qd * QUAD:(qd + 1) * QUAD] = oT_s[qd].astype(BF16)


def _chunk_attn(qT, k, vT, bias, batch, seq):
    hd, T = qT.shape
    ng = seq // GROUP
    n_quad = GROUP // QUAD
    prev = lambda b, g: b * ng + jnp.maximum(g - 1, 0)
    cur = lambda b, g: b * ng + g
    return pl.pallas_call(
        _chunk_attn_kernel,
        grid=(batch, ng),
        in_specs=[pl.BlockSpec((hd, GROUP), lambda b, g: (0, cur(b, g))),
                  pl.BlockSpec((GROUP // 2, hd), lambda b, g: (prev(b, g), 0)),
                  pl.BlockSpec((GROUP // 2, hd), lambda b, g: (cur(b, g), 0)),
                  pl.BlockSpec((hd, GROUP), lambda b, g: (0, prev(b, g))),
                  pl.BlockSpec((hd, GROUP), lambda b, g: (0, cur(b, g))),
                  pl.BlockSpec(bias.shape, lambda b, g: (0, 0, 0),
                               pipeline_mode=pl.Buffered(1))],
        out_specs=pl.BlockSpec((hd, GROUP), lambda b, g: (0, cur(b, g))),
        out_shape=jax.ShapeDtypeStruct((hd, T), BF16),
        scratch_shapes=[pltpu.VMEM((B_HEADS // HEAD_GROUP, GROUP, HEAD_GROUP * B_HEAD_DIM),
                                   jnp.uint32),
                        pltpu.VMEM((n_quad, hd, QUAD_BAND), BF16),
                        pltpu.VMEM((B_HEADS, n_quad, HEAD_GROUP * B_HEAD_DIM, QUAD), BF16),
                        pltpu.VMEM((2 * PHASE_HEADS, QUAD_BAND, QUAD), F32),
                        pltpu.VMEM((2 * PHASE_HEADS, 1, QUAD), F32),
                        pltpu.VMEM((n_quad, hd, QUAD), F32)],
        compiler_params=pltpu.CompilerParams(
            dimension_semantics=("arbitrary", "arbitrary"), vmem_limit_bytes=VMEM_LIMIT),
        name="chunk_attn",
    )(qT, k, k, vT, vT, bias)


def kernel(x, positions, attn_pre_g, attn_post_g, ffn_pre_g, ffn_post_g, ffn_w_gate,
           ffn_w_up, ffn_w_down, mla_w_a, mla_g_q, mla_w_uq, mla_g_kv, mla_w_ukv, mla_w_o,
           kv_src_g, w_kv_shared, b_w_q, b_rel_table, b_w_o):
    batch, seq, d = x.shape
    T = batch * seq
    assert attn_pre_g.shape[0] == 2 and mla_w_a.shape[0] == 1 and b_w_q.shape[0] == 1
    assert seq % GROUP == 0 and seq % MLA_TQ == 0 and T % PROJ_ROWS == 0
    row = lambda g: g.reshape(1, -1).astype(F32)

    x2 = x.reshape(T, d)
    pos = positions.reshape(T, 1).astype(F32)
    pos_row = positions.reshape(1, T).astype(F32)
    inv = 1.0 / (ROPE_THETA ** (jnp.arange(0, ROPE_DIM, 2, dtype=F32) / ROPE_DIM))
    inv128 = jnp.tile(inv, LANES // (ROPE_DIM // 2)).reshape(1, LANES)
    inv_col = inv.reshape(ROPE_DIM // 2, 1)

    wa = jnp.pad(mla_w_a[0], ((0, 0), (0, LANES - ROPE_DIM))).astype(BF16)
    wuq = mla_w_uq[0].reshape(Q_LORA, MLA_HEADS, NOPE_DIM + ROPE_DIM)
    wuq = jnp.pad(wuq, ((0, 0), (0, 0), (0, MLA_HEAD_PAD - NOPE_DIM - ROPE_DIM)))
    wuqT = wuq.reshape(Q_LORA, MLA_HEADS * MLA_HEAD_PAD).T.astype(BF16)
    wukv = mla_w_ukv[0].reshape(KV_LORA, MLA_HEADS, NOPE_DIM + V_DIM)
    wuk = wukv[:, :, :NOPE_DIM].reshape(KV_LORA, MLA_HEADS * NOPE_DIM).astype(BF16)
    wuvT = wukv[:, :, NOPE_DIM:].reshape(KV_LORA, MLA_HEADS * V_DIM).T.astype(BF16)
    hd = B_HEADS * B_HEAD_DIM
    wk = w_kv_shared[:, :hd].astype(BF16)
    wvT = w_kv_shared[:, hd:].T.astype(BF16)
    wqT = b_w_q[0].T.astype(BF16)

    qT0, k0, vT0 = _mla_proj(x2, pos, pos_row, inv128, inv_col, row(attn_pre_g[0]), wa,
                             row(mla_g_q[0]), wuqT, row(mla_g_kv[0]), wuk, wuvT)
    o0 = _mla_attn(qT0, k0, vT0, batch, seq)
    x2 = _block_tail(o0, x2, mla_w_o[0].astype(BF16), row(attn_post_g[0]), row(ffn_pre_g[0]),
                     ffn_w_gate[0].astype(BF16), ffn_w_up[0].astype(BF16),
                     ffn_w_down[0].astype(BF16), row(ffn_post_g[0]))

    qT1, k1, vT1 = _kvq_proj(x2, row(kv_src_g), row(attn_pre_g[1]), wqT, wk, wvT)
    bias = jnp.transpose(_rel_bias(b_rel_table[0]), (1, 0, 2))
    o1 = _chunk_attn(qT1, k1, vT1, bias, batch, seq)
    x2 = _block_tail(o1, x2, b_w_o[0].astype(BF16), row(attn_post_g[1]), row(ffn_pre_g[1]),
                     ffn_w_gate[1].astype(BF16), ffn_w_up[1].astype(BF16),
                     ffn_w_down[1].astype(BF16), row(ffn_post_g[1]))
    return x2.reshape(batch, seq, d)
```

```python
import functools
import math

import jax
import jax.numpy as jnp
from jax import lax
from jax.experimental import pallas as pl
from jax.experimental.pallas import tpu as pltpu

F32 = jnp.float32
BF16 = jnp.bfloat16

CHUNK = 64
MLA_HEADS = 8
Q_LORA = 384
KV_LORA = 256
NOPE_DIM = 128
ROPE_DIM = 64
V_DIM = 128
ROPE_THETA = 10000.0
B_HEADS = 16
B_HEAD_DIM = 64
LEFT_CHUNKS = 8
BAND = (LEFT_CHUNKS + 1) * CHUNK
MAX_REL = 256
REL_TABLE = MAX_REL + CHUNK
EPS = 1e-6

LANES = 128
MLA_HEAD_PAD = 2 * LANES
NEG = -1e30

PROJ_ROWS = 512
TAIL_ROWS = 512
TAIL_SPLIT = 2
MLA_TQ = 256
MLA_TK = 256
SUM_ROWS = 16
QUAD = 4 * CHUNK
QUAD_BAND = BAND + 3 * CHUNK
GROUP = 2 * QUAD
HEAD_GROUP = 2
PHASE_HEADS = 8
BIAS_ROLL = 1024
VMEM_LIMIT = 56 * 1024 * 1024


def _rms(x, g):
    ms = jnp.mean(x * x, axis=-1, keepdims=True)
    return x * lax.rsqrt(ms + EPS) * g


def _dot(a, b):
    return jnp.dot(a, b, preferred_element_type=F32)


def _dot_nt(a, b):
    return lax.dot_general(a, b, (((1,), (1,)), ((), ())), preferred_element_type=F32)


def _mla_proj_kernel(x_ref, posr_ref, invc_ref, gpre_ref, wa_ref, wkrT_ref, gq_ref,
                     wuqT_ref, gkv_ref, wuk_ref, wuvT_ref, qT_ref, k_ref, vT_ref, *, scale):
    h = _rms(x_ref[...], gpre_ref[...]).astype(BF16)
    a = _dot(h, wa_ref[...])
    cqn = _rms(a[:, :Q_LORA], gq_ref[...]).astype(BF16)
    ckvn = _rms(a[:, Q_LORA:], gkv_ref[...]).astype(BF16)
    half = ROPE_DIM // 2

    angT = invc_ref[...] * posr_ref[...]
    cosT, sinT = jnp.cos(angT), jnp.sin(angT)

    krT = _dot_nt(wkrT_ref[...], h)
    k1, k2 = krT[:half], krT[half:ROPE_DIM]
    kr_rotT = jnp.concatenate(
        [k1 * cosT - k2 * sinT, k2 * cosT + k1 * sinT,
         jnp.zeros((LANES - ROPE_DIM, krT.shape[1]), F32)], axis=0)
    kr_packed = pltpu.bitcast(kr_rotT.T.astype(BF16), jnp.uint32)
    kn = _dot(ckvn, wuk_ref[...])
    for hd in range(MLA_HEADS):
        lo = hd * MLA_HEAD_PAD
        k_ref[:, lo:lo + LANES] = pltpu.bitcast(
            kn[:, hd * LANES:(hd + 1) * LANES].astype(BF16), jnp.uint32)
        k_ref[:, lo + LANES:lo + 2 * LANES] = kr_packed

    cosT, sinT = cosT * scale, sinT * scale
    qT = _dot_nt(wuqT_ref[...], cqn)
    for hd in range(MLA_HEADS):
        lo = hd * MLA_HEAD_PAD
        t1 = qT[lo + NOPE_DIM:lo + NOPE_DIM + half]
        t2 = qT[lo + NOPE_DIM + half:lo + NOPE_DIM + ROPE_DIM]
        qT_ref[lo:lo + NOPE_DIM] = (qT[lo:lo + NOPE_DIM] * scale).astype(BF16)
        qT_ref[lo + NOPE_DIM:lo + NOPE_DIM + half] = (t1 * cosT - t2 * sinT).astype(BF16)
        qT_ref[lo + NOPE_DIM + half:lo + NOPE_DIM + ROPE_DIM] = (t2 * cosT + t1 * sinT).astype(BF16)
        qT_ref[lo + NOPE_DIM + ROPE_DIM:lo + MLA_HEAD_PAD] = jnp.zeros(
            (MLA_HEAD_PAD - NOPE_DIM - ROPE_DIM, qT.shape[1]), BF16)

    vT = _dot_nt(wuvT_ref[...], ckvn).astype(BF16)
    for t in range(vT_ref.shape[0]):
        vT_ref[t] = vT[:, t * MLA_TK:(t + 1) * MLA_TK]


def _mla_proj(x2, pos_row, inv_col, gpre, wa, wkrT, gq, wuqT, gkv, wuk, wuvT):
    T, D = x2.shape
    tm = PROJ_ROWS
    const = lambda i: (0, 0)
    full = lambda a: pl.BlockSpec(a.shape, const)
    scale = float((NOPE_DIM + ROPE_DIM) ** -0.5 * math.log2(math.e))
    return pl.pallas_call(
        functools.partial(_mla_proj_kernel, scale=scale),
        grid=(T // tm,),
        in_specs=[pl.BlockSpec((tm, D), lambda i: (i, 0)),
                  pl.BlockSpec((1, tm), lambda i: (0, i)),
                  full(inv_col), full(gpre), full(wa), full(wkrT), full(gq), full(wuqT),
                  full(gkv), full(wuk), full(wuvT)],
        out_specs=[pl.BlockSpec((MLA_HEADS * MLA_HEAD_PAD, tm), lambda i: (0, i)),
                   pl.BlockSpec((tm // 2, MLA_HEADS * MLA_HEAD_PAD), lambda i: (i, 0)),
                   pl.BlockSpec((tm // MLA_TK, MLA_HEADS * V_DIM, MLA_TK), lambda i: (i, 0, 0))],
        out_shape=[jax.ShapeDtypeStruct((MLA_HEADS * MLA_HEAD_PAD, T), BF16),
                   jax.ShapeDtypeStruct((T // 2, MLA_HEADS * MLA_HEAD_PAD), jnp.uint32),
                   jax.ShapeDtypeStruct((T // MLA_TK, MLA_HEADS * V_DIM, MLA_TK), BF16)],
        compiler_params=pltpu.CompilerParams(
            dimension_semantics=("arbitrary",), vmem_limit_bytes=VMEM_LIMIT),
        name="mla_proj",
    )(x2, pos_row, inv_col, gpre, wa, wkrT, gq, wuqT, gkv, wuk, wuvT)


def _mla_attn_kernel(qT_ref, k_ref, vT_ref, oT_ref, m_s, l_s, a_s, acc_s, s_s):
    qi = pl.program_id(1)
    m_s[...] = jnp.full(m_s.shape, NEG, F32)
    l_s[...] = jnp.zeros(l_s.shape, F32)
    a_s[...] = jnp.ones(a_s.shape, F32)
    acc_s[...] = jnp.zeros(acc_s.shape, F32)
    s_s[MLA_HEADS:] = jnp.full((MLA_HEADS, MLA_TK, MLA_TQ), -jnp.inf, F32)
    ones = jnp.ones((SUM_ROWS, MLA_TK), BF16)

    def score_pass(j, diagonal):
        slot = (j & 1) * MLA_HEADS
        half = MLA_TK // 2
        rows = pl.ds(pl.multiple_of(j * half, half), half)
        for hd in range(MLA_HEADS):
            k = pltpu.bitcast(k_ref[rows, hd * MLA_HEAD_PAD:(hd + 1) * MLA_HEAD_PAD], BF16)
            qT = qT_ref[hd * MLA_HEAD_PAD:(hd + 1) * MLA_HEAD_PAD, :]
            sT = _dot(k, qT)
            if diagonal:
                kc = lax.broadcasted_iota(jnp.int32, sT.shape, 0) // CHUNK
                qc = lax.broadcasted_iota(jnp.int32, sT.shape, 1) // CHUNK
                sT = jnp.where(kc <= qc, sT, NEG)
            s_s[slot + hd] = sT
            m = m_s[hd]
            m_new = jnp.maximum(m, jnp.max(sT, axis=0, keepdims=True))
            a_s[hd] = jnp.exp2(m - m_new)
            m_s[hd] = m_new

    def value_pass(j):
        slot = (j & 1) * MLA_HEADS
        jv = jnp.maximum(j, 0)
        for hd in range(MLA_HEADS):
            p = jnp.exp2(s_s[slot + hd] - m_s[hd]).astype(BF16)
            lhs = jnp.concatenate([vT_ref[jv, hd * V_DIM:(hd + 1) * V_DIM, :], ones], axis=0)
            pv = _dot(lhs, p)
            a = a_s[hd]
            acc_s[hd] = a * acc_s[hd] + pv[:V_DIM]
            l_s[hd] = a * l_s[hd] + pv[V_DIM:V_DIM + 1]

    def steady(j, carry):
        value_pass(j - 1)
        score_pass(j, False)
        return carry

    lax.fori_loop(0, qi, steady, 0)
    value_pass(qi - 1)
    score_pass(qi, True)
    value_pass(qi)
    for hd in range(MLA_HEADS):
        oT_ref[hd * V_DIM:(hd + 1) * V_DIM, :] = (acc_s[hd] * (1.0 / l_s[hd])).astype(BF16)


def _mla_attn(qT, k, vT3, batch, seq):
    assert MLA_TQ == MLA_TK and MLA_TQ % CHUNK == 0
    T = qT.shape[1]
    nq = seq // MLA_TQ
    nkt = seq // MLA_TK
    return pl.pallas_call(
        _mla_attn_kernel,
        grid=(batch, nq),
        in_specs=[pl.BlockSpec((MLA_HEADS * MLA_HEAD_PAD, MLA_TQ), lambda b, i: (0, b * nq + i)),
                  pl.BlockSpec((seq // 2, MLA_HEADS * MLA_HEAD_PAD), lambda b, i: (b, 0)),
                  pl.BlockSpec((nkt, MLA_HEADS * V_DIM, MLA_TK), lambda b, i: (b, 0, 0),
                               pipeline_mode=pl.Buffered(1))],
        out_specs=pl.BlockSpec((MLA_HEADS * V_DIM, MLA_TQ), lambda b, i: (0, b * nq + i)),
        out_shape=jax.ShapeDtypeStruct((MLA_HEADS * V_DIM, T), BF16),
        scratch_shapes=[pltpu.VMEM((MLA_HEADS, 1, MLA_TQ), F32),
                        pltpu.VMEM((MLA_HEADS, 1, MLA_TQ), F32),
                        pltpu.VMEM((MLA_HEADS, 1, MLA_TQ), F32),
                        pltpu.VMEM((MLA_HEADS, V_DIM, MLA_TQ), F32),
                        pltpu.VMEM((2 * MLA_HEADS, MLA_TK, MLA_TQ), F32)],
        compiler_params=pltpu.CompilerParams(
            dimension_semantics=("arbitrary", "arbitrary"),
            vmem_limit_bytes=VMEM_LIMIT),
        name="mla_attn",
    )(qT, k, vT3)


def _block_tail_kernel(oT_ref, x_ref, wo_ref, gpost_ref, gfpre_ref, wg_ref, wu_ref,
                       wd_ref, gfpost_ref, out_ref):
    sub = out_ref.shape[0] // TAIL_SPLIT
    blocks = [slice(i * sub, (i + 1) * sub) for i in range(TAIL_SPLIT)]
    y = [lax.dot_general(oT_ref[:, r], wo_ref[...], (((0,), (0,)), ((), ())),
                         preferred_element_type=F32) for r in blocks]
    x1 = [x_ref[r, :] + _rms(yi, gpost_ref[...]) for r, yi in zip(blocks, y)]
    h = [_rms(xi, gfpre_ref[...]).astype(BF16) for xi in x1]
    g = [_dot(hi, wg_ref[...]) for hi in h]
    u = [_dot(hi, wu_ref[...]) for hi in h]
    a = [(gi * (1.0 / (1.0 + jnp.exp(-gi))) * ui).astype(BF16) for gi, ui in zip(g, u)]
    f = [_dot(ai, wd_ref[...]) for ai in a]
    for r, xi, fi in zip(blocks, x1, f):
        out_ref[r, :] = xi + _rms(fi, gfpost_ref[...])


def _block_tail(oT, x2, wo, gpost, gfpre, wg, wu, wd, gfpost):
    T, D = x2.shape
    tm = TAIL_ROWS
    const = lambda i: (0, 0)
    full = lambda a: pl.BlockSpec(a.shape, const, pipeline_mode=pl.Buffered(1))
    row = lambda a: pl.BlockSpec((tm, a.shape[1]), lambda i: (i, 0))
    return pl.pallas_call(
        _block_tail_kernel,
        grid=(T // tm,),
        in_specs=[pl.BlockSpec((oT.shape[0], tm), lambda i: (0, i)), row(x2), full(wo),
                  full(gpost), full(gfpre), full(wg), full(wu), full(wd), full(gfpost)],
        out_specs=pl.BlockSpec((tm, D), lambda i: (i, 0)),
        out_shape=jax.ShapeDtypeStruct((T, D), F32),
        compiler_params=pltpu.CompilerParams(
            dimension_semantics=("arbitrary",), vmem_limit_bytes=VMEM_LIMIT),
        name="block_tail",
    )(oT, x2, wo, gpost, gfpre, wg, wu, wd, gfpost)


def _kvq_proj_kernel(x_ref, gsrc_ref, gpre_ref, wqT_ref, wk_ref, wvT_ref,
                     qT_ref, k_ref, vT_ref, *, scale):
    x = x_ref[...]
    xr = x * lax.rsqrt(jnp.mean(x * x, axis=-1, keepdims=True) + EPS)
    xs = (xr * gsrc_ref[...]).astype(BF16)
    xq = (xr * gpre_ref[...]).astype(BF16)
    qT_ref[...] = (_dot_nt(wqT_ref[...], xq) * scale).astype(BF16)
    k_ref[...] = pltpu.bitcast(_dot(xs, wk_ref[...]).astype(BF16), jnp.uint32)
    vT_ref[...] = _dot_nt(wvT_ref[...], xs).astype(BF16)


def _kvq_proj(x2, gsrc, gpre, wqT, wk, wvT):
    T, D = x2.shape
    tm = PROJ_ROWS
    hd = B_HEADS * B_HEAD_DIM
    const = lambda i: (0, 0)
    full = lambda a: pl.BlockSpec(a.shape, const)
    scale = float(B_HEAD_DIM ** -0.5 * math.log2(math.e))
    return pl.pallas_call(
        functools.partial(_kvq_proj_kernel, scale=scale),
        grid=(T // tm,),
        in_specs=[pl.BlockSpec((tm, D), lambda i: (i, 0)),
                  full(gsrc), full(gpre), full(wqT), full(wk), full(wvT)],
        out_specs=[pl.BlockSpec((hd, tm), lambda i: (0, i)),
                   pl.BlockSpec((tm // 2, hd), lambda i: (i, 0)),
                   pl.BlockSpec((hd, tm), lambda i: (0, i))],
        out_shape=[jax.ShapeDtypeStruct((hd, T), BF16),
                   jax.ShapeDtypeStruct((T // 2, hd), jnp.uint32),
                   jax.ShapeDtypeStruct((hd, T), BF16)],
        compiler_params=pltpu.CompilerParams(
            dimension_semantics=("arbitrary",), vmem_limit_bytes=VMEM_LIMIT),
        name="kvq_proj",
    )(x2, gsrc, gpre, wqT, wk, wvT)


def _rel_bias_kernel(tab_ref, out_ref, g_s):
    tab = tab_ref[...]
    t_hi = tab.astype(BF16)
    rem = tab - t_hi.astype(F32)
    t_mid = rem.astype(BF16)
    t_lo = (rem - t_mid.astype(F32)).astype(BF16)
    row = lax.broadcasted_iota(jnp.int32, (REL_TABLE, BIAS_ROLL), 0)
    lane = lax.broadcasted_iota(jnp.int32, (REL_TABLE, BIAS_ROLL), 1)
    dist = lane - (QUAD - 1)
    idx = jnp.clip(dist, -(CHUNK - 1), MAX_REL) + (CHUNK - 1)
    onehot = jnp.where(idx == row, 1.0, 0.0).astype(BF16)
    g_s[...] = (_dot(t_hi, onehot) + _dot(t_mid, onehot) + _dot(t_lo, onehot)) * math.log2(math.e)
    q_chunk = lax.broadcasted_iota(jnp.int32, (1, QUAD), 1) // CHUNK + LEFT_CHUNKS

    def body(c, carry):
        rolled = pltpu.roll(g_s[...], (c + (BIAS_ROLL - QUAD_BAND + 1)) % BIAS_ROLL, 1)
        back = q_chunk - c // CHUNK
        valid = (back >= 0) & (back <= LEFT_CHUNKS)
        out_ref[c] = jnp.where(valid, rolled[:, :QUAD], NEG)
        return carry

    lax.fori_loop(0, QUAD_BAND, body, 0, unroll=16)


def _rel_bias(table):
    return pl.pallas_call(
        _rel_bias_kernel,
        out_shape=jax.ShapeDtypeStruct((QUAD_BAND, B_HEADS, QUAD), F32),
        scratch_shapes=[pltpu.VMEM((B_HEADS, BIAS_ROLL), F32)],
        name="rel_bias",
    )(table)


def _chunk_attn_kernel(qT_ref, kp_ref, kc_ref, vTp_ref, vTc_ref, bias_ref, o_ref,
                       k4_s, vTq_s, qTm_s, s_s, m_s, oT_s):
    g = pl.program_id(1)
    slab = HEAD_GROUP * B_HEAD_DIM
    n_hg = B_HEADS // HEAD_GROUP
    n_phase = (GROUP // QUAD) * (B_HEADS // PHASE_HEADS)
    half = GROUP // 2
    for hg in range(n_hg):
        lanes = slice(hg * slab, (hg + 1) * slab)
        k4_s[hg, :half, :] = kp_ref[:, lanes]
        k4_s[hg, half:, :] = kc_ref[:, lanes]
    vTq_s[0, :, :GROUP] = vTp_ref[...]
    vTq_s[0, :, GROUP:] = vTc_ref[:, :QUAD]
    vTq_s[1, :, :QUAD] = vTp_ref[:, QUAD:]
    vTq_s[1, :, QUAD:] = vTc_ref[...]
    @pl.when(g == 0)
    def _():
        qTm_s[...] = jnp.zeros(qTm_s.shape, BF16)

    for h in range(B_HEADS):
        own = slice((h % HEAD_GROUP) * B_HEAD_DIM, (h % HEAD_GROUP + 1) * B_HEAD_DIM)
        for qd in range(GROUP // QUAD):
            qTm_s[h, qd, own, :] = qT_ref[h * B_HEAD_DIM:(h + 1) * B_HEAD_DIM,
                                          qd * QUAD:(qd + 1) * QUAD]
    ones = jnp.ones((SUM_ROWS, QUAD_BAND), BF16)

    def phase_ids(t):
        per_quad = B_HEADS // PHASE_HEADS
        slot, qd, part = (t & 1) * PHASE_HEADS, t >> (per_quad.bit_length() - 1), t & (per_quad - 1)
        return slot, qd, part * PHASE_HEADS, part * (PHASE_HEADS // HEAD_GROUP)

    def score_phase(t, first_group):
        slot, qd, head0, hg0 = phase_ids(t)
        start = qd * (QUAD // 2)
        if not isinstance(start, int):
            start = pl.multiple_of(start, QUAD // 2)
        for i in range(PHASE_HEADS):
            h = head0 + i
            kb = pltpu.bitcast(
                k4_s[hg0 + i // HEAD_GROUP, pl.ds(start, QUAD_BAND // 2), :], BF16)
            s = _dot(kb, qTm_s[h, qd]) + bias_ref[h]
            if first_group:
                key = lax.broadcasted_iota(jnp.int32, (QUAD_BAND, 1), 0)
                s = jnp.where(key >= GROUP - qd * QUAD, s, NEG)
            s_s[slot + i] = s
            m_s[slot + i] = jnp.max(s, axis=0, keepdims=True)

    def value_phase(t):
        slot, qd, head0, _ = phase_ids(t)
        for i in range(PHASE_HEADS):
            row0 = (head0 + i) * B_HEAD_DIM
            if not isinstance(row0, int):
                row0 = pl.multiple_of(row0, B_HEAD_DIM)
            p = jnp.exp2(s_s[slot + i] - m_s[slot + i]).astype(BF16)
            lhs = jnp.concatenate([vTq_s[qd, pl.ds(row0, B_HEAD_DIM), :], ones], axis=0)
            r = _dot(lhs, p)
            oT_s[qd, pl.ds(row0, B_HEAD_DIM), :] = (
                r[:B_HEAD_DIM] * (1.0 / r[B_HEAD_DIM:B_HEAD_DIM + 1]))

    def run(first_group):
        score_phase(0, first_group)

        def steady(t, carry):
            value_phase(t - 1)
            score_phase(t, first_group)
            return carry

        lax.fori_loop(1, n_phase, steady, 0)
        value_phase(n_phase - 1)

    pl.when(g == 0)(lambda: run(True))
    pl.when(g != 0)(lambda: run(False))
    for qd in range(GROUP // QUAD):
        o_ref[:, qd * QUAD:(qd + 1) * QUAD] = oT_s[qd].astype(BF16)


def _chunk_attn(qT, k, vT, bias, batch, seq):
    hd, T = qT.shape
    ng = seq // GROUP
    n_quad = GROUP // QUAD
    prev = lambda b, g: b * ng + jnp.maximum(g - 1, 0)
    cur = lambda b, g: b * ng + g
    return pl.pallas_call(
        _chunk_attn_kernel,
        grid=(batch, ng),
        in_specs=[pl.BlockSpec((hd, GROUP), lambda b, g: (0, cur(b, g))),
                  pl.BlockSpec((GROUP // 2, hd), lambda b, g: (prev(b, g), 0)),
                  pl.BlockSpec((GROUP // 2, hd), lambda b, g: (cur(b, g), 0)),
                  pl.BlockSpec((hd, GROUP), lambda b, g: (0, prev(b, g))),
                  pl.BlockSpec((hd, GROUP), lambda b, g: (0, cur(b, g))),
                  pl.BlockSpec(bias.shape, lambda b, g: (0, 0, 0),
                               pipeline_mode=pl.Buffered(1))],
        out_specs=pl.BlockSpec((hd, GROUP), lambda b, g: (0, cur(b, g))),
        out_shape=jax.ShapeDtypeStruct((hd, T), BF16),
        scratch_shapes=[pltpu.VMEM((B_HEADS // HEAD_GROUP, GROUP, HEAD_GROUP * B_HEAD_DIM),
                                   jnp.uint32),
                        pltpu.VMEM((n_quad, hd, QUAD_BAND), BF16),
                        pltpu.VMEM((B_HEADS, n_quad, HEAD_GROUP * B_HEAD_DIM, QUAD), BF16),
                        pltpu.VMEM((2 * PHASE_HEADS, QUAD_BAND, QUAD), F32),
                        pltpu.VMEM((2 * PHASE_HEADS, 1, QUAD), F32),
                        pltpu.VMEM((n_quad, hd, QUAD), F32)],
        compiler_params=pltpu.CompilerParams(
            dimension_semantics=("arbitrary", "arbitrary"), vmem_limit_bytes=VMEM_LIMIT),
        name="chunk_attn",
    )(qT, k, k, vT, vT, bias)


def kernel(x, positions, attn_pre_g, attn_post_g, ffn_pre_g, ffn_post_g, ffn_w_gate,
           ffn_w_up, ffn_w_down, mla_w_a, mla_g_q, mla_w_uq, mla_g_kv, mla_w_ukv, mla_w_o,
           kv_src_g, w_kv_shared, b_w_q, b_rel_table, b_w_o):
    batch, seq, d = x.shape
    T = batch * seq
    assert attn_pre_g.shape[0] == 2 and mla_w_a.shape[0] == 1 and b_w_q.shape[0] == 1
    assert seq % GROUP == 0 and seq % MLA_TQ == 0 and T % PROJ_ROWS == 0
    row = lambda g: g.reshape(1, -1).astype(F32)

    x2 = x.reshape(T, d)
    pos_row = positions.reshape(1, T).astype(F32)
    inv = 1.0 / (ROPE_THETA ** (jnp.arange(0, ROPE_DIM, 2, dtype=F32) / ROPE_DIM))
    inv_col = inv.reshape(ROPE_DIM // 2, 1)

    wa = mla_w_a[0][:, :Q_LORA + KV_LORA].astype(BF16)
    wkrT = jnp.pad(mla_w_a[0][:, Q_LORA + KV_LORA:].T, ((0, LANES - ROPE_DIM), (0, 0))).astype(BF16)
    wuq = mla_w_uq[0].reshape(Q_LORA, MLA_HEADS, NOPE_DIM + ROPE_DIM)
    wuq = jnp.pad(wuq, ((0, 0), (0, 0), (0, MLA_HEAD_PAD - NOPE_DIM - ROPE_DIM)))
    wuqT = wuq.reshape(Q_LORA, MLA_HEADS * MLA_HEAD_PAD).T.astype(BF16)
    wukv = mla_w_ukv[0].reshape(KV_LORA, MLA_HEADS, NOPE_DIM + V_DIM)
    wuk = wukv[:, :, :NOPE_DIM].reshape(KV_LORA, MLA_HEADS * NOPE_DIM).astype(BF16)
    wuvT = wukv[:, :, NOPE_DIM:].reshape(KV_LORA, MLA_HEADS * V_DIM).T.astype(BF16)
    hd = B_HEADS * B_HEAD_DIM
    wk = w_kv_shared[:, :hd].astype(BF16)
    wvT = w_kv_shared[:, hd:].T.astype(BF16)
    wqT = b_w_q[0].T.astype(BF16)

    qT0, k0, vT0 = _mla_proj(x2, pos_row, inv_col, row(attn_pre_g[0]), wa, wkrT,
                             row(mla_g_q[0]), wuqT, row(mla_g_kv[0]), wuk, wuvT)
    o0 = _mla_attn(qT0, k0, vT0, batch, seq)
    x2 = _block_tail(o0, x2, mla_w_o[0].astype(BF16), row(attn_post_g[0]), row(ffn_pre_g[0]),
                     ffn_w_gate[0].astype(BF16), ffn_w_up[0].astype(BF16),
                     ffn_w_down[0].astype(BF16), row(ffn_post_g[0]))

    qT1, k1, vT1 = _kvq_proj(x2, row(kv_src_g), row(attn_pre_g[1]), wqT, wk, wvT)
    bias = jnp.transpose(_rel_bias(b_rel_table[0]), (1, 0, 2))
    o1 = _chunk_attn(qT1, k1, vT1, bias, batch, seq)
    x2 = _block_tail(o1, x2, b_w_o[0].astype(BF16), row(attn_post_g[1]), row(ffn_pre_g[1]),
                     ffn_w_gate[1].astype(BF16), ffn_w_up[1].astype(BF16),
                     ffn_w_down[1].astype(BF16), row(ffn_post_g[1]))
    return x2.reshape(batch, seq, d)
```

```python
import functools
import math

import jax
import jax.numpy as jnp
from jax import lax
from jax.experimental import pallas as pl
from jax.experimental.pallas import tpu as pltpu

F32 = jnp.float32
BF16 = jnp.bfloat16

CHUNK = 64
MLA_HEADS = 8
Q_LORA = 384
KV_LORA = 256
NOPE_DIM = 128
ROPE_DIM = 64
V_DIM = 128
ROPE_THETA = 10000.0
B_HEADS = 16
B_HEAD_DIM = 64
LEFT_CHUNKS = 8
BAND = (LEFT_CHUNKS + 1) * CHUNK
MAX_REL = 256
REL_TABLE = MAX_REL + CHUNK
EPS = 1e-6

LANES = 128
MLA_HEAD_PAD = 2 * LANES
NEG = -1e30

PROJ_ROWS = 512
TAIL_ROWS = 512
TAIL_SPLIT = 2
MLA_TQ = 512
MLA_TK = 256
SUM_ROWS = 16
QUAD = 4 * CHUNK
QUAD_BAND = BAND + 3 * CHUNK
GROUP = 2 * QUAD
HEAD_GROUP = 2
PHASE_HEADS = 8
BIAS_ROLL = 1024
VMEM_LIMIT = 56 * 1024 * 1024


def _rms(x, g):
    ms = jnp.mean(x * x, axis=-1, keepdims=True)
    return x * lax.rsqrt(ms + EPS) * g


def _dot(a, b):
    return jnp.dot(a, b, preferred_element_type=F32)


def _dot_nt(a, b):
    return lax.dot_general(a, b, (((1,), (1,)), ((), ())), preferred_element_type=F32)


def _mla_proj_kernel(x_ref, posr_ref, invc_ref, gpre_ref, wa_ref, wkrT_ref, gq_ref,
                     wuqT_ref, gkv_ref, wuk_ref, wuvT_ref, qT_ref, kn_ref, kr_ref, vT_ref, *,
                     scale):
    h = _rms(x_ref[...], gpre_ref[...]).astype(BF16)
    a = _dot(h, wa_ref[...])
    cqn = _rms(a[:, :Q_LORA], gq_ref[...]).astype(BF16)
    ckvn = _rms(a[:, Q_LORA:], gkv_ref[...]).astype(BF16)
    half = ROPE_DIM // 2

    angT = invc_ref[...] * posr_ref[...]
    cosT, sinT = jnp.cos(angT), jnp.sin(angT)

    krT = _dot_nt(wkrT_ref[...], h)
    k1, k2 = krT[:half], krT[half:ROPE_DIM]
    kr_rotT = jnp.concatenate(
        [k1 * cosT - k2 * sinT, k2 * cosT + k1 * sinT,
         jnp.zeros((LANES - ROPE_DIM, krT.shape[1]), F32)], axis=0)
    kr_ref[...] = pltpu.bitcast(kr_rotT.T.astype(BF16), jnp.uint32)
    kn_ref[...] = pltpu.bitcast(_dot(ckvn, wuk_ref[...]).astype(BF16), jnp.uint32)

    cosT, sinT = cosT * scale, sinT * scale
    qT = _dot_nt(wuqT_ref[...], cqn)
    for hd in range(MLA_HEADS):
        lo = hd * MLA_HEAD_PAD
        t1 = qT[lo + NOPE_DIM:lo + NOPE_DIM + half]
        t2 = qT[lo + NOPE_DIM + half:lo + NOPE_DIM + ROPE_DIM]
        qT_ref[lo:lo + NOPE_DIM] = (qT[lo:lo + NOPE_DIM] * scale).astype(BF16)
        qT_ref[lo + NOPE_DIM:lo + NOPE_DIM + half] = (t1 * cosT - t2 * sinT).astype(BF16)
        qT_ref[lo + NOPE_DIM + half:lo + NOPE_DIM + ROPE_DIM] = (t2 * cosT + t1 * sinT).astype(BF16)
        qT_ref[lo + NOPE_DIM + ROPE_DIM:lo + MLA_HEAD_PAD] = jnp.zeros(
            (MLA_HEAD_PAD - NOPE_DIM - ROPE_DIM, qT.shape[1]), BF16)

    vT = _dot_nt(wuvT_ref[...], ckvn).astype(BF16)
    for t in range(vT_ref.shape[0]):
        vT_ref[t] = vT[:, t * MLA_TK:(t + 1) * MLA_TK]


def _mla_proj(x2, pos_row, inv_col, gpre, wa, wkrT, gq, wuqT, gkv, wuk, wuvT):
    T, D = x2.shape
    tm = PROJ_ROWS
    const = lambda i: (0, 0)
    full = lambda a: pl.BlockSpec(a.shape, const)
    scale = float((NOPE_DIM + ROPE_DIM) ** -0.5 * math.log2(math.e))
    return pl.pallas_call(
        functools.partial(_mla_proj_kernel, scale=scale),
        grid=(T // tm,),
        in_specs=[pl.BlockSpec((tm, D), lambda i: (i, 0)),
                  pl.BlockSpec((1, tm), lambda i: (0, i)),
                  full(inv_col), full(gpre), full(wa), full(wkrT), full(gq), full(wuqT),
                  full(gkv), full(wuk), full(wuvT)],
        out_specs=[pl.BlockSpec((MLA_HEADS * MLA_HEAD_PAD, tm), lambda i: (0, i)),
                   pl.BlockSpec((tm // 2, MLA_HEADS * NOPE_DIM), lambda i: (i, 0)),
                   pl.BlockSpec((tm // 2, LANES), lambda i: (i, 0)),
                   pl.BlockSpec((tm // MLA_TK, MLA_HEADS * V_DIM, MLA_TK), lambda i: (i, 0, 0))],
        out_shape=[jax.ShapeDtypeStruct((MLA_HEADS * MLA_HEAD_PAD, T), BF16),
                   jax.ShapeDtypeStruct((T // 2, MLA_HEADS * NOPE_DIM), jnp.uint32),
                   jax.ShapeDtypeStruct((T // 2, LANES), jnp.uint32),
                   jax.ShapeDtypeStruct((T // MLA_TK, MLA_HEADS * V_DIM, MLA_TK), BF16)],
        compiler_params=pltpu.CompilerParams(
            dimension_semantics=("arbitrary",), vmem_limit_bytes=VMEM_LIMIT),
        name="mla_proj",
    )(x2, pos_row, inv_col, gpre, wa, wkrT, gq, wuqT, gkv, wuk, wuvT)


def _mla_attn_kernel(qT_ref, kn_ref, kr_ref, vT_ref, oT_ref, m_s, l_s, a_s, acc_s, s_s):
    qi = pl.program_id(1)
    n_full = qi * (MLA_TQ // MLA_TK)
    m_s[...] = jnp.full(m_s.shape, NEG, F32)
    l_s[...] = jnp.zeros(l_s.shape, F32)
    a_s[...] = jnp.ones(a_s.shape, F32)
    acc_s[...] = jnp.zeros(acc_s.shape, F32)
    s_s[MLA_HEADS:] = jnp.full((MLA_HEADS, MLA_TK, MLA_TQ), -jnp.inf, F32)
    ones = jnp.ones((SUM_ROWS, MLA_TK), BF16)

    def score_pass(j, diag):
        slot = (j & 1) * MLA_HEADS
        half = MLA_TK // 2
        rows = pl.ds(pl.multiple_of(j * half, half), half)
        k_rope = pltpu.bitcast(kr_ref[rows, :], BF16)
        for hd in range(MLA_HEADS):
            k_nope = pltpu.bitcast(kn_ref[rows, hd * NOPE_DIM:(hd + 1) * NOPE_DIM], BF16)
            k = jnp.concatenate([k_nope, k_rope], axis=1)
            qT = qT_ref[hd * MLA_HEAD_PAD:(hd + 1) * MLA_HEAD_PAD, :]
            sT = _dot(k, qT)
            if diag is not None:
                kc = lax.broadcasted_iota(jnp.int32, sT.shape, 0) // CHUNK + diag * (MLA_TK // CHUNK)
                qc = lax.broadcasted_iota(jnp.int32, sT.shape, 1) // CHUNK
                sT = jnp.where(kc <= qc, sT, NEG)
            s_s[slot + hd] = sT
            m = m_s[hd]
            m_new = jnp.maximum(m, jnp.max(sT, axis=0, keepdims=True))
            a_s[hd] = jnp.exp2(m - m_new)
            m_s[hd] = m_new

    def value_pass(j):
        slot = (j & 1) * MLA_HEADS
        jv = jnp.maximum(j, 0)
        for hd in range(MLA_HEADS):
            p = jnp.exp2(s_s[slot + hd] - m_s[hd]).astype(BF16)
            lhs = jnp.concatenate([vT_ref[jv, hd * V_DIM:(hd + 1) * V_DIM, :], ones], axis=0)
            pv = _dot(lhs, p)
            a = a_s[hd]
            acc_s[hd] = a * acc_s[hd] + pv[:V_DIM]
            l_s[hd] = a * l_s[hd] + pv[V_DIM:V_DIM + 1]

    def steady(j, carry):
        value_pass(j - 1)
        score_pass(j, None)
        return carry

    lax.fori_loop(0, n_full, steady, 0)
    for d in range(MLA_TQ // MLA_TK):
        value_pass(n_full + d - 1)
        score_pass(n_full + d, d)
    value_pass(n_full + MLA_TQ // MLA_TK - 1)
    for hd in range(MLA_HEADS):
        oT_ref[hd * V_DIM:(hd + 1) * V_DIM, :] = (acc_s[hd] * (1.0 / l_s[hd])).astype(BF16)


def _mla_attn(qT, kn, kr, vT3, batch, seq):
    assert MLA_TQ % MLA_TK == 0 and MLA_TK % CHUNK == 0
    T = qT.shape[1]
    nq = seq // MLA_TQ
    nkt = seq // MLA_TK
    return pl.pallas_call(
        _mla_attn_kernel,
        grid=(batch, nq),
        in_specs=[pl.BlockSpec((MLA_HEADS * MLA_HEAD_PAD, MLA_TQ), lambda b, i: (0, b * nq + i)),
                  pl.BlockSpec((seq // 2, MLA_HEADS * NOPE_DIM), lambda b, i: (b, 0)),
                  pl.BlockSpec((seq // 2, LANES), lambda b, i: (b, 0)),
                  pl.BlockSpec((nkt, MLA_HEADS * V_DIM, MLA_TK), lambda b, i: (b, 0, 0),
                               pipeline_mode=pl.Buffered(1))],
        out_specs=pl.BlockSpec((MLA_HEADS * V_DIM, MLA_TQ), lambda b, i: (0, b * nq + i)),
        out_shape=jax.ShapeDtypeStruct((MLA_HEADS * V_DIM, T), BF16),
        scratch_shapes=[pltpu.VMEM((MLA_HEADS, 1, MLA_TQ), F32),
                        pltpu.VMEM((MLA_HEADS, 1, MLA_TQ), F32),
                        pltpu.VMEM((MLA_HEADS, 1, MLA_TQ), F32),
                        pltpu.VMEM((MLA_HEADS, V_DIM, MLA_TQ), F32),
                        pltpu.VMEM((2 * MLA_HEADS, MLA_TK, MLA_TQ), F32)],
        compiler_params=pltpu.CompilerParams(
            dimension_semantics=("arbitrary", "arbitrary"),
            vmem_limit_bytes=VMEM_LIMIT),
        name="mla_attn",
    )(qT, kn, kr, vT3)


def _block_tail_kernel(oT_ref, x_ref, wo_ref, gpost_ref, gfpre_ref, wg_ref, wu_ref,
                       wd_ref, gfpost_ref, out_ref):
    sub = out_ref.shape[0] // TAIL_SPLIT
    blocks = [slice(i * sub, (i + 1) * sub) for i in range(TAIL_SPLIT)]
    y = [lax.dot_general(oT_ref[:, r], wo_ref[...], (((0,), (0,)), ((), ())),
                         preferred_element_type=F32) for r in blocks]
    x1 = [x_ref[r, :] + _rms(yi, gpost_ref[...]) for r, yi in zip(blocks, y)]
    h = [_rms(xi, gfpre_ref[...]).astype(BF16) for xi in x1]
    g = [_dot(hi, wg_ref[...]) for hi in h]
    u = [_dot(hi, wu_ref[...]) for hi in h]
    a = [(gi * (1.0 / (1.0 + jnp.exp(-gi))) * ui).astype(BF16) for gi, ui in zip(g, u)]
    f = [_dot(ai, wd_ref[...]) for ai in a]
    for r, xi, fi in zip(blocks, x1, f):
        out_ref[r, :] = xi + _rms(fi, gfpost_ref[...])


def _block_tail(oT, x2, wo, gpost, gfpre, wg, wu, wd, gfpost):
    T, D = x2.shape
    tm = TAIL_ROWS
    const = lambda i: (0, 0)
    full = lambda a: pl.BlockSpec(a.shape, const, pipeline_mode=pl.Buffered(1))
    row = lambda a: pl.BlockSpec((tm, a.shape[1]), lambda i: (i, 0))
    return pl.pallas_call(
        _block_tail_kernel,
        grid=(T // tm,),
        in_specs=[pl.BlockSpec((oT.shape[0], tm), lambda i: (0, i)), row(x2), full(wo),
                  full(gpost), full(gfpre), full(wg), full(wu), full(wd), full(gfpost)],
        out_specs=pl.BlockSpec((tm, D), lambda i: (i, 0)),
        out_shape=jax.ShapeDtypeStruct((T, D), F32),
        compiler_params=pltpu.CompilerParams(
            dimension_semantics=("arbitrary",), vmem_limit_bytes=VMEM_LIMIT),
        name="block_tail",
    )(oT, x2, wo, gpost, gfpre, wg, wu, wd, gfpost)


def _kvq_proj_kernel(x_ref, gsrc_ref, gpre_ref, wqT_ref, wk_ref, wvT_ref,
                     qT_ref, k_ref, vT_ref, *, scale):
    x = x_ref[...]
    xr = x * lax.rsqrt(jnp.mean(x * x, axis=-1, keepdims=True) + EPS)
    xs = (xr * gsrc_ref[...]).astype(BF16)
    xq = (xr * gpre_ref[...]).astype(BF16)
    qT_ref[...] = (_dot_nt(wqT_ref[...], xq) * scale).astype(BF16)
    k_ref[...] = pltpu.bitcast(_dot(xs, wk_ref[...]).astype(BF16), jnp.uint32)
    vT_ref[...] = _dot_nt(wvT_ref[...], xs).astype(BF16)


def _kvq_proj(x2, gsrc, gpre, wqT, wk, wvT):
    T, D = x2.shape
    tm = PROJ_ROWS
    hd = B_HEADS * B_HEAD_DIM
    const = lambda i: (0, 0)
    full = lambda a: pl.BlockSpec(a.shape, const)
    scale = float(B_HEAD_DIM ** -0.5 * math.log2(math.e))
    return pl.pallas_call(
        functools.partial(_kvq_proj_kernel, scale=scale),
        grid=(T // tm,),
        in_specs=[pl.BlockSpec((tm, D), lambda i: (i, 0)),
                  full(gsrc), full(gpre), full(wqT), full(wk), full(wvT)],
        out_specs=[pl.BlockSpec((hd, tm), lambda i: (0, i)),
                   pl.BlockSpec((tm // 2, hd), lambda i: (i, 0)),
                   pl.BlockSpec((hd, tm), lambda i: (0, i))],
        out_shape=[jax.ShapeDtypeStruct((hd, T), BF16),
                   jax.ShapeDtypeStruct((T // 2, hd), jnp.uint32),
                   jax.ShapeDtypeStruct((hd, T), BF16)],
        compiler_params=pltpu.CompilerParams(
            dimension_semantics=("arbitrary",), vmem_limit_bytes=VMEM_LIMIT),
        name="kvq_proj",
    )(x2, gsrc, gpre, wqT, wk, wvT)


def _rel_bias_kernel(tab_ref, out_ref, g_s):
    tab = tab_ref[...]
    t_hi = tab.astype(BF16)
    rem = tab - t_hi.astype(F32)
    t_mid = rem.astype(BF16)
    t_lo = (rem - t_mid.astype(F32)).astype(BF16)
    row = lax.broadcasted_iota(jnp.int32, (REL_TABLE, BIAS_ROLL), 0)
    lane = lax.broadcasted_iota(jnp.int32, (REL_TABLE, BIAS_ROLL), 1)
    dist = lane - (QUAD - 1)
    idx = jnp.clip(dist, -(CHUNK - 1), MAX_REL) + (CHUNK - 1)
    onehot = jnp.where(idx == row, 1.0, 0.0).astype(BF16)
    g_s[...] = (_dot(t_hi, onehot) + _dot(t_mid, onehot) + _dot(t_lo, onehot)) * math.log2(math.e)
    q_chunk = lax.broadcasted_iota(jnp.int32, (1, QUAD), 1) // CHUNK + LEFT_CHUNKS

    def body(c, carry):
        rolled = pltpu.roll(g_s[...], (c + (BIAS_ROLL - QUAD_BAND + 1)) % BIAS_ROLL, 1)
        back = q_chunk - c // CHUNK
        valid = (back >= 0) & (back <= LEFT_CHUNKS)
        out_ref[c] = jnp.where(valid, rolled[:, :QUAD], NEG)
        return carry

    lax.fori_loop(0, QUAD_BAND, body, 0, unroll=16)


def _rel_bias(table):
    return pl.pallas_call(
        _rel_bias_kernel,
        out_shape=jax.ShapeDtypeStruct((QUAD_BAND, B_HEADS, QUAD), F32),
        scratch_shapes=[pltpu.VMEM((B_HEADS, BIAS_ROLL), F32)],
        name="rel_bias",
    )(table)


def _chunk_attn_kernel(qT_ref, kp_ref, kc_ref, vTp_ref, vTc_ref, bias_ref, o_ref,
                       k4_s, vTq_s, qTm_s, s_s, m_s, oT_s):
    g = pl.program_id(1)
    slab = HEAD_GROUP * B_HEAD_DIM
    n_hg = B_HEADS // HEAD_GROUP
    n_phase = (GROUP // QUAD) * (B_HEADS // PHASE_HEADS)
    half = GROUP // 2
    for hg in range(n_hg):
        lanes = slice(hg * slab, (hg + 1) * slab)
        k4_s[hg, :half, :] = kp_ref[:, lanes]
        k4_s[hg, half:, :] = kc_ref[:, lanes]
    vTq_s[0, :, :GROUP] = vTp_ref[...]
    vTq_s[0, :, GROUP:] = vTc_ref[:, :QUAD]
    vTq_s[1, :, :QUAD] = vTp_ref[:, QUAD:]
    vTq_s[1, :, QUAD:] = vTc_ref[...]
    @pl.when(g == 0)
    def _():
        qTm_s[...] = jnp.zeros(qTm_s.shape, BF16)

    for h in range(B_HEADS):
        own = slice((h % HEAD_GROUP) * B_HEAD_DIM, (h % HEAD_GROUP + 1) * B_HEAD_DIM)
        for qd in range(GROUP // QUAD):
            qTm_s[h, qd, own, :] = qT_ref[h * B_HEAD_DIM:(h + 1) * B_HEAD_DIM,
                                          qd * QUAD:(qd + 1) * QUAD]
    ones = jnp.ones((SUM_ROWS, QUAD_BAND), BF16)

    def phase_ids(t):
        per_quad = B_HEADS // PHASE_HEADS
        slot, qd, part = (t & 1) * PHASE_HEADS, t >> (per_quad.bit_length() - 1), t & (per_quad - 1)
        return slot, qd, part * PHASE_HEADS, part * (PHASE_HEADS // HEAD_GROUP)

    def score_phase(t, first_group):
        slot, qd, head0, hg0 = phase_ids(t)
        start = qd * (QUAD // 2)
        if not isinstance(start, int):
            start = pl.multiple_of(start, QUAD // 2)
        for i in range(PHASE_HEADS):
            h = head0 + i
            kb = pltpu.bitcast(
                k4_s[hg0 + i // HEAD_GROUP, pl.ds(start, QUAD_BAND // 2), :], BF16)
            s = _dot(kb, qTm_s[h, qd]) + bias_ref[h]
            if first_group:
                key = lax.broadcasted_iota(jnp.int32, (QUAD_BAND, 1), 0)
                s = jnp.where(key >= GROUP - qd * QUAD, s, NEG)
            s_s[slot + i] = s
            m_s[slot + i] = jnp.max(s, axis=0, keepdims=True)

    def value_phase(t):
        slot, qd, head0, _ = phase_ids(t)
        for i in range(PHASE_HEADS):
            row0 = (head0 + i) * B_HEAD_DIM
            if not isinstance(row0, int):
                row0 = pl.multiple_of(row0, B_HEAD_DIM)
            p = jnp.exp2(s_s[slot + i] - m_s[slot + i]).astype(BF16)
            lhs = jnp.concatenate([vTq_s[qd, pl.ds(row0, B_HEAD_DIM), :], ones], axis=0)
            r = _dot(lhs, p)
            oT_s[qd, pl.ds(row0, B_HEAD_DIM), :] = (
                r[:B_HEAD_DIM] * (1.0 / r[B_HEAD_DIM:B_HEAD_DIM + 1]))

    def run(first_group):
        score_phase(0, first_group)

        def steady(t, carry):
            value_phase(t - 1)
            score_phase(t, first_group)
            return carry

        lax.fori_loop(1, n_phase, steady, 0)
        value_phase(n_phase - 1)

    pl.when(g == 0)(lambda: run(True))
    pl.when(g != 0)(lambda: run(False))
    for qd in range(GROUP // QUAD):
        o_ref[:, qd * QUAD:(qd + 1) * QUAD] = oT_s[qd].astype(BF16)


def _chunk_attn(qT, k, vT, bias, batch, seq):
    hd, T = qT.shape
    ng = seq // GROUP
    n_quad = GROUP // QUAD
    prev = lambda b, g: b * ng + jnp.maximum(g - 1, 0)
    cur = lambda b, g: b * ng + g
    return pl.pallas_call(
        _chunk_attn_kernel,
        grid=(batch, ng),
        in_specs=[pl.BlockSpec((hd, GROUP), lambda b, g: (0, cur(b, g))),
                  pl.BlockSpec((GROUP // 2, hd), lambda b, g: (prev(b, g), 0)),
                  pl.BlockSpec((GROUP // 2, hd), lambda b, g: (cur(b, g), 0)),
                  pl.BlockSpec((hd, GROUP), lambda b, g: (0, prev(b, g))),
                  pl.BlockSpec((hd, GROUP), lambda b, g: (0, cur(b, g))),
                  pl.BlockSpec(bias.shape, lambda b, g: (0, 0, 0),
                               pipeline_mode=pl.Buffered(1))],
        out_specs=pl.BlockSpec((hd, GROUP), lambda b, g: (0, cur(b, g))),
        out_shape=jax.ShapeDtypeStruct((hd, T), BF16),
        scratch_shapes=[pltpu.VMEM((B_HEADS // HEAD_GROUP, GROUP, HEAD_GROUP * B_HEAD_DIM),
                                   jnp.uint32),
                        pltpu.VMEM((n_quad, hd, QUAD_BAND), BF16),
                        pltpu.VMEM((B_HEADS, n_quad, HEAD_GROUP * B_HEAD_DIM, QUAD), BF16),
                        pltpu.VMEM((2 * PHASE_HEADS, QUAD_BAND, QUAD), F32),
                        pltpu.VMEM((2 * PHASE_HEADS, 1, QUAD), F32),
                        pltpu.VMEM((n_quad, hd, QUAD), F32)],
        compiler_params=pltpu.CompilerParams(
            dimension_semantics=("arbitrary", "arbitrary"), vmem_limit_bytes=VMEM_LIMIT),
        name="chunk_attn",
    )(qT, k, k, vT, vT, bias)


def kernel(x, positions, attn_pre_g, attn_post_g, ffn_pre_g, ffn_post_g, ffn_w_gate,
           ffn_w_up, ffn_w_down, mla_w_a, mla_g_q, mla_w_uq, mla_g_kv, mla_w_ukv, mla_w_o,
           kv_src_g, w_kv_shared, b_w_q, b_rel_table, b_w_o):
    batch, seq, d = x.shape
    T = batch * seq
    assert attn_pre_g.shape[0] == 2 and mla_w_a.shape[0] == 1 and b_w_q.shape[0] == 1
    assert seq % GROUP == 0 and seq % MLA_TQ == 0 and T % PROJ_ROWS == 0
    row = lambda g: g.reshape(1, -1).astype(F32)

    x2 = x.reshape(T, d)
    pos_row = positions.reshape(1, T).astype(F32)
    inv = 1.0 / (ROPE_THETA ** (jnp.arange(0, ROPE_DIM, 2, dtype=F32) / ROPE_DIM))
    inv_col = inv.reshape(ROPE_DIM // 2, 1)

    wa = mla_w_a[0][:, :Q_LORA + KV_LORA].astype(BF16)
    wkrT = jnp.pad(mla_w_a[0][:, Q_LORA + KV_LORA:].T, ((0, LANES - ROPE_DIM), (0, 0))).astype(BF16)
    wuq = mla_w_uq[0].reshape(Q_LORA, MLA_HEADS, NOPE_DIM + ROPE_DIM)
    wuq = jnp.pad(wuq, ((0, 0), (0, 0), (0, MLA_HEAD_PAD - NOPE_DIM - ROPE_DIM)))
    wuqT = wuq.reshape(Q_LORA, MLA_HEADS * MLA_HEAD_PAD).T.astype(BF16)
    wukv = mla_w_ukv[0].reshape(KV_LORA, MLA_HEADS, NOPE_DIM + V_DIM)
    wuk = wukv[:, :, :NOPE_DIM].reshape(KV_LORA, MLA_HEADS * NOPE_DIM).astype(BF16)
    wuvT = wukv[:, :, NOPE_DIM:].reshape(KV_LORA, MLA_HEADS * V_DIM).T.astype(BF16)
    hd = B_HEADS * B_HEAD_DIM
    wk = w_kv_shared[:, :hd].astype(BF16)
    wvT = w_kv_shared[:, hd:].T.astype(BF16)
    wqT = b_w_q[0].T.astype(BF16)

    qT0, kn0, kr0, vT0 = _mla_proj(x2, pos_row, inv_col, row(attn_pre_g[0]), wa, wkrT,
                                   row(mla_g_q[0]), wuqT, row(mla_g_kv[0]), wuk, wuvT)
    o0 = _mla_attn(qT0, kn0, kr0, vT0, batch, seq)
    x2 = _block_tail(o0, x2, mla_w_o[0].astype(BF16), row(attn_post_g[0]), row(ffn_pre_g[0]),
                     ffn_w_gate[0].astype(BF16), ffn_w_up[0].astype(BF16),
                     ffn_w_down[0].astype(BF16), row(ffn_post_g[0]))

    qT1, k1, vT1 = _kvq_proj(x2, row(kv_src_g), row(attn_pre_g[1]), wqT, wk, wvT)
    bias = jnp.transpose(_rel_bias(b_rel_table[0]), (1, 0, 2))
    o1 = _chunk_attn(qT1, k1, vT1, bias, batch, seq)
    x2 = _block_tail(o1, x2, b_w_o[0].astype(BF16), row(attn_post_g[1]), row(ffn_pre_g[1]),
                     ffn_w_gate[1].astype(BF16), ffn_w_up[1].astype(BF16),
                     ffn_w_down[1].astype(BF16), row(ffn_post_g[1]))
    return x2.reshape(batch, seq, d)
```

```python
import functools
import math

import jax
import jax.numpy as jnp
from jax import lax
from jax.experimental import pallas as pl
from jax.experimental.pallas import tpu as pltpu

F32 = jnp.float32
BF16 = jnp.bfloat16

CHUNK = 64
MLA_HEADS = 8
Q_LORA = 384
KV_LORA = 256
NOPE_DIM = 128
ROPE_DIM = 64
V_DIM = 128
ROPE_THETA = 10000.0
B_HEADS = 16
B_HEAD_DIM = 64
LEFT_CHUNKS = 8
BAND = (LEFT_CHUNKS + 1) * CHUNK
MAX_REL = 256
REL_TABLE = MAX_REL + CHUNK
EPS = 1e-6

LANES = 128
SUBLANES = 8
MLA_HEAD_PAD = 2 * LANES
NEG = -1e30

PROJ_ROWS = 512
TAIL_ROWS = 512
TAIL_SPLIT = 2
MLA_TQ = 512
MLA_TK = 256
SUM_ROWS = 16
QUAD = 4 * CHUNK
QUAD_BAND = BAND + 3 * CHUNK
GROUP = 2 * QUAD
HEAD_GROUP = 2
PHASE_HEADS = 8
BIAS_ROLL = 1024
VMEM_LIMIT = 56 * 1024 * 1024


def _rms(x, g):
    ms = jnp.mean(x * x, axis=-1, keepdims=True)
    return x * lax.rsqrt(ms + EPS) * g


def _dot(a, b):
    return jnp.dot(a, b, preferred_element_type=F32)


def _dot_nt(a, b):
    return lax.dot_general(a, b, (((1,), (1,)), ((), ())), preferred_element_type=F32)


def _mla_proj_kernel(x_ref, posr_ref, invc_ref, gpre_ref, wa_ref, wkrT_ref, gq_ref,
                     wuqT_ref, gkv_ref, wuk_ref, wuvT_ref, qT_ref, kn_ref, kr_ref, vT_ref, *,
                     scale):
    h = _rms(x_ref[...], gpre_ref[...]).astype(BF16)
    a = _dot(h, wa_ref[...])
    cqn = _rms(a[:, :Q_LORA], gq_ref[...]).astype(BF16)
    ckvn = _rms(a[:, Q_LORA:], gkv_ref[...]).astype(BF16)
    half = ROPE_DIM // 2

    angT = invc_ref[...] * posr_ref[...]
    cosT, sinT = jnp.cos(angT), jnp.sin(angT)

    krT = _dot_nt(wkrT_ref[...], h)
    k1, k2 = krT[:half], krT[half:ROPE_DIM]
    kr_rotT = jnp.concatenate(
        [k1 * cosT - k2 * sinT, k2 * cosT + k1 * sinT,
         jnp.zeros((LANES - ROPE_DIM, krT.shape[1]), F32)], axis=0)
    kr_ref[...] = pltpu.bitcast(kr_rotT.T.astype(BF16), jnp.uint32)
    kn_ref[...] = pltpu.bitcast(_dot(ckvn, wuk_ref[...]).astype(BF16), jnp.uint32)

    cosT, sinT = cosT * scale, sinT * scale
    qT = _dot_nt(wuqT_ref[...], cqn)
    for hd in range(MLA_HEADS):
        lo = hd * MLA_HEAD_PAD
        t1 = qT[lo + NOPE_DIM:lo + NOPE_DIM + half]
        t2 = qT[lo + NOPE_DIM + half:lo + NOPE_DIM + ROPE_DIM]
        qT_ref[lo:lo + NOPE_DIM] = (qT[lo:lo + NOPE_DIM] * scale).astype(BF16)
        qT_ref[lo + NOPE_DIM:lo + NOPE_DIM + half] = (t1 * cosT - t2 * sinT).astype(BF16)
        qT_ref[lo + NOPE_DIM + half:lo + NOPE_DIM + ROPE_DIM] = (t2 * cosT + t1 * sinT).astype(BF16)
        qT_ref[lo + NOPE_DIM + ROPE_DIM:lo + MLA_HEAD_PAD] = jnp.zeros(
            (MLA_HEAD_PAD - NOPE_DIM - ROPE_DIM, qT.shape[1]), BF16)

    vT = _dot_nt(wuvT_ref[...], ckvn).astype(BF16)
    for t in range(vT_ref.shape[0]):
        vT_ref[t] = vT[:, t * MLA_TK:(t + 1) * MLA_TK]


def _mla_proj(x2, pos_row, inv_col, gpre, wa, wkrT, gq, wuqT, gkv, wuk, wuvT):
    T, D = x2.shape
    tm = PROJ_ROWS
    const = lambda i: (0, 0)
    full = lambda a: pl.BlockSpec(a.shape, const)
    scale = float((NOPE_DIM + ROPE_DIM) ** -0.5 * math.log2(math.e))
    return pl.pallas_call(
        functools.partial(_mla_proj_kernel, scale=scale),
        grid=(T // tm,),
        in_specs=[pl.BlockSpec((tm, D), lambda i: (i, 0)),
                  pl.BlockSpec((1, tm), lambda i: (0, i)),
                  full(inv_col), full(gpre), full(wa), full(wkrT), full(gq), full(wuqT),
                  full(gkv), full(wuk), full(wuvT)],
        out_specs=[pl.BlockSpec((MLA_HEADS * MLA_HEAD_PAD, tm), lambda i: (0, i)),
                   pl.BlockSpec((tm // 2, MLA_HEADS * NOPE_DIM), lambda i: (i, 0)),
                   pl.BlockSpec((tm // 2, LANES), lambda i: (i, 0)),
                   pl.BlockSpec((tm // MLA_TK, MLA_HEADS * V_DIM, MLA_TK), lambda i: (i, 0, 0))],
        out_shape=[jax.ShapeDtypeStruct((MLA_HEADS * MLA_HEAD_PAD, T), BF16),
                   jax.ShapeDtypeStruct((T // 2, MLA_HEADS * NOPE_DIM), jnp.uint32),
                   jax.ShapeDtypeStruct((T // 2, LANES), jnp.uint32),
                   jax.ShapeDtypeStruct((T // MLA_TK, MLA_HEADS * V_DIM, MLA_TK), BF16)],
        compiler_params=pltpu.CompilerParams(
            dimension_semantics=("arbitrary",), vmem_limit_bytes=VMEM_LIMIT),
        name="mla_proj",
    )(x2, pos_row, inv_col, gpre, wa, wkrT, gq, wuqT, gkv, wuk, wuvT)


def _mla_attn_kernel(qT_ref, kn_ref, kr_ref, vT_ref, oT_ref, m_s, l_s, a_s, acc_s, s_s):
    qi = pl.program_id(1)
    n_full = qi * (MLA_TQ // MLA_TK)
    m_s[...] = jnp.full(m_s.shape, NEG, F32)
    l_s[...] = jnp.zeros(l_s.shape, F32)
    a_s[...] = jnp.ones(a_s.shape, F32)
    acc_s[...] = jnp.zeros(acc_s.shape, F32)
    s_s[MLA_HEADS:] = jnp.full((MLA_HEADS, MLA_TK, MLA_TQ), -jnp.inf, F32)
    ones = jnp.ones((SUM_ROWS, MLA_TK), BF16)

    def score_pass(j, diag):
        slot = (j & 1) * MLA_HEADS
        half = MLA_TK // 2
        rows = pl.ds(pl.multiple_of(j * half, half), half)
        q0 = 0 if diag is None else diag * MLA_TK
        k_rope = pltpu.bitcast(kr_ref[rows, :], BF16)
        for hd in range(MLA_HEADS):
            k_nope = pltpu.bitcast(kn_ref[rows, hd * NOPE_DIM:(hd + 1) * NOPE_DIM], BF16)
            k = jnp.concatenate([k_nope, k_rope], axis=1)
            qT = qT_ref[hd * MLA_HEAD_PAD:(hd + 1) * MLA_HEAD_PAD, q0:]
            sT = _dot(k, qT)
            if diag is not None:
                kc = lax.broadcasted_iota(jnp.int32, sT.shape, 0) // CHUNK
                qc = lax.broadcasted_iota(jnp.int32, sT.shape, 1) // CHUNK
                sT = jnp.where(kc <= qc, sT, NEG)
            s_s[slot + hd, :, q0:] = sT
            m = m_s[hd, :, q0:]
            m_new = jnp.maximum(m, jnp.max(sT, axis=0, keepdims=True))
            a_s[hd, :, q0:] = jnp.exp2(m - m_new)
            m_s[hd, :, q0:] = m_new

    def value_pass(j, diag=None):
        slot = (j & 1) * MLA_HEADS
        jv = jnp.maximum(j, 0)
        q0 = 0 if diag is None else diag * MLA_TK
        for hd in range(MLA_HEADS):
            p = jnp.exp2(s_s[slot + hd, :, q0:] - m_s[hd, :, q0:]).astype(BF16)
            lhs = jnp.concatenate([vT_ref[jv, hd * V_DIM:(hd + 1) * V_DIM, :], ones], axis=0)
            pv = _dot(lhs, p)
            a = a_s[hd, :, q0:]
            acc_s[hd, :, q0:] = a * acc_s[hd, :, q0:] + pv[:V_DIM]
            l_s[hd, :, q0:] = a * l_s[hd, :, q0:] + pv[V_DIM:V_DIM + 1]

    def steady(j, carry):
        value_pass(j - 1)
        score_pass(j, None)
        return carry

    lax.fori_loop(0, n_full, steady, 0)
    n_diag = MLA_TQ // MLA_TK
    for d in range(n_diag):
        value_pass(n_full + d - 1, d - 1 if d > 0 else None)
        score_pass(n_full + d, d)
    value_pass(n_full + n_diag - 1, n_diag - 1)
    for hd in range(MLA_HEADS):
        oT_ref[hd * V_DIM:(hd + 1) * V_DIM, :] = (acc_s[hd] * (1.0 / l_s[hd])).astype(BF16)


def _mla_attn(qT, kn, kr, vT3, batch, seq):
    assert MLA_TQ % MLA_TK == 0 and MLA_TK % CHUNK == 0
    T = qT.shape[1]
    nq = seq // MLA_TQ
    nkt = seq // MLA_TK
    return pl.pallas_call(
        _mla_attn_kernel,
        grid=(batch, nq),
        in_specs=[pl.BlockSpec((MLA_HEADS * MLA_HEAD_PAD, MLA_TQ), lambda b, i: (0, b * nq + i)),
                  pl.BlockSpec((seq // 2, MLA_HEADS * NOPE_DIM), lambda b, i: (b, 0)),
                  pl.BlockSpec((seq // 2, LANES), lambda b, i: (b, 0)),
                  pl.BlockSpec((nkt, MLA_HEADS * V_DIM, MLA_TK), lambda b, i: (b, 0, 0),
                               pipeline_mode=pl.Buffered(1))],
        out_specs=pl.BlockSpec((MLA_HEADS * V_DIM, MLA_TQ), lambda b, i: (0, b * nq + i)),
        out_shape=jax.ShapeDtypeStruct((MLA_HEADS * V_DIM, T), BF16),
        scratch_shapes=[pltpu.VMEM((MLA_HEADS, 1, MLA_TQ), F32),
                        pltpu.VMEM((MLA_HEADS, 1, MLA_TQ), F32),
                        pltpu.VMEM((MLA_HEADS, 1, MLA_TQ), F32),
                        pltpu.VMEM((MLA_HEADS, V_DIM, MLA_TQ), F32),
                        pltpu.VMEM((2 * MLA_HEADS, MLA_TK, MLA_TQ), F32)],
        compiler_params=pltpu.CompilerParams(
            dimension_semantics=("arbitrary", "arbitrary"),
            vmem_limit_bytes=VMEM_LIMIT),
        name="mla_attn",
    )(qT, kn, kr, vT3)


def _block_tail_kernel(oT_ref, x_ref, wo_ref, gpost_ref, gfpre_ref, wg_ref, wu_ref,
                       wd_ref, gfpost_ref, out_ref):
    sub = out_ref.shape[0] // TAIL_SPLIT
    blocks = [slice(i * sub, (i + 1) * sub) for i in range(TAIL_SPLIT)]
    y = [lax.dot_general(oT_ref[:, r], wo_ref[...], (((0,), (0,)), ((), ())),
                         preferred_element_type=F32) for r in blocks]
    x1 = [x_ref[r, :] + _rms(yi, gpost_ref[...]) for r, yi in zip(blocks, y)]
    h = [_rms(xi, gfpre_ref[...]).astype(BF16) for xi in x1]
    g = [_dot(hi, wg_ref[...]) for hi in h]
    u = [_dot(hi, wu_ref[...]) for hi in h]
    a = [(gi * (1.0 / (1.0 + jnp.exp(-gi))) * ui).astype(BF16) for gi, ui in zip(g, u)]
    f = [_dot(ai, wd_ref[...]) for ai in a]
    for r, xi, fi in zip(blocks, x1, f):
        out_ref[r, :] = xi + _rms(fi, gfpost_ref[...])


def _block_tail(oT, x2, wo, gpost, gfpre, wg, wu, wd, gfpost, layer):
    T, D = x2.shape
    tm = TAIL_ROWS
    const = lambda i: (0, 0)
    full = lambda a: pl.BlockSpec(a.shape, const, pipeline_mode=pl.Buffered(1))
    of_layer = lambda a: pl.BlockSpec((None,) + a.shape[1:], lambda i: (layer, 0, 0),
                                      pipeline_mode=pl.Buffered(1))
    row = lambda a: pl.BlockSpec((tm, a.shape[1]), lambda i: (i, 0))
    return pl.pallas_call(
        _block_tail_kernel,
        grid=(T // tm,),
        in_specs=[pl.BlockSpec((oT.shape[0], tm), lambda i: (0, i)), row(x2), full(wo),
                  full(gpost), full(gfpre), of_layer(wg), of_layer(wu), of_layer(wd),
                  full(gfpost)],
        out_specs=pl.BlockSpec((tm, D), lambda i: (i, 0)),
        out_shape=jax.ShapeDtypeStruct((T, D), F32),
        compiler_params=pltpu.CompilerParams(
            dimension_semantics=("arbitrary",), vmem_limit_bytes=VMEM_LIMIT),
        name="block_tail",
    )(oT, x2, wo, gpost, gfpre, wg, wu, wd, gfpost)


def _kvq_proj_kernel(x_ref, gsrc_ref, gpre_ref, wqT_ref, wk_ref, wvT_ref,
                     qT_ref, k_ref, vT_ref, *, scale):
    x = x_ref[...]
    xr = x * lax.rsqrt(jnp.mean(x * x, axis=-1, keepdims=True) + EPS)
    xs = (xr * gsrc_ref[...]).astype(BF16)
    xq = (xr * gpre_ref[...]).astype(BF16)
    qT_ref[...] = (_dot_nt(wqT_ref[...], xq) * scale).astype(BF16)
    k_ref[...] = pltpu.bitcast(_dot(xs, wk_ref[...]).astype(BF16), jnp.uint32)
    vT_ref[...] = _dot_nt(wvT_ref[...], xs).astype(BF16)


def _kvq_proj(x2, gsrc, gpre, wqT, wk, wvT):
    T, D = x2.shape
    tm = PROJ_ROWS
    hd = B_HEADS * B_HEAD_DIM
    const = lambda i: (0, 0)
    full = lambda a: pl.BlockSpec(a.shape, const)
    scale = float(B_HEAD_DIM ** -0.5 * math.log2(math.e))
    return pl.pallas_call(
        functools.partial(_kvq_proj_kernel, scale=scale),
        grid=(T // tm,),
        in_specs=[pl.BlockSpec((tm, D), lambda i: (i, 0)),
                  full(gsrc), full(gpre), full(wqT), full(wk), full(wvT)],
        out_specs=[pl.BlockSpec((hd, tm), lambda i: (0, i)),
                   pl.BlockSpec((tm // 2, hd), lambda i: (i, 0)),
                   pl.BlockSpec((hd, tm), lambda i: (0, i))],
        out_shape=[jax.ShapeDtypeStruct((hd, T), BF16),
                   jax.ShapeDtypeStruct((T // 2, hd), jnp.uint32),
                   jax.ShapeDtypeStruct((hd, T), BF16)],
        compiler_params=pltpu.CompilerParams(
            dimension_semantics=("arbitrary",), vmem_limit_bytes=VMEM_LIMIT),
        name="kvq_proj",
    )(x2, gsrc, gpre, wqT, wk, wvT)


def _rel_bias_kernel(tab_ref, out_ref, g_s):
    tab = tab_ref[...]
    t_hi = tab.astype(BF16)
    rem = tab - t_hi.astype(F32)
    t_mid = rem.astype(BF16)
    t_lo = (rem - t_mid.astype(F32)).astype(BF16)
    row = lax.broadcasted_iota(jnp.int32, (REL_TABLE, BIAS_ROLL), 0)
    lane = lax.broadcasted_iota(jnp.int32, (REL_TABLE, BIAS_ROLL), 1)
    dist = jnp.where(lane < QUAD, lane, lane - BIAS_ROLL) + LEFT_CHUNKS * CHUNK
    idx = jnp.clip(dist, -(CHUNK - 1), MAX_REL) + (CHUNK - 1)
    onehot = jnp.where(idx == row, 1.0, 0.0).astype(BF16)
    g_s[...] = (_dot(t_hi, onehot) + _dot(t_mid, onehot) + _dot(t_lo, onehot)) * math.log2(math.e)
    q_chunk = lax.broadcasted_iota(jnp.int32, (1, QUAD), 1) // CHUNK + LEFT_CHUNKS

    def body(cb, carry):
        c0 = pl.multiple_of(cb * SUBLANES, SUBLANES)
        back = q_chunk - cb // (CHUNK // SUBLANES)
        valid = (back >= 0) & (back <= LEFT_CHUNKS)
        for h in range(B_HEADS):
            rows = jnp.broadcast_to(g_s[h:h + 1, :], (SUBLANES, BIAS_ROLL))
            rolled = pltpu.roll(rows, c0, 1, stride=1, stride_axis=0)
            out_ref[h, pl.ds(c0, SUBLANES), :] = jnp.where(valid, rolled[:, :QUAD], NEG)
        return carry

    lax.fori_loop(0, QUAD_BAND // SUBLANES, body, 0)


def _rel_bias(table):
    return pl.pallas_call(
        _rel_bias_kernel,
        out_shape=jax.ShapeDtypeStruct((B_HEADS, QUAD_BAND, QUAD), F32),
        scratch_shapes=[pltpu.VMEM((B_HEADS, BIAS_ROLL), F32)],
        name="rel_bias",
    )(table)


def _chunk_attn_kernel(qT_ref, kp_ref, kc_ref, vTp_ref, vTc_ref, bias_ref, o_ref,
                       k4_s, vTq_s, qTm_s, s_s, m_s, oT_s):
    g = pl.program_id(1)
    slab = HEAD_GROUP * B_HEAD_DIM
    n_hg = B_HEADS // HEAD_GROUP
    n_phase = (GROUP // QUAD) * (B_HEADS // PHASE_HEADS)
    half = GROUP // 2
    for hg in range(n_hg):
        lanes = slice(hg * slab, (hg + 1) * slab)
        k4_s[hg, :half, :] = kp_ref[:, lanes]
        k4_s[hg, half:, :] = kc_ref[:, lanes]
    vTq_s[0, :, :GROUP] = vTp_ref[...]
    vTq_s[0, :, GROUP:] = vTc_ref[:, :QUAD]
    vTq_s[1, :, :QUAD] = vTp_ref[:, QUAD:]
    vTq_s[1, :, QUAD:] = vTc_ref[...]
    @pl.when(g == 0)
    def _():
        qTm_s[...] = jnp.zeros(qTm_s.shape, BF16)

    for h in range(B_HEADS):
        own = slice((h % HEAD_GROUP) * B_HEAD_DIM, (h % HEAD_GROUP + 1) * B_HEAD_DIM)
        for qd in range(GROUP // QUAD):
            qTm_s[h, qd, own, :] = qT_ref[h * B_HEAD_DIM:(h + 1) * B_HEAD_DIM,
                                          qd * QUAD:(qd + 1) * QUAD]
    ones = jnp.ones((SUM_ROWS, QUAD_BAND), BF16)

    def phase_ids(t):
        per_quad = B_HEADS // PHASE_HEADS
        slot, qd, part = (t & 1) * PHASE_HEADS, t >> (per_quad.bit_length() - 1), t & (per_quad - 1)
        return slot, qd, part * PHASE_HEADS, part * (PHASE_HEADS // HEAD_GROUP)

    def score_phase(t, first_group):
        slot, qd, head0, hg0 = phase_ids(t)
        start = qd * (QUAD // 2)
        if not isinstance(start, int):
            start = pl.multiple_of(start, QUAD // 2)
        for i in range(PHASE_HEADS):
            h = head0 + i
            kb = pltpu.bitcast(
                k4_s[hg0 + i // HEAD_GROUP, pl.ds(start, QUAD_BAND // 2), :], BF16)
            s = _dot(kb, qTm_s[h, qd]) + bias_ref[h]
            if first_group:
                key = lax.broadcasted_iota(jnp.int32, (QUAD_BAND, 1), 0)
                s = jnp.where(key >= GROUP - qd * QUAD, s, NEG)
            s_s[slot + i] = s
            m_s[slot + i] = jnp.max(s, axis=0, keepdims=True)

    def value_phase(t):
        slot, qd, head0, _ = phase_ids(t)
        for i in range(PHASE_HEADS):
            row0 = (head0 + i) * B_HEAD_DIM
            if not isinstance(row0, int):
                row0 = pl.multiple_of(row0, B_HEAD_DIM)
            p = jnp.exp2(s_s[slot + i] - m_s[slot + i]).astype(BF16)
            lhs = jnp.concatenate([vTq_s[qd, pl.ds(row0, B_HEAD_DIM), :], ones], axis=0)
            r = _dot(lhs, p)
            oT_s[qd, pl.ds(row0, B_HEAD_DIM), :] = (
                r[:B_HEAD_DIM] * (1.0 / r[B_HEAD_DIM:B_HEAD_DIM + 1]))

    def run(first_group):
        score_phase(0, first_group)

        def steady(t, carry):
            value_phase(t - 1)
            score_phase(t, first_group)
            return carry

        lax.fori_loop(1, n_phase, steady, 0)
        value_phase(n_phase - 1)

    pl.when(g == 0)(lambda: run(True))
    pl.when(g != 0)(lambda: run(False))
    for qd in range(GROUP // QUAD):
        o_ref[:, qd * QUAD:(qd + 1) * QUAD] = oT_s[qd].astype(BF16)


def _chunk_attn(qT, k, vT, bias, batch, seq):
    hd, T = qT.shape
    ng = seq // GROUP
    n_quad = GROUP // QUAD
    prev = lambda b, g: b * ng + jnp.maximum(g - 1, 0)
    cur = lambda b, g: b * ng + g
    return pl.pallas_call(
        _chunk_attn_kernel,
        grid=(batch, ng),
        in_specs=[pl.BlockSpec((hd, GROUP), lambda b, g: (0, cur(b, g))),
                  pl.BlockSpec((GROUP // 2, hd), lambda b, g: (prev(b, g), 0)),
                  pl.BlockSpec((GROUP // 2, hd), lambda b, g: (cur(b, g), 0)),
                  pl.BlockSpec((hd, GROUP), lambda b, g: (0, prev(b, g))),
                  pl.BlockSpec((hd, GROUP), lambda b, g: (0, cur(b, g))),
                  pl.BlockSpec(bias.shape, lambda b, g: (0, 0, 0),
                               pipeline_mode=pl.Buffered(1))],
        out_specs=pl.BlockSpec((hd, GROUP), lambda b, g: (0, cur(b, g))),
        out_shape=jax.ShapeDtypeStruct((hd, T), BF16),
        scratch_shapes=[pltpu.VMEM((B_HEADS // HEAD_GROUP, GROUP, HEAD_GROUP * B_HEAD_DIM),
                                   jnp.uint32),
                        pltpu.VMEM((n_quad, hd, QUAD_BAND), BF16),
                        pltpu.VMEM((B_HEADS, n_quad, HEAD_GROUP * B_HEAD_DIM, QUAD), BF16),
                        pltpu.VMEM((2 * PHASE_HEADS, QUAD_BAND, QUAD), F32),
                        pltpu.VMEM((2 * PHASE_HEADS, 1, QUAD), F32),
                        pltpu.VMEM((n_quad, hd, QUAD), F32)],
        compiler_params=pltpu.CompilerParams(
            dimension_semantics=("arbitrary", "arbitrary"), vmem_limit_bytes=VMEM_LIMIT),
        name="chunk_attn",
    )(qT, k, k, vT, vT, bias)


def kernel(x, positions, attn_pre_g, attn_post_g, ffn_pre_g, ffn_post_g, ffn_w_gate,
           ffn_w_up, ffn_w_down, mla_w_a, mla_g_q, mla_w_uq, mla_g_kv, mla_w_ukv, mla_w_o,
           kv_src_g, w_kv_shared, b_w_q, b_rel_table, b_w_o):
    batch, seq, d = x.shape
    T = batch * seq
    assert attn_pre_g.shape[0] == 2 and mla_w_a.shape[0] == 1 and b_w_q.shape[0] == 1
    assert seq % GROUP == 0 and seq % MLA_TQ == 0 and T % PROJ_ROWS == 0
    row = lambda g: g.reshape(1, -1).astype(F32)

    x2 = x.reshape(T, d)
    pos_row = positions.reshape(1, T).astype(F32)
    inv = 1.0 / (ROPE_THETA ** (jnp.arange(0, ROPE_DIM, 2, dtype=F32) / ROPE_DIM))
    inv_col = inv.reshape(ROPE_DIM // 2, 1)

    wa = mla_w_a[0][:, :Q_LORA + KV_LORA].astype(BF16)
    wkrT = jnp.pad(mla_w_a[0][:, Q_LORA + KV_LORA:].T, ((0, LANES - ROPE_DIM), (0, 0))).astype(BF16)
    wuq = mla_w_uq[0].reshape(Q_LORA, MLA_HEADS, NOPE_DIM + ROPE_DIM)
    wuq = jnp.pad(wuq, ((0, 0), (0, 0), (0, MLA_HEAD_PAD - NOPE_DIM - ROPE_DIM)))
    wuqT = wuq.reshape(Q_LORA, MLA_HEADS * MLA_HEAD_PAD).T.astype(BF16)
    wukv = mla_w_ukv[0].reshape(KV_LORA, MLA_HEADS, NOPE_DIM + V_DIM)
    wuk = wukv[:, :, :NOPE_DIM].reshape(KV_LORA, MLA_HEADS * NOPE_DIM).astype(BF16)
    wuvT = wukv[:, :, NOPE_DIM:].reshape(KV_LORA, MLA_HEADS * V_DIM).T.astype(BF16)
    hd = B_HEADS * B_HEAD_DIM
    wk = w_kv_shared[:, :hd].astype(BF16)
    wvT = w_kv_shared[:, hd:].T.astype(BF16)
    wqT = b_w_q[0].T.astype(BF16)

    qT0, kn0, kr0, vT0 = _mla_proj(x2, pos_row, inv_col, row(attn_pre_g[0]), wa, wkrT,
                                   row(mla_g_q[0]), wuqT, row(mla_g_kv[0]), wuk, wuvT)
    o0 = _mla_attn(qT0, kn0, kr0, vT0, batch, seq)
    wg, wu, wd = ffn_w_gate.astype(BF16), ffn_w_up.astype(BF16), ffn_w_down.astype(BF16)
    x2 = _block_tail(o0, x2, mla_w_o[0].astype(BF16), row(attn_post_g[0]), row(ffn_pre_g[0]),
                     wg, wu, wd, row(ffn_post_g[0]), 0)

    qT1, k1, vT1 = _kvq_proj(x2, row(kv_src_g), row(attn_pre_g[1]), wqT, wk, wvT)
    bias = _rel_bias(b_rel_table[0])
    o1 = _chunk_attn(qT1, k1, vT1, bias, batch, seq)
    x2 = _block_tail(o1, x2, b_w_o[0].astype(BF16), row(attn_post_g[1]), row(ffn_pre_g[1]),
                     wg, wu, wd, row(ffn_post_g[1]), 1)
    return x2.reshape(batch, seq, d)
```

```python
import functools
import math

import jax
import jax.numpy as jnp
from jax import lax
from jax.experimental import pallas as pl
from jax.experimental.pallas import tpu as pltpu

F32 = jnp.float32
BF16 = jnp.bfloat16

CHUNK = 64
MLA_HEADS = 8
Q_LORA = 384
KV_LORA = 256
NOPE_DIM = 128
ROPE_DIM = 64
V_DIM = 128
ROPE_THETA = 10000.0
B_HEADS = 16
B_HEAD_DIM = 64
LEFT_CHUNKS = 8
BAND = (LEFT_CHUNKS + 1) * CHUNK
MAX_REL = 256
REL_TABLE = MAX_REL + CHUNK
EPS = 1e-6

LANES = 128
SUBLANES = 8
MLA_HEAD_PAD = 2 * LANES
NEG = -1e30

PROJ_ROWS = 1024
TAIL_ROWS = 512
TAIL_SPLIT = 2
MLA_TQ = 512
MLA_TK = 256
SUM_ROWS = 16
QUAD = 4 * CHUNK
QUAD_BAND = BAND + 3 * CHUNK
GROUP = 2 * QUAD
HEAD_GROUP = 2
PHASE_HEADS = 8
BIAS_ROLL = 1024
VMEM_LIMIT = 56 * 1024 * 1024


def _rms(x, g):
    ms = jnp.mean(x * x, axis=-1, keepdims=True)
    return x * lax.rsqrt(ms + EPS) * g


def _dot(a, b):
    return jnp.dot(a, b, preferred_element_type=F32)


def _dot_nt(a, b):
    return lax.dot_general(a, b, (((1,), (1,)), ((), ())), preferred_element_type=F32)


def _mla_proj_kernel(x_ref, posr_ref, invc_ref, gpre_ref, wa_ref, wkrT_ref, gq_ref,
                     wuqT_ref, gkv_ref, wuk_ref, wuvT_ref, qT_ref, kn_ref, kr_ref, vT_ref, *,
                     scale):
    h = _rms(x_ref[...], gpre_ref[...]).astype(BF16)
    a = _dot(h, wa_ref[...])
    cqn = _rms(a[:, :Q_LORA], gq_ref[...]).astype(BF16)
    ckvn = _rms(a[:, Q_LORA:], gkv_ref[...]).astype(BF16)
    half = ROPE_DIM // 2

    angT = invc_ref[...] * posr_ref[...]
    cosT, sinT = jnp.cos(angT), jnp.sin(angT)

    krT = _dot_nt(wkrT_ref[...], h)
    k1, k2 = krT[:half], krT[half:ROPE_DIM]
    kr_rotT = jnp.concatenate(
        [k1 * cosT - k2 * sinT, k2 * cosT + k1 * sinT,
         jnp.zeros((LANES - ROPE_DIM, krT.shape[1]), F32)], axis=0)
    kr_ref[...] = pltpu.bitcast(kr_rotT.T.astype(BF16), jnp.uint32)
    kn_ref[...] = pltpu.bitcast(_dot(ckvn, wuk_ref[...]).astype(BF16), jnp.uint32)

    cosT, sinT = cosT * scale, sinT * scale
    qT = _dot_nt(wuqT_ref[...], cqn)
    for hd in range(MLA_HEADS):
        lo = hd * MLA_HEAD_PAD
        t1 = qT[lo + NOPE_DIM:lo + NOPE_DIM + half]
        t2 = qT[lo + NOPE_DIM + half:lo + NOPE_DIM + ROPE_DIM]
        qT_ref[lo:lo + NOPE_DIM] = (qT[lo:lo + NOPE_DIM] * scale).astype(BF16)
        qT_ref[lo + NOPE_DIM:lo + NOPE_DIM + half] = (t1 * cosT - t2 * sinT).astype(BF16)
        qT_ref[lo + NOPE_DIM + half:lo + NOPE_DIM + ROPE_DIM] = (t2 * cosT + t1 * sinT).astype(BF16)
        qT_ref[lo + NOPE_DIM + ROPE_DIM:lo + MLA_HEAD_PAD] = jnp.zeros(
            (MLA_HEAD_PAD - NOPE_DIM - ROPE_DIM, qT.shape[1]), BF16)

    vT = _dot_nt(wuvT_ref[...], ckvn).astype(BF16)
    for t in range(vT_ref.shape[0]):
        vT_ref[t] = vT[:, t * MLA_TK:(t + 1) * MLA_TK]


def _mla_proj(x2, pos_row, inv_col, gpre, wa, wkrT, gq, wuqT, gkv, wuk, wuvT):
    T, D = x2.shape
    tm = PROJ_ROWS
    const = lambda i: (0, 0)
    full = lambda a: pl.BlockSpec(a.shape, const)
    scale = float((NOPE_DIM + ROPE_DIM) ** -0.5 * math.log2(math.e))
    return pl.pallas_call(
        functools.partial(_mla_proj_kernel, scale=scale),
        grid=(T // tm,),
        in_specs=[pl.BlockSpec((tm, D), lambda i: (i, 0)),
                  pl.BlockSpec((1, tm), lambda i: (0, i)),
                  full(inv_col), full(gpre), full(wa), full(wkrT), full(gq), full(wuqT),
                  full(gkv), full(wuk), full(wuvT)],
        out_specs=[pl.BlockSpec((MLA_HEADS * MLA_HEAD_PAD, tm), lambda i: (0, i)),
                   pl.BlockSpec((tm // 2, MLA_HEADS * NOPE_DIM), lambda i: (i, 0)),
                   pl.BlockSpec((tm // 2, LANES), lambda i: (i, 0)),
                   pl.BlockSpec((tm // MLA_TK, MLA_HEADS * V_DIM, MLA_TK), lambda i: (i, 0, 0))],
        out_shape=[jax.ShapeDtypeStruct((MLA_HEADS * MLA_HEAD_PAD, T), BF16),
                   jax.ShapeDtypeStruct((T // 2, MLA_HEADS * NOPE_DIM), jnp.uint32),
                   jax.ShapeDtypeStruct((T // 2, LANES), jnp.uint32),
                   jax.ShapeDtypeStruct((T // MLA_TK, MLA_HEADS * V_DIM, MLA_TK), BF16)],
        compiler_params=pltpu.CompilerParams(
            dimension_semantics=("arbitrary",), vmem_limit_bytes=VMEM_LIMIT),
        name="mla_proj",
    )(x2, pos_row, inv_col, gpre, wa, wkrT, gq, wuqT, gkv, wuk, wuvT)


def _mla_attn_kernel(qT_ref, kn_ref, kr_ref, vT_ref, oT_ref, m_s, l_s, a_s, acc_s, s_s):
    qi = pl.program_id(1)
    n_full = qi * (MLA_TQ // MLA_TK)
    m_s[...] = jnp.full(m_s.shape, NEG, F32)
    l_s[...] = jnp.zeros(l_s.shape, F32)
    a_s[...] = jnp.ones(a_s.shape, F32)
    acc_s[...] = jnp.zeros(acc_s.shape, F32)
    s_s[MLA_HEADS:] = jnp.full((MLA_HEADS, MLA_TK, MLA_TQ), -jnp.inf, F32)
    ones = jnp.ones((SUM_ROWS, MLA_TK), BF16)

    def score_pass(j, parity, diag):
        slot = parity * MLA_HEADS
        half = MLA_TK // 2
        rows = pl.ds(pl.multiple_of(j * half, half), half)
        q0 = 0 if diag is None else diag * MLA_TK
        k_rope = pltpu.bitcast(kr_ref[rows, :], BF16)
        for hd in range(MLA_HEADS):
            k_nope = pltpu.bitcast(kn_ref[rows, hd * NOPE_DIM:(hd + 1) * NOPE_DIM], BF16)
            k = jnp.concatenate([k_nope, k_rope], axis=1)
            qT = qT_ref[hd * MLA_HEAD_PAD:(hd + 1) * MLA_HEAD_PAD, q0:]
            sT = _dot(k, qT)
            if diag is not None:
                kc = lax.broadcasted_iota(jnp.int32, sT.shape, 0) // CHUNK
                qc = lax.broadcasted_iota(jnp.int32, sT.shape, 1) // CHUNK
                sT = jnp.where(kc <= qc, sT, NEG)
            s_s[slot + hd, :, q0:] = sT
            m = m_s[hd, :, q0:]
            m_new = jnp.maximum(m, jnp.max(sT, axis=0, keepdims=True))
            a_s[hd, :, q0:] = jnp.exp2(m - m_new)
            m_s[hd, :, q0:] = m_new

    def value_pass(j, parity, diag=None):
        slot = parity * MLA_HEADS
        jv = jnp.maximum(j, 0)
        q0 = 0 if diag is None else diag * MLA_TK
        for hd in range(MLA_HEADS):
            p = jnp.exp2(s_s[slot + hd, :, q0:] - m_s[hd, :, q0:]).astype(BF16)
            lhs = jnp.concatenate([vT_ref[jv, hd * V_DIM:(hd + 1) * V_DIM, :], ones], axis=0)
            pv = _dot(lhs, p)
            a = a_s[hd, :, q0:]
            acc_s[hd, :, q0:] = a * acc_s[hd, :, q0:] + pv[:V_DIM]
            l_s[hd, :, q0:] = a * l_s[hd, :, q0:] + pv[V_DIM:V_DIM + 1]

    n_diag = MLA_TQ // MLA_TK

    def steady(i, carry):
        for u in range(n_diag):
            value_pass(i * n_diag + u - 1, (u - 1) % 2)
            score_pass(i * n_diag + u, u % 2, None)
        return carry

    lax.fori_loop(0, qi, steady, 0)
    for d in range(n_diag):
        value_pass(n_full + d - 1, (d - 1) % 2, d - 1 if d > 0 else None)
        score_pass(n_full + d, d % 2, d)
    value_pass(n_full + n_diag - 1, (n_diag - 1) % 2, n_diag - 1)
    for hd in range(MLA_HEADS):
        oT_ref[hd * V_DIM:(hd + 1) * V_DIM, :] = (acc_s[hd] * (1.0 / l_s[hd])).astype(BF16)


def _mla_attn(qT, kn, kr, vT3, batch, seq):
    assert MLA_TQ % (2 * MLA_TK) == 0 and MLA_TK % CHUNK == 0
    T = qT.shape[1]
    nq = seq // MLA_TQ
    nkt = seq // MLA_TK
    return pl.pallas_call(
        _mla_attn_kernel,
        grid=(batch, nq),
        in_specs=[pl.BlockSpec((MLA_HEADS * MLA_HEAD_PAD, MLA_TQ), lambda b, i: (0, b * nq + i)),
                  pl.BlockSpec((seq // 2, MLA_HEADS * NOPE_DIM), lambda b, i: (b, 0)),
                  pl.BlockSpec((seq // 2, LANES), lambda b, i: (b, 0)),
                  pl.BlockSpec((nkt, MLA_HEADS * V_DIM, MLA_TK), lambda b, i: (b, 0, 0),
                               pipeline_mode=pl.Buffered(1))],
        out_specs=pl.BlockSpec((MLA_HEADS * V_DIM, MLA_TQ), lambda b, i: (0, b * nq + i)),
        out_shape=jax.ShapeDtypeStruct((MLA_HEADS * V_DIM, T), BF16),
        scratch_shapes=[pltpu.VMEM((MLA_HEADS, 1, MLA_TQ), F32),
                        pltpu.VMEM((MLA_HEADS, 1, MLA_TQ), F32),
                        pltpu.VMEM((MLA_HEADS, 1, MLA_TQ), F32),
                        pltpu.VMEM((MLA_HEADS, V_DIM, MLA_TQ), F32),
                        pltpu.VMEM((2 * MLA_HEADS, MLA_TK, MLA_TQ), F32)],
        compiler_params=pltpu.CompilerParams(
            dimension_semantics=("arbitrary", "arbitrary"),
            vmem_limit_bytes=VMEM_LIMIT),
        name="mla_attn",
    )(qT, kn, kr, vT3)


def _block_tail_kernel(oT_ref, x_ref, wo_ref, gpost_ref, gfpre_ref, wg_ref, wu_ref,
                       wd_ref, gfpost_ref, out_ref):
    sub = out_ref.shape[0] // TAIL_SPLIT
    blocks = [slice(i * sub, (i + 1) * sub) for i in range(TAIL_SPLIT)]
    y = [lax.dot_general(oT_ref[:, r], wo_ref[...], (((0,), (0,)), ((), ())),
                         preferred_element_type=F32) for r in blocks]
    x1 = [x_ref[r, :] + _rms(yi, gpost_ref[...]) for r, yi in zip(blocks, y)]
    h = [_rms(xi, gfpre_ref[...]).astype(BF16) for xi in x1]
    g = [_dot(hi, wg_ref[...]) for hi in h]
    u = [_dot(hi, wu_ref[...]) for hi in h]
    a = [(gi * (1.0 / (1.0 + jnp.exp(-gi))) * ui).astype(BF16) for gi, ui in zip(g, u)]
    f = [_dot(ai, wd_ref[...]) for ai in a]
    for r, xi, fi in zip(blocks, x1, f):
        out_ref[r, :] = xi + _rms(fi, gfpost_ref[...])


def _block_tail(oT, x2, wo, gpost, gfpre, wg, wu, wd, gfpost, layer):
    T, D = x2.shape
    tm = TAIL_ROWS
    const = lambda i: (0, 0)
    full = lambda a: pl.BlockSpec(a.shape, const, pipeline_mode=pl.Buffered(1))
    of_layer = lambda a: pl.BlockSpec((None,) + a.shape[1:], lambda i: (layer, 0, 0),
                                      pipeline_mode=pl.Buffered(1))
    row = lambda a: pl.BlockSpec((tm, a.shape[1]), lambda i: (i, 0))
    return pl.pallas_call(
        _block_tail_kernel,
        grid=(T // tm,),
        in_specs=[pl.BlockSpec((oT.shape[0], tm), lambda i: (0, i)), row(x2), full(wo),
                  full(gpost), full(gfpre), of_layer(wg), of_layer(wu), of_layer(wd),
                  full(gfpost)],
        out_specs=pl.BlockSpec((tm, D), lambda i: (i, 0)),
        out_shape=jax.ShapeDtypeStruct((T, D), F32),
        compiler_params=pltpu.CompilerParams(
            dimension_semantics=("arbitrary",), vmem_limit_bytes=VMEM_LIMIT),
        name="block_tail",
    )(oT, x2, wo, gpost, gfpre, wg, wu, wd, gfpost)


def _kvq_proj_kernel(x_ref, gsrc_ref, gpre_ref, wqT_ref, wk_ref, wvT_ref,
                     qT_ref, k_ref, vT_ref, *, scale):
    x = x_ref[...]
    xr = x * lax.rsqrt(jnp.mean(x * x, axis=-1, keepdims=True) + EPS)
    xs = (xr * gsrc_ref[...]).astype(BF16)
    xq = (xr * gpre_ref[...]).astype(BF16)
    qT_ref[...] = (_dot_nt(wqT_ref[...], xq) * scale).astype(BF16)
    k_ref[...] = pltpu.bitcast(_dot(xs, wk_ref[...]).astype(BF16), jnp.uint32)
    vT_ref[...] = _dot_nt(wvT_ref[...], xs).astype(BF16)


def _kvq_proj(x2, gsrc, gpre, wqT, wk, wvT):
    T, D = x2.shape
    tm = PROJ_ROWS
    hd = B_HEADS * B_HEAD_DIM
    const = lambda i: (0, 0)
    full = lambda a: pl.BlockSpec(a.shape, const)
    scale = float(B_HEAD_DIM ** -0.5 * math.log2(math.e))
    return pl.pallas_call(
        functools.partial(_kvq_proj_kernel, scale=scale),
        grid=(T // tm,),
        in_specs=[pl.BlockSpec((tm, D), lambda i: (i, 0)),
                  full(gsrc), full(gpre), full(wqT), full(wk), full(wvT)],
        out_specs=[pl.BlockSpec((hd, tm), lambda i: (0, i)),
                   pl.BlockSpec((tm // 2, hd), lambda i: (i, 0)),
                   pl.BlockSpec((hd, tm), lambda i: (0, i))],
        out_shape=[jax.ShapeDtypeStruct((hd, T), BF16),
                   jax.ShapeDtypeStruct((T // 2, hd), jnp.uint32),
                   jax.ShapeDtypeStruct((hd, T), BF16)],
        compiler_params=pltpu.CompilerParams(
            dimension_semantics=("arbitrary",), vmem_limit_bytes=VMEM_LIMIT),
        name="kvq_proj",
    )(x2, gsrc, gpre, wqT, wk, wvT)


def _rel_bias_kernel(tab_ref, out_ref, g_s):
    tab = tab_ref[...]
    t_hi = tab.astype(BF16)
    rem = tab - t_hi.astype(F32)
    t_mid = rem.astype(BF16)
    t_lo = (rem - t_mid.astype(F32)).astype(BF16)
    row = lax.broadcasted_iota(jnp.int32, (REL_TABLE, BIAS_ROLL), 0)
    lane = lax.broadcasted_iota(jnp.int32, (REL_TABLE, BIAS_ROLL), 1)
    dist = jnp.where(lane < QUAD, lane, lane - BIAS_ROLL) + LEFT_CHUNKS * CHUNK
    idx = jnp.clip(dist, -(CHUNK - 1), MAX_REL) + (CHUNK - 1)
    onehot = jnp.where(idx == row, 1.0, 0.0).astype(BF16)
    g_s[...] = (_dot(t_hi, onehot) + _dot(t_mid, onehot) + _dot(t_lo, onehot)) * math.log2(math.e)
    q_chunk = lax.broadcasted_iota(jnp.int32, (1, QUAD), 1) // CHUNK + LEFT_CHUNKS

    def body(cb, carry):
        c0 = pl.multiple_of(cb * SUBLANES, SUBLANES)
        back = q_chunk - cb // (CHUNK // SUBLANES)
        valid = (back >= 0) & (back <= LEFT_CHUNKS)
        for h in range(B_HEADS):
            rows = jnp.broadcast_to(g_s[h:h + 1, :], (SUBLANES, BIAS_ROLL))
            rolled = pltpu.roll(rows, c0, 1, stride=1, stride_axis=0)
            out_ref[h, pl.ds(c0, SUBLANES), :] = jnp.where(valid, rolled[:, :QUAD], NEG)
        return carry

    lax.fori_loop(0, QUAD_BAND // SUBLANES, body, 0)


def _rel_bias(table):
    return pl.pallas_call(
        _rel_bias_kernel,
        out_shape=jax.ShapeDtypeStruct((B_HEADS, QUAD_BAND, QUAD), F32),
        scratch_shapes=[pltpu.VMEM((B_HEADS, BIAS_ROLL), F32)],
        name="rel_bias",
    )(table)


def _chunk_attn_kernel(qT_ref, kp_ref, kc_ref, vTp_ref, vTc_ref, bias_ref, o_ref,
                       k4_s, vTq_s, qTm_s, s_s, m_s, oT_s):
    g = pl.program_id(1)
    slab = HEAD_GROUP * B_HEAD_DIM
    n_hg = B_HEADS // HEAD_GROUP
    n_phase = (GROUP // QUAD) * (B_HEADS // PHASE_HEADS)
    half = GROUP // 2
    for hg in range(n_hg):
        lanes = slice(hg * slab, (hg + 1) * slab)
        k4_s[hg, :half, :] = kp_ref[:, lanes]
        k4_s[hg, half:, :] = kc_ref[:, lanes]
    vTq_s[0, :, :GROUP] = vTp_ref[...]
    vTq_s[0, :, GROUP:] = vTc_ref[:, :QUAD]
    vTq_s[1, :, :QUAD] = vTp_ref[:, QUAD:]
    vTq_s[1, :, QUAD:] = vTc_ref[...]
    @pl.when(g == 0)
    def _():
        qTm_s[...] = jnp.zeros(qTm_s.shape, BF16)

    for h in range(B_HEADS):
        own = slice((h % HEAD_GROUP) * B_HEAD_DIM, (h % HEAD_GROUP + 1) * B_HEAD_DIM)
        for qd in range(GROUP // QUAD):
            qTm_s[h, qd, own, :] = qT_ref[h * B_HEAD_DIM:(h + 1) * B_HEAD_DIM,
                                          qd * QUAD:(qd + 1) * QUAD]
    ones = jnp.ones((SUM_ROWS, QUAD_BAND), BF16)

    def phase_ids(t):
        per_quad = B_HEADS // PHASE_HEADS
        slot, qd, part = (t & 1) * PHASE_HEADS, t >> (per_quad.bit_length() - 1), t & (per_quad - 1)
        return slot, qd, part * PHASE_HEADS, part * (PHASE_HEADS // HEAD_GROUP)

    def score_phase(t, first_group):
        slot, qd, head0, hg0 = phase_ids(t)
        start = qd * (QUAD // 2)
        if not isinstance(start, int):
            start = pl.multiple_of(start, QUAD // 2)
        for i in range(PHASE_HEADS):
            h = head0 + i
            kb = pltpu.bitcast(
                k4_s[hg0 + i // HEAD_GROUP, pl.ds(start, QUAD_BAND // 2), :], BF16)
            s = _dot(kb, qTm_s[h, qd]) + bias_ref[h]
            if first_group:
                key = lax.broadcasted_iota(jnp.int32, (QUAD_BAND, 1), 0)
                s = jnp.where(key >= GROUP - qd * QUAD, s, NEG)
            s_s[slot + i] = s
            m_s[slot + i] = jnp.max(s, axis=0, keepdims=True)

    def value_phase(t):
        slot, qd, head0, _ = phase_ids(t)
        for i in range(PHASE_HEADS):
            row0 = (head0 + i) * B_HEAD_DIM
            if not isinstance(row0, int):
                row0 = pl.multiple_of(row0, B_HEAD_DIM)
            p = jnp.exp2(s_s[slot + i] - m_s[slot + i]).astype(BF16)
            lhs = jnp.concatenate([vTq_s[qd, pl.ds(row0, B_HEAD_DIM), :], ones], axis=0)
            r = _dot(lhs, p)
            oT_s[qd, pl.ds(row0, B_HEAD_DIM), :] = (
                r[:B_HEAD_DIM] * (1.0 / r[B_HEAD_DIM:B_HEAD_DIM + 1]))

    def run(first_group):
        score_phase(0, first_group)

        def steady(t, carry):
            value_phase(t - 1)
            score_phase(t, first_group)
            return carry

        lax.fori_loop(1, n_phase, steady, 0)
        value_phase(n_phase - 1)

    pl.when(g == 0)(lambda: run(True))
    pl.when(g != 0)(lambda: run(False))
    for qd in range(GROUP // QUAD):
        o_ref[:, qd * QUAD:(qd + 1) * QUAD] = oT_s[qd].astype(BF16)


def _chunk_attn(qT, k, vT, bias, batch, seq):
    hd, T = qT.shape
    ng = seq // GROUP
    n_quad = GROUP // QUAD
    prev = lambda b, g: b * ng + jnp.maximum(g - 1, 0)
    cur = lambda b, g: b * ng + g
    return pl.pallas_call(
        _chunk_attn_kernel,
        grid=(batch, ng),
        in_specs=[pl.BlockSpec((hd, GROUP), lambda b, g: (0, cur(b, g))),
                  pl.BlockSpec((GROUP // 2, hd), lambda b, g: (prev(b, g), 0)),
                  pl.BlockSpec((GROUP // 2, hd), lambda b, g: (cur(b, g), 0)),
                  pl.BlockSpec((hd, GROUP), lambda b, g: (0, prev(b, g))),
                  pl.BlockSpec((hd, GROUP), lambda b, g: (0, cur(b, g))),
                  pl.BlockSpec(bias.shape, lambda b, g: (0, 0, 0),
                               pipeline_mode=pl.Buffered(1))],
        out_specs=pl.BlockSpec((hd, GROUP), lambda b, g: (0, cur(b, g))),
        out_shape=jax.ShapeDtypeStruct((hd, T), BF16),
        scratch_shapes=[pltpu.VMEM((B_HEADS // HEAD_GROUP, GROUP, HEAD_GROUP * B_HEAD_DIM),
                                   jnp.uint32),
                        pltpu.VMEM((n_quad, hd, QUAD_BAND), BF16),
                        pltpu.VMEM((B_HEADS, n_quad, HEAD_GROUP * B_HEAD_DIM, QUAD), BF16),
                        pltpu.VMEM((2 * PHASE_HEADS, QUAD_BAND, QUAD), F32),
                        pltpu.VMEM((2 * PHASE_HEADS, 1, QUAD), F32),
                        pltpu.VMEM((n_quad, hd, QUAD), F32)],
        compiler_params=pltpu.CompilerParams(
            dimension_semantics=("arbitrary", "arbitrary"), vmem_limit_bytes=VMEM_LIMIT),
        name="chunk_attn",
    )(qT, k, k, vT, vT, bias)


def kernel(x, positions, attn_pre_g, attn_post_g, ffn_pre_g, ffn_post_g, ffn_w_gate,
           ffn_w_up, ffn_w_down, mla_w_a, mla_g_q, mla_w_uq, mla_g_kv, mla_w_ukv, mla_w_o,
           kv_src_g, w_kv_shared, b_w_q, b_rel_table, b_w_o):
    batch, seq, d = x.shape
    T = batch * seq
    assert attn_pre_g.shape[0] == 2 and mla_w_a.shape[0] == 1 and b_w_q.shape[0] == 1
    assert seq % GROUP == 0 and seq % MLA_TQ == 0 and T % PROJ_ROWS == 0
    row = lambda g: g.reshape(1, -1).astype(F32)

    x2 = x.reshape(T, d)
    pos_row = positions.reshape(1, T).astype(F32)
    inv = 1.0 / (ROPE_THETA ** (jnp.arange(0, ROPE_DIM, 2, dtype=F32) / ROPE_DIM))
    inv_col = inv.reshape(ROPE_DIM // 2, 1)

    wa = mla_w_a[0][:, :Q_LORA + KV_LORA].astype(BF16)
    wkrT = jnp.pad(mla_w_a[0][:, Q_LORA + KV_LORA:].T, ((0, LANES - ROPE_DIM), (0, 0))).astype(BF16)
    wuq = mla_w_uq[0].reshape(Q_LORA, MLA_HEADS, NOPE_DIM + ROPE_DIM)
    wuq = jnp.pad(wuq, ((0, 0), (0, 0), (0, MLA_HEAD_PAD - NOPE_DIM - ROPE_DIM)))
    wuqT = wuq.reshape(Q_LORA, MLA_HEADS * MLA_HEAD_PAD).T.astype(BF16)
    wukv = mla_w_ukv[0].reshape(KV_LORA, MLA_HEADS, NOPE_DIM + V_DIM)
    wuk = wukv[:, :, :NOPE_DIM].reshape(KV_LORA, MLA_HEADS * NOPE_DIM).astype(BF16)
    wuvT = wukv[:, :, NOPE_DIM:].reshape(KV_LORA, MLA_HEADS * V_DIM).T.astype(BF16)
    hd = B_HEADS * B_HEAD_DIM
    wk = w_kv_shared[:, :hd].astype(BF16)
    wvT = w_kv_shared[:, hd:].T.astype(BF16)
    wqT = b_w_q[0].T.astype(BF16)

    qT0, kn0, kr0, vT0 = _mla_proj(x2, pos_row, inv_col, row(attn_pre_g[0]), wa, wkrT,
                                   row(mla_g_q[0]), wuqT, row(mla_g_kv[0]), wuk, wuvT)
    o0 = _mla_attn(qT0, kn0, kr0, vT0, batch, seq)
    wg, wu, wd = ffn_w_gate.astype(BF16), ffn_w_up.astype(BF16), ffn_w_down.astype(BF16)
    x2 = _block_tail(o0, x2, mla_w_o[0].astype(BF16), row(attn_post_g[0]), row(ffn_pre_g[0]),
                     wg, wu, wd, row(ffn_post_g[0]), 0)

    qT1, k1, vT1 = _kvq_proj(x2, row(kv_src_g), row(attn_pre_g[1]), wqT, wk, wvT)
    bias = _rel_bias(b_rel_table[0])
    o1 = _chunk_attn(qT1, k1, vT1, bias, batch, seq)
    x2 = _block_tail(o1, x2, b_w_o[0].astype(BF16), row(attn_post_g[1]), row(ffn_pre_g[1]),
                     wg, wu, wd, row(ffn_post_g[1]), 1)
    return x2.reshape(batch, seq, d)
```

```python
import functools
import math

import jax
import jax.numpy as jnp
from jax import lax
from jax.experimental import pallas as pl
from jax.experimental.pallas import tpu as pltpu

F32 = jnp.float32
BF16 = jnp.bfloat16

CHUNK = 64
MLA_HEADS = 8
Q_LORA = 384
KV_LORA = 256
NOPE_DIM = 128
ROPE_DIM = 64
V_DIM = 128
ROPE_THETA = 10000.0
B_HEADS = 16
B_HEAD_DIM = 64
LEFT_CHUNKS = 8
BAND = (LEFT_CHUNKS + 1) * CHUNK
MAX_REL = 256
REL_TABLE = MAX_REL + CHUNK
EPS = 1e-6

LANES = 128
SUBLANES = 8
MLA_HEAD_PAD = 2 * LANES
NEG = -1e30

PROJ_ROWS = 1024
TAIL_ROWS = 512
TAIL_SPLIT = 2
MLA_TQ = 512
MLA_TK = 256
MLA_TRIP_TILES = 4
SUM_ROWS = 16
QUAD = 4 * CHUNK
QUAD_BAND = BAND + 3 * CHUNK
GROUP = 2 * QUAD
HEAD_GROUP = 2
PHASE_HEADS = 8
BIAS_ROLL = 1024
VMEM_LIMIT = 56 * 1024 * 1024


def _rms(x, g):
    ms = jnp.mean(x * x, axis=-1, keepdims=True)
    return x * lax.rsqrt(ms + EPS) * g


def _dot(a, b):
    return jnp.dot(a, b, preferred_element_type=F32)


def _dot_nt(a, b):
    return lax.dot_general(a, b, (((1,), (1,)), ((), ())), preferred_element_type=F32)


def _mla_proj_kernel(x_ref, posr_ref, invc_ref, gpre_ref, wa_ref, wkrT_ref, gq_ref,
                     wuqT_ref, gkv_ref, wuk_ref, wuvT_ref, qT_ref, kn_ref, kr_ref, vT_ref, *,
                     scale):
    h = _rms(x_ref[...], gpre_ref[...]).astype(BF16)
    a = _dot(h, wa_ref[...])
    cqn = _rms(a[:, :Q_LORA], gq_ref[...]).astype(BF16)
    ckvn = _rms(a[:, Q_LORA:], gkv_ref[...]).astype(BF16)
    half = ROPE_DIM // 2

    angT = invc_ref[...] * posr_ref[...]
    cosT, sinT = jnp.cos(angT), jnp.sin(angT)

    krT = _dot_nt(wkrT_ref[...], h)
    k1, k2 = krT[:half], krT[half:ROPE_DIM]
    kr_rotT = jnp.concatenate(
        [k1 * cosT - k2 * sinT, k2 * cosT + k1 * sinT,
         jnp.zeros((LANES - ROPE_DIM, krT.shape[1]), F32)], axis=0)
    kr_ref[...] = pltpu.bitcast(kr_rotT.T.astype(BF16), jnp.uint32)
    kn_ref[...] = pltpu.bitcast(_dot(ckvn, wuk_ref[...]).astype(BF16), jnp.uint32)

    cosT, sinT = cosT * scale, sinT * scale
    qT = _dot_nt(wuqT_ref[...], cqn)
    for hd in range(MLA_HEADS):
        lo = hd * MLA_HEAD_PAD
        t1 = qT[lo + NOPE_DIM:lo + NOPE_DIM + half]
        t2 = qT[lo + NOPE_DIM + half:lo + NOPE_DIM + ROPE_DIM]
        qT_ref[lo:lo + NOPE_DIM] = (qT[lo:lo + NOPE_DIM] * scale).astype(BF16)
        qT_ref[lo + NOPE_DIM:lo + NOPE_DIM + half] = (t1 * cosT - t2 * sinT).astype(BF16)
        qT_ref[lo + NOPE_DIM + half:lo + NOPE_DIM + ROPE_DIM] = (t2 * cosT + t1 * sinT).astype(BF16)
        qT_ref[lo + NOPE_DIM + ROPE_DIM:lo + MLA_HEAD_PAD] = jnp.zeros(
            (MLA_HEAD_PAD - NOPE_DIM - ROPE_DIM, qT.shape[1]), BF16)

    vT = _dot_nt(wuvT_ref[...], ckvn).astype(BF16)
    for t in range(vT_ref.shape[0]):
        vT_ref[t] = vT[:, t * MLA_TK:(t + 1) * MLA_TK]


def _mla_proj(x2, pos_row, inv_col, gpre, wa, wkrT, gq, wuqT, gkv, wuk, wuvT):
    T, D = x2.shape
    tm = PROJ_ROWS
    const = lambda i: (0, 0)
    full = lambda a: pl.BlockSpec(a.shape, const)
    scale = float((NOPE_DIM + ROPE_DIM) ** -0.5 * math.log2(math.e))
    return pl.pallas_call(
        functools.partial(_mla_proj_kernel, scale=scale),
        grid=(T // tm,),
        in_specs=[pl.BlockSpec((tm, D), lambda i: (i, 0)),
                  pl.BlockSpec((1, tm), lambda i: (0, i)),
                  full(inv_col), full(gpre), full(wa), full(wkrT), full(gq), full(wuqT),
                  full(gkv), full(wuk), full(wuvT)],
        out_specs=[pl.BlockSpec((MLA_HEADS * MLA_HEAD_PAD, tm), lambda i: (0, i)),
                   pl.BlockSpec((tm // 2, MLA_HEADS * NOPE_DIM), lambda i: (i, 0)),
                   pl.BlockSpec((tm // 2, LANES), lambda i: (i, 0)),
                   pl.BlockSpec((tm // MLA_TK, MLA_HEADS * V_DIM, MLA_TK), lambda i: (i, 0, 0))],
        out_shape=[jax.ShapeDtypeStruct((MLA_HEADS * MLA_HEAD_PAD, T), BF16),
                   jax.ShapeDtypeStruct((T // 2, MLA_HEADS * NOPE_DIM), jnp.uint32),
                   jax.ShapeDtypeStruct((T // 2, LANES), jnp.uint32),
                   jax.ShapeDtypeStruct((T // MLA_TK, MLA_HEADS * V_DIM, MLA_TK), BF16)],
        compiler_params=pltpu.CompilerParams(
            dimension_semantics=("arbitrary",), vmem_limit_bytes=VMEM_LIMIT),
        name="mla_proj",
    )(x2, pos_row, inv_col, gpre, wa, wkrT, gq, wuqT, gkv, wuk, wuvT)


def _mla_attn_kernel(qT_ref, kn_ref, kr_ref, vT_ref, oT_ref, m_s, l_s, a_s, acc_s, s_s):
    qi = pl.program_id(1)
    n_full = qi * (MLA_TQ // MLA_TK)
    m_s[...] = jnp.full(m_s.shape, NEG, F32)
    l_s[...] = jnp.zeros(l_s.shape, F32)
    a_s[...] = jnp.ones(a_s.shape, F32)
    acc_s[...] = jnp.zeros(acc_s.shape, F32)
    s_s[MLA_HEADS:] = jnp.full((MLA_HEADS, MLA_TK, MLA_TQ), -jnp.inf, F32)
    ones = jnp.ones((SUM_ROWS, MLA_TK), BF16)

    def score_pass(j, parity, diag):
        slot = parity * MLA_HEADS
        half = MLA_TK // 2
        rows = pl.ds(pl.multiple_of(j * half, half), half)
        q0 = 0 if diag is None else diag * MLA_TK
        k_rope = pltpu.bitcast(kr_ref[rows, :], BF16)
        for hd in range(MLA_HEADS):
            k_nope = pltpu.bitcast(kn_ref[rows, hd * NOPE_DIM:(hd + 1) * NOPE_DIM], BF16)
            k = jnp.concatenate([k_nope, k_rope], axis=1)
            qT = qT_ref[hd * MLA_HEAD_PAD:(hd + 1) * MLA_HEAD_PAD, q0:]
            sT = _dot(k, qT)
            if diag is not None:
                kc = lax.broadcasted_iota(jnp.int32, sT.shape, 0) // CHUNK
                qc = lax.broadcasted_iota(jnp.int32, sT.shape, 1) // CHUNK
                sT = jnp.where(kc <= qc, sT, NEG)
            s_s[slot + hd, :, q0:] = sT
            m = m_s[hd, :, q0:]
            m_new = jnp.maximum(m, jnp.max(sT, axis=0, keepdims=True))
            a_s[hd, :, q0:] = jnp.exp2(m - m_new)
            m_s[hd, :, q0:] = m_new

    def value_pass(j, parity, diag=None):
        slot = parity * MLA_HEADS
        jv = jnp.maximum(j, 0)
        q0 = 0 if diag is None else diag * MLA_TK
        for hd in range(MLA_HEADS):
            p = jnp.exp2(s_s[slot + hd, :, q0:] - m_s[hd, :, q0:]).astype(BF16)
            lhs = jnp.concatenate([vT_ref[jv, hd * V_DIM:(hd + 1) * V_DIM, :], ones], axis=0)
            pv = _dot(lhs, p)
            a = a_s[hd, :, q0:]
            acc_s[hd, :, q0:] = a * acc_s[hd, :, q0:] + pv[:V_DIM]
            l_s[hd, :, q0:] = a * l_s[hd, :, q0:] + pv[V_DIM:V_DIM + 1]

    n_diag = MLA_TQ // MLA_TK

    def full_tiles(base, count):
        for u in range(count):
            value_pass(base + u - 1, (u - 1) % 2)
            score_pass(base + u, u % 2, None)

    def steady(i, carry):
        full_tiles(i * MLA_TRIP_TILES, MLA_TRIP_TILES)
        return carry

    n_trips = n_full // MLA_TRIP_TILES
    lax.fori_loop(0, n_trips, steady, 0)
    for left in range(n_diag, MLA_TRIP_TILES, n_diag):
        @pl.when(n_full - n_trips * MLA_TRIP_TILES == left)
        def _(left=left):
            full_tiles(n_full - left, left)
    for d in range(n_diag):
        value_pass(n_full + d - 1, (d - 1) % 2, d - 1 if d > 0 else None)
        score_pass(n_full + d, d % 2, d)
    value_pass(n_full + n_diag - 1, (n_diag - 1) % 2, n_diag - 1)
    for hd in range(MLA_HEADS):
        oT_ref[hd * V_DIM:(hd + 1) * V_DIM, :] = (acc_s[hd] * (1.0 / l_s[hd])).astype(BF16)


def _mla_attn(qT, kn, kr, vT3, batch, seq):
    assert MLA_TQ % (2 * MLA_TK) == 0 and MLA_TK % CHUNK == 0
    T = qT.shape[1]
    nq = seq // MLA_TQ
    nkt = seq // MLA_TK
    return pl.pallas_call(
        _mla_attn_kernel,
        grid=(batch, nq),
        in_specs=[pl.BlockSpec((MLA_HEADS * MLA_HEAD_PAD, MLA_TQ), lambda b, i: (0, b * nq + i)),
                  pl.BlockSpec((seq // 2, MLA_HEADS * NOPE_DIM), lambda b, i: (b, 0)),
                  pl.BlockSpec((seq // 2, LANES), lambda b, i: (b, 0)),
                  pl.BlockSpec((nkt, MLA_HEADS * V_DIM, MLA_TK), lambda b, i: (b, 0, 0),
                               pipeline_mode=pl.Buffered(1))],
        out_specs=pl.BlockSpec((MLA_HEADS * V_DIM, MLA_TQ), lambda b, i: (0, b * nq + i)),
        out_shape=jax.ShapeDtypeStruct((MLA_HEADS * V_DIM, T), BF16),
        scratch_shapes=[pltpu.VMEM((MLA_HEADS, 1, MLA_TQ), F32),
                        pltpu.VMEM((MLA_HEADS, 1, MLA_TQ), F32),
                        pltpu.VMEM((MLA_HEADS, 1, MLA_TQ), F32),
                        pltpu.VMEM((MLA_HEADS, V_DIM, MLA_TQ), F32),
                        pltpu.VMEM((2 * MLA_HEADS, MLA_TK, MLA_TQ), F32)],
        compiler_params=pltpu.CompilerParams(
            dimension_semantics=("arbitrary", "arbitrary"),
            vmem_limit_bytes=VMEM_LIMIT),
        name="mla_attn",
    )(qT, kn, kr, vT3)


def _block_tail_kernel(oT_ref, x_ref, wo_ref, gpost_ref, gfpre_ref, wg_ref, wu_ref,
                       wd_ref, gfpost_ref, out_ref):
    sub = out_ref.shape[0] // TAIL_SPLIT
    blocks = [slice(i * sub, (i + 1) * sub) for i in range(TAIL_SPLIT)]
    y = [lax.dot_general(oT_ref[:, r], wo_ref[...], (((0,), (0,)), ((), ())),
                         preferred_element_type=F32) for r in blocks]
    x1 = [x_ref[r, :] + _rms(yi, gpost_ref[...]) for r, yi in zip(blocks, y)]
    h = [_rms(xi, gfpre_ref[...]).astype(BF16) for xi in x1]
    g = [_dot(hi, wg_ref[...]) for hi in h]
    u = [_dot(hi, wu_ref[...]) for hi in h]
    a = [(gi * (1.0 / (1.0 + jnp.exp(-gi))) * ui).astype(BF16) for gi, ui in zip(g, u)]
    f = [_dot(ai, wd_ref[...]) for ai in a]
    for r, xi, fi in zip(blocks, x1, f):
        out_ref[r, :] = xi + _rms(fi, gfpost_ref[...])


def _block_tail(oT, x2, wo, gpost, gfpre, wg, wu, wd, gfpost, layer):
    T, D = x2.shape
    tm = TAIL_ROWS
    const = lambda i: (0, 0)
    full = lambda a: pl.BlockSpec(a.shape, const, pipeline_mode=pl.Buffered(1))
    of_layer = lambda a: pl.BlockSpec((None,) + a.shape[1:], lambda i: (layer, 0, 0),
                                      pipeline_mode=pl.Buffered(1))
    row = lambda a: pl.BlockSpec((tm, a.shape[1]), lambda i: (i, 0))
    return pl.pallas_call(
        _block_tail_kernel,
        grid=(T // tm,),
        in_specs=[pl.BlockSpec((oT.shape[0], tm), lambda i: (0, i)), row(x2), full(wo),
                  full(gpost), full(gfpre), of_layer(wg), of_layer(wu), of_layer(wd),
                  full(gfpost)],
        out_specs=pl.BlockSpec((tm, D), lambda i: (i, 0)),
        out_shape=jax.ShapeDtypeStruct((T, D), F32),
        compiler_params=pltpu.CompilerParams(
            dimension_semantics=("arbitrary",), vmem_limit_bytes=VMEM_LIMIT),
        name="block_tail",
    )(oT, x2, wo, gpost, gfpre, wg, wu, wd, gfpost)


def _kvq_proj_kernel(x_ref, gsrc_ref, gpre_ref, wqT_ref, wk_ref, wvT_ref, tab_ref,
                     qT_ref, k_ref, vT_ref, bias_ref, g_s, *, scale):
    i = pl.program_id(0)

    @pl.when(i == 0)
    def _():
        _rel_bias_gather(tab_ref[...], g_s)

    blocks = bias_ref.shape[1] // SUBLANES
    for local in range(blocks):
        _rel_bias_rows(g_s, bias_ref, i * blocks + local, local)
    x = x_ref[...]
    xr = x * lax.rsqrt(jnp.mean(x * x, axis=-1, keepdims=True) + EPS)
    xs = (xr * gsrc_ref[...]).astype(BF16)
    xq = (xr * gpre_ref[...]).astype(BF16)
    qT_ref[...] = (_dot_nt(wqT_ref[...], xq) * scale).astype(BF16)
    k_ref[...] = pltpu.bitcast(_dot(xs, wk_ref[...]).astype(BF16), jnp.uint32)
    vT_ref[...] = _dot_nt(wvT_ref[...], xs).astype(BF16)


def _kvq_proj(x2, gsrc, gpre, wqT, wk, wvT, table):
    T, D = x2.shape
    tm = PROJ_ROWS
    steps = T // tm
    hd = B_HEADS * B_HEAD_DIM
    assert QUAD_BAND % (steps * SUBLANES) == 0
    const = lambda i: (0, 0)
    full = lambda a: pl.BlockSpec(a.shape, const)
    scale = float(B_HEAD_DIM ** -0.5 * math.log2(math.e))
    return pl.pallas_call(
        functools.partial(_kvq_proj_kernel, scale=scale),
        grid=(steps,),
        in_specs=[pl.BlockSpec((tm, D), lambda i: (i, 0)),
                  full(gsrc), full(gpre), full(wqT), full(wk), full(wvT), full(table)],
        out_specs=[pl.BlockSpec((hd, tm), lambda i: (0, i)),
                   pl.BlockSpec((tm // 2, hd), lambda i: (i, 0)),
                   pl.BlockSpec((hd, tm), lambda i: (0, i)),
                   pl.BlockSpec((B_HEADS, QUAD_BAND // steps, QUAD), lambda i: (0, i, 0))],
        out_shape=[jax.ShapeDtypeStruct((hd, T), BF16),
                   jax.ShapeDtypeStruct((T // 2, hd), jnp.uint32),
                   jax.ShapeDtypeStruct((hd, T), BF16),
                   jax.ShapeDtypeStruct((B_HEADS, QUAD_BAND, QUAD), F32)],
        scratch_shapes=[pltpu.VMEM((B_HEADS, BIAS_ROLL), F32)],
        compiler_params=pltpu.CompilerParams(
            dimension_semantics=("arbitrary",), vmem_limit_bytes=VMEM_LIMIT),
        name="kvq_proj",
    )(x2, gsrc, gpre, wqT, wk, wvT, table)


def _rel_bias_gather(tab, g_s):
    t_hi = tab.astype(BF16)
    rem = tab - t_hi.astype(F32)
    t_mid = rem.astype(BF16)
    t_lo = (rem - t_mid.astype(F32)).astype(BF16)
    row = lax.broadcasted_iota(jnp.int32, (REL_TABLE, BIAS_ROLL), 0)
    lane = lax.broadcasted_iota(jnp.int32, (REL_TABLE, BIAS_ROLL), 1)
    dist = jnp.where(lane < QUAD, lane, lane - BIAS_ROLL) + LEFT_CHUNKS * CHUNK
    idx = jnp.clip(dist, -(CHUNK - 1), MAX_REL) + (CHUNK - 1)
    onehot = jnp.where(idx == row, 1.0, 0.0).astype(BF16)
    g_s[...] = (_dot(t_hi, onehot) + _dot(t_mid, onehot) + _dot(t_lo, onehot)) * math.log2(math.e)


def _rel_bias_rows(g_s, out_ref, cb, local):
    c0 = cb * SUBLANES
    q_chunk = lax.broadcasted_iota(jnp.int32, (1, QUAD), 1) // CHUNK + LEFT_CHUNKS
    back = q_chunk - cb // (CHUNK // SUBLANES)
    valid = (back >= 0) & (back <= LEFT_CHUNKS)
    for h in range(B_HEADS):
        rows = jnp.broadcast_to(g_s[h:h + 1, :], (SUBLANES, BIAS_ROLL))
        rolled = pltpu.roll(rows, c0, 1, stride=1, stride_axis=0)
        out_ref[h, local * SUBLANES:(local + 1) * SUBLANES, :] = jnp.where(
            valid, rolled[:, :QUAD], NEG)


def _chunk_attn_kernel(qT_ref, kp_ref, kc_ref, vTp_ref, vTc_ref, bias_ref, o_ref,
                       k4_s, vTq_s, qTm_s, s_s, m_s, oT_s):
    g = pl.program_id(1)
    slab = HEAD_GROUP * B_HEAD_DIM
    n_hg = B_HEADS // HEAD_GROUP
    n_phase = (GROUP // QUAD) * (B_HEADS // PHASE_HEADS)
    half = GROUP // 2
    for hg in range(n_hg):
        lanes = slice(hg * slab, (hg + 1) * slab)
        k4_s[hg, :half, :] = kp_ref[:, lanes]
        k4_s[hg, half:, :] = kc_ref[:, lanes]
    vTq_s[0, :, :GROUP] = vTp_ref[...]
    vTq_s[0, :, GROUP:] = vTc_ref[:, :QUAD]
    vTq_s[1, :, :QUAD] = vTp_ref[:, QUAD:]
    vTq_s[1, :, QUAD:] = vTc_ref[...]
    @pl.when(g == 0)
    def _():
        qTm_s[...] = jnp.zeros(qTm_s.shape, BF16)

    for h in range(B_HEADS):
        own = slice((h % HEAD_GROUP) * B_HEAD_DIM, (h % HEAD_GROUP + 1) * B_HEAD_DIM)
        for qd in range(GROUP // QUAD):
            qTm_s[h, qd, own, :] = qT_ref[h * B_HEAD_DIM:(h + 1) * B_HEAD_DIM,
                                          qd * QUAD:(qd + 1) * QUAD]
    ones = jnp.ones((SUM_ROWS, QUAD_BAND), BF16)

    def phase_ids(t):
        per_quad = B_HEADS // PHASE_HEADS
        slot, qd, part = (t & 1) * PHASE_HEADS, t >> (per_quad.bit_length() - 1), t & (per_quad - 1)
        return slot, qd, part * PHASE_HEADS, part * (PHASE_HEADS // HEAD_GROUP)

    def score_phase(t, first_group):
        slot, qd, head0, hg0 = phase_ids(t)
        start = qd * (QUAD // 2)
        if not isinstance(start, int):
            start = pl.multiple_of(start, QUAD // 2)
        for i in range(PHASE_HEADS):
            h = head0 + i
            kb = pltpu.bitcast(
                k4_s[hg0 + i // HEAD_GROUP, pl.ds(start, QUAD_BAND // 2), :], BF16)
            s = _dot(kb, qTm_s[h, qd]) + bias_ref[h]
            if first_group:
                key = lax.broadcasted_iota(jnp.int32, (QUAD_BAND, 1), 0)
                s = jnp.where(key >= GROUP - qd * QUAD, s, NEG)
            s_s[slot + i] = s
            m_s[slot + i] = jnp.max(s, axis=0, keepdims=True)

    def value_phase(t):
        slot, qd, head0, _ = phase_ids(t)
        for i in range(PHASE_HEADS):
            row0 = (head0 + i) * B_HEAD_DIM
            if not isinstance(row0, int):
                row0 = pl.multiple_of(row0, B_HEAD_DIM)
            p = jnp.exp2(s_s[slot + i] - m_s[slot + i]).astype(BF16)
            lhs = jnp.concatenate([vTq_s[qd, pl.ds(row0, B_HEAD_DIM), :], ones], axis=0)
            r = _dot(lhs, p)
            oT_s[qd, pl.ds(row0, B_HEAD_DIM), :] = (
                r[:B_HEAD_DIM] * (1.0 / r[B_HEAD_DIM:B_HEAD_DIM + 1]))

    def run(first_group):
        score_phase(0, first_group)

        def steady(t, carry):
            value_phase(t - 1)
            score_phase(t, first_group)
            return carry

        lax.fori_loop(1, n_phase, steady, 0)
        value_phase(n_phase - 1)

    pl.when(g == 0)(lambda: run(True))
    pl.when(g != 0)(lambda: run(False))
    for qd in range(GROUP // QUAD):
        o_ref[:, qd * QUAD:(qd + 1) * QUAD] = oT_s[qd].astype(BF16)


def _chunk_attn(qT, k, vT, bias, batch, seq):
    hd, T = qT.shape
    ng = seq // GROUP
    n_quad = GROUP // QUAD
    prev = lambda b, g: b * ng + jnp.maximum(g - 1, 0)
    cur = lambda b, g: b * ng + g
    return pl.pallas_call(
        _chunk_attn_kernel,
        grid=(batch, ng),
        in_specs=[pl.BlockSpec((hd, GROUP), lambda b, g: (0, cur(b, g))),
                  pl.BlockSpec((GROUP // 2, hd), lambda b, g: (prev(b, g), 0)),
                  pl.BlockSpec((GROUP // 2, hd), lambda b, g: (cur(b, g), 0)),
                  pl.BlockSpec((hd, GROUP), lambda b, g: (0, prev(b, g))),
                  pl.BlockSpec((hd, GROUP), lambda b, g: (0, cur(b, g))),
                  pl.BlockSpec(bias.shape, lambda b, g: (0, 0, 0),
                               pipeline_mode=pl.Buffered(1))],
        out_specs=pl.BlockSpec((hd, GROUP), lambda b, g: (0, cur(b, g))),
        out_shape=jax.ShapeDtypeStruct((hd, T), BF16),
        scratch_shapes=[pltpu.VMEM((B_HEADS // HEAD_GROUP, GROUP, HEAD_GROUP * B_HEAD_DIM),
                                   jnp.uint32),
                        pltpu.VMEM((n_quad, hd, QUAD_BAND), BF16),
                        pltpu.VMEM((B_HEADS, n_quad, HEAD_GROUP * B_HEAD_DIM, QUAD), BF16),
                        pltpu.VMEM((2 * PHASE_HEADS, QUAD_BAND, QUAD), F32),
                        pltpu.VMEM((2 * PHASE_HEADS, 1, QUAD), F32),
                        pltpu.VMEM((n_quad, hd, QUAD), F32)],
        compiler_params=pltpu.CompilerParams(
            dimension_semantics=("arbitrary", "arbitrary"), vmem_limit_bytes=VMEM_LIMIT),
        name="chunk_attn",
    )(qT, k, k, vT, vT, bias)


def kernel(x, positions, attn_pre_g, attn_post_g, ffn_pre_g, ffn_post_g, ffn_w_gate,
           ffn_w_up, ffn_w_down, mla_w_a, mla_g_q, mla_w_uq, mla_g_kv, mla_w_ukv, mla_w_o,
           kv_src_g, w_kv_shared, b_w_q, b_rel_table, b_w_o):
    batch, seq, d = x.shape
    T = batch * seq
    assert attn_pre_g.shape[0] == 2 and mla_w_a.shape[0] == 1 and b_w_q.shape[0] == 1
    assert seq % GROUP == 0 and seq % MLA_TQ == 0 and T % PROJ_ROWS == 0
    row = lambda g: g.reshape(1, -1).astype(F32)

    x2 = x.reshape(T, d)
    pos_row = positions.reshape(1, T).astype(F32)
    inv = 1.0 / (ROPE_THETA ** (jnp.arange(0, ROPE_DIM, 2, dtype=F32) / ROPE_DIM))
    inv_col = inv.reshape(ROPE_DIM // 2, 1)

    wa = mla_w_a[0][:, :Q_LORA + KV_LORA].astype(BF16)
    wkrT = jnp.pad(mla_w_a[0][:, Q_LORA + KV_LORA:].T, ((0, LANES - ROPE_DIM), (0, 0))).astype(BF16)
    wuq = mla_w_uq[0].reshape(Q_LORA, MLA_HEADS, NOPE_DIM + ROPE_DIM)
    wuq = jnp.pad(wuq, ((0, 0), (0, 0), (0, MLA_HEAD_PAD - NOPE_DIM - ROPE_DIM)))
    wuqT = wuq.reshape(Q_LORA, MLA_HEADS * MLA_HEAD_PAD).T.astype(BF16)
    wukv = mla_w_ukv[0].reshape(KV_LORA, MLA_HEADS, NOPE_DIM + V_DIM)
    wuk = wukv[:, :, :NOPE_DIM].reshape(KV_LORA, MLA_HEADS * NOPE_DIM).astype(BF16)
    wuvT = wukv[:, :, NOPE_DIM:].reshape(KV_LORA, MLA_HEADS * V_DIM).T.astype(BF16)
    hd = B_HEADS * B_HEAD_DIM
    wk = w_kv_shared[:, :hd].astype(BF16)
    wvT = w_kv_shared[:, hd:].T.astype(BF16)
    wqT = b_w_q[0].T.astype(BF16)

    qT0, kn0, kr0, vT0 = _mla_proj(x2, pos_row, inv_col, row(attn_pre_g[0]), wa, wkrT,
                                   row(mla_g_q[0]), wuqT, row(mla_g_kv[0]), wuk, wuvT)
    o0 = _mla_attn(qT0, kn0, kr0, vT0, batch, seq)
    wg, wu, wd = ffn_w_gate.astype(BF16), ffn_w_up.astype(BF16), ffn_w_down.astype(BF16)
    x2 = _block_tail(o0, x2, mla_w_o[0].astype(BF16), row(attn_post_g[0]), row(ffn_pre_g[0]),
                     wg, wu, wd, row(ffn_post_g[0]), 0)

    qT1, k1, vT1, bias = _kvq_proj(x2, row(kv_src_g), row(attn_pre_g[1]), wqT, wk, wvT,
                                   b_rel_table[0])
    o1 = _chunk_attn(qT1, k1, vT1, bias, batch, seq)
    x2 = _block_tail(o1, x2, b_w_o[0].astype(BF16), row(attn_post_g[1]), row(ffn_pre_g[1]),
                     wg, wu, wd, row(ffn_post_g[1]), 1)
    return x2.reshape(batch, seq, d)
```

```python
import functools
import math

import jax
import jax.numpy as jnp
from jax import lax
from jax.experimental import pallas as pl
from jax.experimental.pallas import tpu as pltpu

F32 = jnp.float32
BF16 = jnp.bfloat16

CHUNK = 64
MLA_HEADS = 8
Q_LORA = 384
KV_LORA = 256
NOPE_DIM = 128
ROPE_DIM = 64
V_DIM = 128
ROPE_THETA = 10000.0
B_HEADS = 16
B_HEAD_DIM = 64
LEFT_CHUNKS = 8
BAND = (LEFT_CHUNKS + 1) * CHUNK
MAX_REL = 256
REL_TABLE = MAX_REL + CHUNK
EPS = 1e-6

LANES = 128
SUBLANES = 8
MLA_HEAD_PAD = 2 * LANES
NEG = -1e30

PROJ_ROWS = 1024
TAIL_ROWS = 512
TAIL_SPLIT = 2
MLA_TQ = 512
MLA_TK = 256
MLA_TRIP_TILES = 4
SUM_ROWS = 16
QUAD = 4 * CHUNK
QUAD_BAND = BAND + 3 * CHUNK
GROUP = 2 * QUAD
HEAD_GROUP = 2
PHASE_HEADS = 8
BIAS_ROLL = 1024
VMEM_LIMIT = 56 * 1024 * 1024


def _rms(x, g):
    ms = jnp.mean(x * x, axis=-1, keepdims=True)
    return x * lax.rsqrt(ms + EPS) * g


def _dot(a, b):
    return jnp.dot(a, b, preferred_element_type=F32)


def _dot_nt(a, b):
    return lax.dot_general(a, b, (((1,), (1,)), ((), ())), preferred_element_type=F32)


def _mla_proj_kernel(x_ref, posr_ref, invc_ref, gpre_ref, wa_ref, wkrT_ref, gq_ref,
                     wuqT_ref, gkv_ref, wuk_ref, wuvT_ref, qT_ref, kn_ref, kr_ref, vT_ref, *,
                     scale):
    h = _rms(x_ref[...], gpre_ref[...]).astype(BF16)
    a = _dot(h, wa_ref[...])
    cqn = _rms(a[:, :Q_LORA], gq_ref[...]).astype(BF16)
    ckvn = _rms(a[:, Q_LORA:], gkv_ref[...]).astype(BF16)
    half = ROPE_DIM // 2

    angT = invc_ref[...] * posr_ref[...]
    cosT, sinT = jnp.cos(angT), jnp.sin(angT)

    krT = _dot_nt(wkrT_ref[...], h)
    k1, k2 = krT[:half], krT[half:ROPE_DIM]
    kr_rotT = jnp.concatenate(
        [k1 * cosT - k2 * sinT, k2 * cosT + k1 * sinT,
         jnp.zeros((LANES - ROPE_DIM, krT.shape[1]), F32)], axis=0)
    kr_ref[...] = pltpu.bitcast(kr_rotT.T.astype(BF16), jnp.uint32)
    kn_ref[...] = pltpu.bitcast(_dot(ckvn, wuk_ref[...]).astype(BF16), jnp.uint32)

    cosT, sinT = cosT * scale, sinT * scale
    qT = _dot_nt(wuqT_ref[...], cqn)
    for hd in range(MLA_HEADS):
        lo = hd * MLA_HEAD_PAD
        t1 = qT[lo + NOPE_DIM:lo + NOPE_DIM + half]
        t2 = qT[lo + NOPE_DIM + half:lo + NOPE_DIM + ROPE_DIM]
        qT_ref[lo:lo + NOPE_DIM] = (qT[lo:lo + NOPE_DIM] * scale).astype(BF16)
        qT_ref[lo + NOPE_DIM:lo + NOPE_DIM + half] = (t1 * cosT - t2 * sinT).astype(BF16)
        qT_ref[lo + NOPE_DIM + half:lo + NOPE_DIM + ROPE_DIM] = (t2 * cosT + t1 * sinT).astype(BF16)
        qT_ref[lo + NOPE_DIM + ROPE_DIM:lo + MLA_HEAD_PAD] = jnp.zeros(
            (MLA_HEAD_PAD - NOPE_DIM - ROPE_DIM, qT.shape[1]), BF16)

    vT = _dot_nt(wuvT_ref[...], ckvn).astype(BF16)
    for t in range(vT_ref.shape[0]):
        vT_ref[t] = vT[:, t * MLA_TK:(t + 1) * MLA_TK]


def _mla_proj(x2, pos_row, inv_col, gpre, wa, wkrT, gq, wuqT, gkv, wuk, wuvT):
    T, D = x2.shape
    tm = PROJ_ROWS
    const = lambda i: (0, 0)
    full = lambda a: pl.BlockSpec(a.shape, const)
    scale = float((NOPE_DIM + ROPE_DIM) ** -0.5 * math.log2(math.e))
    return pl.pallas_call(
        functools.partial(_mla_proj_kernel, scale=scale),
        grid=(T // tm,),
        in_specs=[pl.BlockSpec((tm, D), lambda i: (i, 0)),
                  pl.BlockSpec((1, tm), lambda i: (0, i)),
                  full(inv_col), full(gpre), full(wa), full(wkrT), full(gq), full(wuqT),
                  full(gkv), full(wuk), full(wuvT)],
        out_specs=[pl.BlockSpec((MLA_HEADS * MLA_HEAD_PAD, tm), lambda i: (0, i)),
                   pl.BlockSpec((tm // 2, MLA_HEADS * NOPE_DIM), lambda i: (i, 0)),
                   pl.BlockSpec((tm // 2, LANES), lambda i: (i, 0)),
                   pl.BlockSpec((tm // MLA_TK, MLA_HEADS * V_DIM, MLA_TK), lambda i: (i, 0, 0))],
        out_shape=[jax.ShapeDtypeStruct((MLA_HEADS * MLA_HEAD_PAD, T), BF16),
                   jax.ShapeDtypeStruct((T // 2, MLA_HEADS * NOPE_DIM), jnp.uint32),
                   jax.ShapeDtypeStruct((T // 2, LANES), jnp.uint32),
                   jax.ShapeDtypeStruct((T // MLA_TK, MLA_HEADS * V_DIM, MLA_TK), BF16)],
        compiler_params=pltpu.CompilerParams(
            dimension_semantics=("arbitrary",), vmem_limit_bytes=VMEM_LIMIT),
        name="mla_proj",
    )(x2, pos_row, inv_col, gpre, wa, wkrT, gq, wuqT, gkv, wuk, wuvT)


def _mla_attn_kernel(qT_ref, kn_ref, kr_ref, vT_ref, oT_ref, m_s, l_s, a_s, acc_s, s_s):
    qi = pl.program_id(1)
    n_full = qi * (MLA_TQ // MLA_TK)
    m_s[...] = jnp.full(m_s.shape, NEG, F32)
    l_s[...] = jnp.zeros(l_s.shape, F32)
    a_s[...] = jnp.ones(a_s.shape, F32)
    acc_s[...] = jnp.zeros(acc_s.shape, F32)
    ones = jnp.ones((SUM_ROWS, MLA_TK), BF16)

    def score_pass(j, parity, diag):
        slot = parity * MLA_HEADS
        half = MLA_TK // 2
        rows = pl.ds(pl.multiple_of(j * half, half), half)
        q0 = 0 if diag is None else diag * MLA_TK
        k_rope = pltpu.bitcast(kr_ref[rows, :], BF16)
        for hd in range(MLA_HEADS):
            k_nope = pltpu.bitcast(kn_ref[rows, hd * NOPE_DIM:(hd + 1) * NOPE_DIM], BF16)
            k = jnp.concatenate([k_nope, k_rope], axis=1)
            qT = qT_ref[hd * MLA_HEAD_PAD:(hd + 1) * MLA_HEAD_PAD, q0:]
            sT = _dot(k, qT)
            if diag is not None:
                kc = lax.broadcasted_iota(jnp.int32, sT.shape, 0) // CHUNK
                qc = lax.broadcasted_iota(jnp.int32, sT.shape, 1) // CHUNK
                sT = jnp.where(kc <= qc, sT, NEG)
            s_s[slot + hd, :, q0:] = sT
            m = m_s[hd, :, q0:]
            m_new = jnp.maximum(m, jnp.max(sT, axis=0, keepdims=True))
            a_s[hd, :, q0:] = jnp.exp2(m - m_new)
            m_s[hd, :, q0:] = m_new
            if q0:
                s_s[slot + hd, :, :q0] = jnp.full((MLA_TK, q0), -jnp.inf, F32)
                a_s[hd, :, :q0] = jnp.ones((1, q0), F32)

    def value_pass(j, parity):
        slot = parity * MLA_HEADS
        jv = jnp.where(j < 0, n_full + 1, j)
        for hd in range(MLA_HEADS):
            p = jnp.exp2(s_s[slot + hd] - m_s[hd]).astype(BF16)
            lhs = jnp.concatenate([vT_ref[jv, hd * V_DIM:(hd + 1) * V_DIM, :], ones], axis=0)
            pv = _dot(lhs, p)
            a = a_s[hd]
            acc_s[hd] = a * acc_s[hd] + pv[:V_DIM]
            l_s[hd] = a * l_s[hd] + pv[V_DIM:V_DIM + 1]

    def full_tiles(base, count):
        for u in range(count):
            value_pass(base + u - 1, (u - 1) % 2)
            score_pass(base + u, u % 2, None)

    def steady(i, carry):
        full_tiles(i * MLA_TRIP_TILES, MLA_TRIP_TILES)
        return carry

    score_pass(n_full, 0, 0)
    value_pass(n_full, 0)
    score_pass(n_full + 1, 1, 1)
    n_trips = n_full // MLA_TRIP_TILES
    lax.fori_loop(0, n_trips, steady, 0)
    for left in range(2, MLA_TRIP_TILES, 2):
        @pl.when(n_full - n_trips * MLA_TRIP_TILES == left)
        def _(left=left):
            full_tiles(n_full - left, left)
    value_pass(n_full - 1, 1)
    for hd in range(MLA_HEADS):
        oT_ref[hd * V_DIM:(hd + 1) * V_DIM, :] = (acc_s[hd] * (1.0 / l_s[hd])).astype(BF16)


def _mla_attn(qT, kn, kr, vT3, batch, seq):
    assert MLA_TQ == 2 * MLA_TK and MLA_TK % CHUNK == 0 and MLA_TRIP_TILES % 2 == 0
    T = qT.shape[1]
    nq = seq // MLA_TQ
    nkt = seq // MLA_TK
    return pl.pallas_call(
        _mla_attn_kernel,
        grid=(batch, nq),
        in_specs=[pl.BlockSpec((MLA_HEADS * MLA_HEAD_PAD, MLA_TQ), lambda b, i: (0, b * nq + i)),
                  pl.BlockSpec((seq // 2, MLA_HEADS * NOPE_DIM), lambda b, i: (b, 0)),
                  pl.BlockSpec((seq // 2, LANES), lambda b, i: (b, 0)),
                  pl.BlockSpec((nkt, MLA_HEADS * V_DIM, MLA_TK), lambda b, i: (b, 0, 0),
                               pipeline_mode=pl.Buffered(1))],
        out_specs=pl.BlockSpec((MLA_HEADS * V_DIM, MLA_TQ), lambda b, i: (0, b * nq + i)),
        out_shape=jax.ShapeDtypeStruct((MLA_HEADS * V_DIM, T), BF16),
        scratch_shapes=[pltpu.VMEM((MLA_HEADS, 1, MLA_TQ), F32),
                        pltpu.VMEM((MLA_HEADS, 1, MLA_TQ), F32),
                        pltpu.VMEM((MLA_HEADS, 1, MLA_TQ), F32),
                        pltpu.VMEM((MLA_HEADS, V_DIM, MLA_TQ), F32),
                        pltpu.VMEM((2 * MLA_HEADS, MLA_TK, MLA_TQ), F32)],
        compiler_params=pltpu.CompilerParams(
            dimension_semantics=("arbitrary", "arbitrary"),
            vmem_limit_bytes=VMEM_LIMIT),
        name="mla_attn",
    )(qT, kn, kr, vT3)


def _block_tail_kernel(oT_ref, x_ref, wo_ref, gpost_ref, gfpre_ref, wg_ref, wu_ref,
                       wd_ref, gfpost_ref, out_ref):
    sub = out_ref.shape[0] // TAIL_SPLIT
    blocks = [slice(i * sub, (i + 1) * sub) for i in range(TAIL_SPLIT)]
    y = [lax.dot_general(oT_ref[:, r], wo_ref[...], (((0,), (0,)), ((), ())),
                         preferred_element_type=F32) for r in blocks]
    x1 = [x_ref[r, :] + _rms(yi, gpost_ref[...]) for r, yi in zip(blocks, y)]
    h = [_rms(xi, gfpre_ref[...]).astype(BF16) for xi in x1]
    gu = [(_dot(hi, wg_ref[...]), _dot(hi, wu_ref[...])) for hi in h]
    a = [(gi * (1.0 / (1.0 + jnp.exp(-gi))) * ui).astype(BF16) for gi, ui in gu]
    f = [_dot(ai, wd_ref[...]) for ai in a]
    for r, xi, fi in zip(blocks, x1, f):
        out_ref[r, :] = xi + _rms(fi, gfpost_ref[...])


def _block_tail(oT, x2, wo, gpost, gfpre, wg, wu, wd, gfpost, layer):
    T, D = x2.shape
    tm = TAIL_ROWS
    const = lambda i: (0, 0)
    full = lambda a: pl.BlockSpec(a.shape, const, pipeline_mode=pl.Buffered(1))
    of_layer = lambda a: pl.BlockSpec((None,) + a.shape[1:], lambda i: (layer, 0, 0),
                                      pipeline_mode=pl.Buffered(1))
    row = lambda a: pl.BlockSpec((tm, a.shape[1]), lambda i: (i, 0))
    return pl.pallas_call(
        _block_tail_kernel,
        grid=(T // tm,),
        in_specs=[pl.BlockSpec((oT.shape[0], tm), lambda i: (0, i)), row(x2), full(wo),
                  full(gpost), full(gfpre), of_layer(wg), of_layer(wu), of_layer(wd),
                  full(gfpost)],
        out_specs=pl.BlockSpec((tm, D), lambda i: (i, 0)),
        out_shape=jax.ShapeDtypeStruct((T, D), F32),
        compiler_params=pltpu.CompilerParams(
            dimension_semantics=("arbitrary",), vmem_limit_bytes=VMEM_LIMIT),
        name="block_tail",
    )(oT, x2, wo, gpost, gfpre, wg, wu, wd, gfpost)


def _kvq_proj_kernel(x_ref, gsrc_ref, gpre_ref, wqT_ref, wk_ref, wvT_ref, tab_ref,
                     qT_ref, k_ref, vT_ref, bias_ref, g_s, *, scale):
    i = pl.program_id(0)

    @pl.when(i == 0)
    def _():
        _rel_bias_gather(tab_ref[...], g_s)

    blocks = bias_ref.shape[1] // SUBLANES
    for local in range(blocks):
        _rel_bias_rows(g_s, bias_ref, i * blocks + local, local)
    x = x_ref[...]
    xr = x * lax.rsqrt(jnp.mean(x * x, axis=-1, keepdims=True) + EPS)
    xs = (xr * gsrc_ref[...]).astype(BF16)
    xq = (xr * gpre_ref[...]).astype(BF16)
    qT_ref[...] = (_dot_nt(wqT_ref[...], xq) * scale).astype(BF16)
    k_ref[...] = pltpu.bitcast(_dot(xs, wk_ref[...]).astype(BF16), jnp.uint32)
    vT_ref[...] = _dot_nt(wvT_ref[...], xs).astype(BF16)


def _kvq_proj(x2, gsrc, gpre, wqT, wk, wvT, table):
    T, D = x2.shape
    tm = PROJ_ROWS
    steps = T // tm
    hd = B_HEADS * B_HEAD_DIM
    assert QUAD_BAND % (steps * SUBLANES) == 0
    const = lambda i: (0, 0)
    full = lambda a: pl.BlockSpec(a.shape, const)
    scale = float(B_HEAD_DIM ** -0.5 * math.log2(math.e))
    return pl.pallas_call(
        functools.partial(_kvq_proj_kernel, scale=scale),
        grid=(steps,),
        in_specs=[pl.BlockSpec((tm, D), lambda i: (i, 0)),
                  full(gsrc), full(gpre), full(wqT), full(wk), full(wvT), full(table)],
        out_specs=[pl.BlockSpec((hd, tm), lambda i: (0, i)),
                   pl.BlockSpec((tm // 2, hd), lambda i: (i, 0)),
                   pl.BlockSpec((hd, tm), lambda i: (0, i)),
                   pl.BlockSpec((B_HEADS, QUAD_BAND // steps, QUAD), lambda i: (0, i, 0))],
        out_shape=[jax.ShapeDtypeStruct((hd, T), BF16),
                   jax.ShapeDtypeStruct((T // 2, hd), jnp.uint32),
                   jax.ShapeDtypeStruct((hd, T), BF16),
                   jax.ShapeDtypeStruct((B_HEADS, QUAD_BAND, QUAD), F32)],
        scratch_shapes=[pltpu.VMEM((B_HEADS, BIAS_ROLL), F32)],
        compiler_params=pltpu.CompilerParams(
            dimension_semantics=("arbitrary",), vmem_limit_bytes=VMEM_LIMIT),
        name="kvq_proj",
    )(x2, gsrc, gpre, wqT, wk, wvT, table)


def _rel_bias_gather(tab, g_s):
    t_hi = tab.astype(BF16)
    rem = tab - t_hi.astype(F32)
    t_mid = rem.astype(BF16)
    t_lo = (rem - t_mid.astype(F32)).astype(BF16)
    row = lax.broadcasted_iota(jnp.int32, (REL_TABLE, BIAS_ROLL), 0)
    lane = lax.broadcasted_iota(jnp.int32, (REL_TABLE, BIAS_ROLL), 1)
    dist = jnp.where(lane < QUAD, lane, lane - BIAS_ROLL) + LEFT_CHUNKS * CHUNK
    idx = jnp.clip(dist, -(CHUNK - 1), MAX_REL) + (CHUNK - 1)
    onehot = jnp.where(idx == row, 1.0, 0.0).astype(BF16)
    g_s[...] = (_dot(t_hi, onehot) + _dot(t_mid, onehot) + _dot(t_lo, onehot)) * math.log2(math.e)


def _rel_bias_rows(g_s, out_ref, cb, local):
    c0 = cb * SUBLANES
    q_chunk = lax.broadcasted_iota(jnp.int32, (1, QUAD), 1) // CHUNK + LEFT_CHUNKS
    back = q_chunk - cb // (CHUNK // SUBLANES)
    valid = (back >= 0) & (back <= LEFT_CHUNKS)
    for h in range(B_HEADS):
        rows = jnp.broadcast_to(g_s[h:h + 1, :], (SUBLANES, BIAS_ROLL))
        rolled = pltpu.roll(rows, c0, 1, stride=1, stride_axis=0)
        out_ref[h, local * SUBLANES:(local + 1) * SUBLANES, :] = jnp.where(
            valid, rolled[:, :QUAD], NEG)


def _chunk_attn_kernel(qT_ref, kp_ref, kc_ref, vTp_ref, vTc_ref, bias_ref, o_ref,
                       k4_s, vTq_s, qTm_s, s_s, m_s, oT_s):
    g = pl.program_id(1)
    slab = HEAD_GROUP * B_HEAD_DIM
    n_hg = B_HEADS // HEAD_GROUP
    n_phase = (GROUP // QUAD) * (B_HEADS // PHASE_HEADS)
    half = GROUP // 2
    for hg in range(n_hg):
        lanes = slice(hg * slab, (hg + 1) * slab)
        k4_s[hg, :half, :] = kp_ref[:, lanes]
        k4_s[hg, half:, :] = kc_ref[:, lanes]
    vTq_s[0, :, :GROUP] = vTp_ref[...]
    vTq_s[0, :, GROUP:] = vTc_ref[:, :QUAD]
    vTq_s[1, :, :QUAD] = vTp_ref[:, QUAD:]
    vTq_s[1, :, QUAD:] = vTc_ref[...]
    @pl.when(g == 0)
    def _():
        qTm_s[...] = jnp.zeros(qTm_s.shape, BF16)

    for h in range(B_HEADS):
        own = slice((h % HEAD_GROUP) * B_HEAD_DIM, (h % HEAD_GROUP + 1) * B_HEAD_DIM)
        for qd in range(GROUP // QUAD):
            qTm_s[h, qd, own, :] = qT_ref[h * B_HEAD_DIM:(h + 1) * B_HEAD_DIM,
                                          qd * QUAD:(qd + 1) * QUAD]
    ones = jnp.ones((SUM_ROWS, QUAD_BAND), BF16)

    def phase_ids(t):
        per_quad = B_HEADS // PHASE_HEADS
        slot, qd, part = (t & 1) * PHASE_HEADS, t >> (per_quad.bit_length() - 1), t & (per_quad - 1)
        return slot, qd, part * PHASE_HEADS, part * (PHASE_HEADS // HEAD_GROUP)

    def score_phase(t, first_group):
        slot, qd, head0, hg0 = phase_ids(t)
        start = qd * (QUAD // 2)
        if not isinstance(start, int):
            start = pl.multiple_of(start, QUAD // 2)
        for i in range(PHASE_HEADS):
            h = head0 + i
            kb = pltpu.bitcast(
                k4_s[hg0 + i // HEAD_GROUP, pl.ds(start, QUAD_BAND // 2), :], BF16)
            s = _dot(kb, qTm_s[h, qd]) + bias_ref[h]
            if first_group:
                key = lax.broadcasted_iota(jnp.int32, (QUAD_BAND, 1), 0)
                s = jnp.where(key >= GROUP - qd * QUAD, s, NEG)
            s_s[slot + i] = s
            m_s[slot + i] = jnp.max(s, axis=0, keepdims=True)

    def value_phase(t):
        slot, qd, head0, _ = phase_ids(t)
        for i in range(PHASE_HEADS):
            row0 = (head0 + i) * B_HEAD_DIM
            if not isinstance(row0, int):
                row0 = pl.multiple_of(row0, B_HEAD_DIM)
            p = jnp.exp2(s_s[slot + i] - m_s[slot + i]).astype(BF16)
            lhs = jnp.concatenate([vTq_s[qd, pl.ds(row0, B_HEAD_DIM), :], ones], axis=0)
            r = _dot(lhs, p)
            oT_s[qd, pl.ds(row0, B_HEAD_DIM), :] = (
                r[:B_HEAD_DIM] * (1.0 / r[B_HEAD_DIM:B_HEAD_DIM + 1]))

    def run(first_group):
        score_phase(0, first_group)

        def steady(t, carry):
            value_phase(t - 1)
            score_phase(t, first_group)
            return carry

        lax.fori_loop(1, n_phase, steady, 0)
        value_phase(n_phase - 1)

    pl.when(g == 0)(lambda: run(True))
    pl.when(g != 0)(lambda: run(False))
    for qd in range(GROUP // QUAD):
        o_ref[:, qd * QUAD:(qd + 1) * QUAD] = oT_s[qd].astype(BF16)


def _chunk_attn(qT, k, vT, bias, batch, seq):
    hd, T = qT.shape
    ng = seq // GROUP
    n_quad = GROUP // QUAD
    prev = lambda b, g: b * ng + jnp.maximum(g - 1, 0)
    cur = lambda b, g: b * ng + g
    return pl.pallas_call(
        _chunk_attn_kernel,
        grid=(batch, ng),
        in_specs=[pl.BlockSpec((hd, GROUP), lambda b, g: (0, cur(b, g))),
                  pl.BlockSpec((GROUP // 2, hd), lambda b, g: (prev(b, g), 0)),
                  pl.BlockSpec((GROUP // 2, hd), lambda b, g: (cur(b, g), 0)),
                  pl.BlockSpec((hd, GROUP), lambda b, g: (0, prev(b, g))),
                  pl.BlockSpec((hd, GROUP), lambda b, g: (0, cur(b, g))),
                  pl.BlockSpec(bias.shape, lambda b, g: (0, 0, 0),
                               pipeline_mode=pl.Buffered(1))],
        out_specs=pl.BlockSpec((hd, GROUP), lambda b, g: (0, cur(b, g))),
        out_shape=jax.ShapeDtypeStruct((hd, T), BF16),
        scratch_shapes=[pltpu.VMEM((B_HEADS // HEAD_GROUP, GROUP, HEAD_GROUP * B_HEAD_DIM),
                                   jnp.uint32),
                        pltpu.VMEM((n_quad, hd, QUAD_BAND), BF16),
                        pltpu.VMEM((B_HEADS, n_quad, HEAD_GROUP * B_HEAD_DIM, QUAD), BF16),
                        pltpu.VMEM((2 * PHASE_HEADS, QUAD_BAND, QUAD), F32),
                        pltpu.VMEM((2 * PHASE_HEADS, 1, QUAD), F32),
                        pltpu.VMEM((n_quad, hd, QUAD), F32)],
        compiler_params=pltpu.CompilerParams(
            dimension_semantics=("arbitrary", "arbitrary"), vmem_limit_bytes=VMEM_LIMIT),
        name="chunk_attn",
    )(qT, k, k, vT, vT, bias)


def kernel(x, positions, attn_pre_g, attn_post_g, ffn_pre_g, ffn_post_g, ffn_w_gate,
           ffn_w_up, ffn_w_down, mla_w_a, mla_g_q, mla_w_uq, mla_g_kv, mla_w_ukv, mla_w_o,
           kv_src_g, w_kv_shared, b_w_q, b_rel_table, b_w_o):
    batch, seq, d = x.shape
    T = batch * seq
    assert attn_pre_g.shape[0] == 2 and mla_w_a.shape[0] == 1 and b_w_q.shape[0] == 1
    assert seq % GROUP == 0 and seq % MLA_TQ == 0 and T % PROJ_ROWS == 0
    row = lambda g: g.reshape(1, -1).astype(F32)

    x2 = x.reshape(T, d)
    pos_row = positions.reshape(1, T).astype(F32)
    inv = 1.0 / (ROPE_THETA ** (jnp.arange(0, ROPE_DIM, 2, dtype=F32) / ROPE_DIM))
    inv_col = inv.reshape(ROPE_DIM // 2, 1)

    wa = mla_w_a[0][:, :Q_LORA + KV_LORA].astype(BF16)
    wkrT = jnp.pad(mla_w_a[0][:, Q_LORA + KV_LORA:].T, ((0, LANES - ROPE_DIM), (0, 0))).astype(BF16)
    wuq = mla_w_uq[0].reshape(Q_LORA, MLA_HEADS, NOPE_DIM + ROPE_DIM)
    wuq = jnp.pad(wuq, ((0, 0), (0, 0), (0, MLA_HEAD_PAD - NOPE_DIM - ROPE_DIM)))
    wuqT = wuq.reshape(Q_LORA, MLA_HEADS * MLA_HEAD_PAD).T.astype(BF16)
    wukv = mla_w_ukv[0].reshape(KV_LORA, MLA_HEADS, NOPE_DIM + V_DIM)
    wuk = wukv[:, :, :NOPE_DIM].reshape(KV_LORA, MLA_HEADS * NOPE_DIM).astype(BF16)
    wuvT = wukv[:, :, NOPE_DIM:].reshape(KV_LORA, MLA_HEADS * V_DIM).T.astype(BF16)
    hd = B_HEADS * B_HEAD_DIM
    wk = w_kv_shared[:, :hd].astype(BF16)
    wvT = w_kv_shared[:, hd:].T.astype(BF16)
    wqT = b_w_q[0].T.astype(BF16)

    qT0, kn0, kr0, vT0 = _mla_proj(x2, pos_row, inv_col, row(attn_pre_g[0]), wa, wkrT,
                                   row(mla_g_q[0]), wuqT, row(mla_g_kv[0]), wuk, wuvT)
    o0 = _mla_attn(qT0, kn0, kr0, vT0, batch, seq)
    wg, wu, wd = ffn_w_gate.astype(BF16), ffn_w_up.astype(BF16), ffn_w_down.astype(BF16)
    x2 = _block_tail(o0, x2, mla_w_o[0].astype(BF16), row(attn_post_g[0]), row(ffn_pre_g[0]),
                     wg, wu, wd, row(ffn_post_g[0]), 0)

    qT1, k1, vT1, bias = _kvq_proj(x2, row(kv_src_g), row(attn_pre_g[1]), wqT, wk, wvT,
                                   b_rel_table[0])
    o1 = _chunk_attn(qT1, k1, vT1, bias, batch, seq)
    x2 = _block_tail(o1, x2, b_w_o[0].astype(BF16), row(attn_post_g[1]), row(ffn_pre_g[1]),
                     wg, wu, wd, row(ffn_post_g[1]), 1)
    return x2.reshape(batch, seq, d)
```

```python
import functools
import math

import jax
import jax.numpy as jnp
from jax import lax
from jax.experimental import pallas as pl
from jax.experimental.pallas import tpu as pltpu

F32 = jnp.float32
BF16 = jnp.bfloat16

CHUNK = 64
MLA_HEADS = 8
Q_LORA = 384
KV_LORA = 256
NOPE_DIM = 128
ROPE_DIM = 64
V_DIM = 128
ROPE_THETA = 10000.0
B_HEADS = 16
B_HEAD_DIM = 64
LEFT_CHUNKS = 8
BAND = (LEFT_CHUNKS + 1) * CHUNK
MAX_REL = 256
REL_TABLE = MAX_REL + CHUNK
EPS = 1e-6

LANES = 128
SUBLANES = 8
MLA_HEAD_PAD = 2 * LANES
NEG = -1e30

PROJ_ROWS = 1024
PROJ_SPLIT = 2
TAIL_ROWS = 512
TAIL_SPLIT = 2
MLA_TQ = 512
MLA_TK = 256
MLA_TRIP_TILES = 4
SUM_ROWS = 16
QUAD = 4 * CHUNK
QUAD_BAND = BAND + 3 * CHUNK
GROUP = 2 * QUAD
HEAD_GROUP = 2
PHASE_HEADS = 8
BIAS_ROLL = 1024
VMEM_LIMIT = 56 * 1024 * 1024


def _rms(x, g):
    ms = jnp.mean(x * x, axis=-1, keepdims=True)
    return x * lax.rsqrt(ms + EPS) * g


def _dot(a, b):
    return jnp.dot(a, b, preferred_element_type=F32)


def _dot_nt(a, b):
    return lax.dot_general(a, b, (((1,), (1,)), ((), ())), preferred_element_type=F32)


def _mla_proj_kernel(x_ref, posr_ref, invc_ref, gpre_ref, wa_ref, wkrT_ref, gq_ref,
                     wuqT_ref, gkv_ref, wuk_ref, wuvT_ref, qT_ref, kn_ref, kr_ref, vT_ref, *,
                     scale):
    sub = x_ref.shape[0] // PROJ_SPLIT
    blocks = [slice(n * sub, (n + 1) * sub) for n in range(PROJ_SPLIT)]
    half = ROPE_DIM // 2
    h = [_rms(x_ref[r, :], gpre_ref[...]).astype(BF16) for r in blocks]
    a = [_dot(hn, wa_ref[...]) for hn in h]
    krT = [_dot_nt(wkrT_ref[...], hn) for hn in h]
    cqn = [_rms(an[:, :Q_LORA], gq_ref[...]).astype(BF16) for an in a]
    ckvn = [_rms(an[:, Q_LORA:], gkv_ref[...]).astype(BF16) for an in a]
    qT = [_dot_nt(wuqT_ref[...], c) for c in cqn]
    kn = [_dot(c, wuk_ref[...]) for c in ckvn]
    vT = [_dot_nt(wuvT_ref[...], c).astype(BF16) for c in ckvn]
    for n, r in enumerate(blocks):
        packed = slice(r.start // 2, r.stop // 2)
        angT = invc_ref[...] * posr_ref[:, r]
        cosT, sinT = jnp.cos(angT), jnp.sin(angT)
        k1, k2 = krT[n][:half], krT[n][half:ROPE_DIM]
        kr_rotT = jnp.concatenate(
            [k1 * cosT - k2 * sinT, k2 * cosT + k1 * sinT,
             jnp.zeros((LANES - ROPE_DIM, sub), F32)], axis=0)
        kr_ref[packed, :] = pltpu.bitcast(kr_rotT.T.astype(BF16), jnp.uint32)
        kn_ref[packed, :] = pltpu.bitcast(kn[n].astype(BF16), jnp.uint32)
        cosT, sinT = cosT * scale, sinT * scale
        for hd in range(MLA_HEADS):
            lo = hd * MLA_HEAD_PAD
            t1 = qT[n][lo + NOPE_DIM:lo + NOPE_DIM + half]
            t2 = qT[n][lo + NOPE_DIM + half:lo + NOPE_DIM + ROPE_DIM]
            qT_ref[lo:lo + NOPE_DIM, r] = (qT[n][lo:lo + NOPE_DIM] * scale).astype(BF16)
            qT_ref[lo + NOPE_DIM:lo + NOPE_DIM + half, r] = (t1 * cosT - t2 * sinT).astype(BF16)
            qT_ref[lo + NOPE_DIM + half:lo + NOPE_DIM + ROPE_DIM, r] = (
                t2 * cosT + t1 * sinT).astype(BF16)
            qT_ref[lo + NOPE_DIM + ROPE_DIM:lo + MLA_HEAD_PAD, r] = jnp.zeros(
                (MLA_HEAD_PAD - NOPE_DIM - ROPE_DIM, sub), BF16)
        tiles = sub // MLA_TK
        for t in range(tiles):
            vT_ref[n * tiles + t] = vT[n][:, t * MLA_TK:(t + 1) * MLA_TK]


def _mla_proj(x2, pos_row, inv_col, gpre, wa, wkrT, gq, wuqT, gkv, wuk, wuvT):
    T, D = x2.shape
    tm = PROJ_ROWS
    const = lambda i: (0, 0)
    full = lambda a: pl.BlockSpec(a.shape, const)
    scale = float((NOPE_DIM + ROPE_DIM) ** -0.5 * math.log2(math.e))
    return pl.pallas_call(
        functools.partial(_mla_proj_kernel, scale=scale),
        grid=(T // tm,),
        in_specs=[pl.BlockSpec((tm, D), lambda i: (i, 0)),
                  pl.BlockSpec((1, tm), lambda i: (0, i)),
                  full(inv_col), full(gpre), full(wa), full(wkrT), full(gq), full(wuqT),
                  full(gkv), full(wuk), full(wuvT)],
        out_specs=[pl.BlockSpec((MLA_HEADS * MLA_HEAD_PAD, tm), lambda i: (0, i)),
                   pl.BlockSpec((tm // 2, MLA_HEADS * NOPE_DIM), lambda i: (i, 0)),
                   pl.BlockSpec((tm // 2, LANES), lambda i: (i, 0)),
                   pl.BlockSpec((tm // MLA_TK, MLA_HEADS * V_DIM, MLA_TK), lambda i: (i, 0, 0))],
        out_shape=[jax.ShapeDtypeStruct((MLA_HEADS * MLA_HEAD_PAD, T), BF16),
                   jax.ShapeDtypeStruct((T // 2, MLA_HEADS * NOPE_DIM), jnp.uint32),
                   jax.ShapeDtypeStruct((T // 2, LANES), jnp.uint32),
                   jax.ShapeDtypeStruct((T // MLA_TK, MLA_HEADS * V_DIM, MLA_TK), BF16)],
        compiler_params=pltpu.CompilerParams(
            dimension_semantics=("arbitrary",), vmem_limit_bytes=VMEM_LIMIT),
        name="mla_proj",
    )(x2, pos_row, inv_col, gpre, wa, wkrT, gq, wuqT, gkv, wuk, wuvT)


def _mla_attn_kernel(qT_ref, kn_ref, kr_ref, vT_ref, oT_ref, m_s, l_s, a_s, acc_s, s_s):
    qi = pl.program_id(1)
    n_full = qi * (MLA_TQ // MLA_TK)
    m_s[...] = jnp.full(m_s.shape, NEG, F32)
    l_s[...] = jnp.zeros(l_s.shape, F32)
    a_s[...] = jnp.ones(a_s.shape, F32)
    acc_s[...] = jnp.zeros(acc_s.shape, F32)
    ones = jnp.ones((SUM_ROWS, MLA_TK), BF16)

    def score_pass(j, parity, diag):
        slot = parity * MLA_HEADS
        half = MLA_TK // 2
        rows = pl.ds(pl.multiple_of(j * half, half), half)
        q0 = 0 if diag is None else diag * MLA_TK
        k_rope = pltpu.bitcast(kr_ref[rows, :], BF16)
        for hd in range(MLA_HEADS):
            k_nope = pltpu.bitcast(kn_ref[rows, hd * NOPE_DIM:(hd + 1) * NOPE_DIM], BF16)
            k = jnp.concatenate([k_nope, k_rope], axis=1)
            qT = qT_ref[hd * MLA_HEAD_PAD:(hd + 1) * MLA_HEAD_PAD, q0:]
            sT = _dot(k, qT)
            if diag is not None:
                kc = lax.broadcasted_iota(jnp.int32, sT.shape, 0) // CHUNK
                qc = lax.broadcasted_iota(jnp.int32, sT.shape, 1) // CHUNK
                sT = jnp.where(kc <= qc, sT, NEG)
            s_s[slot + hd, :, q0:] = sT
            m = m_s[hd, :, q0:]
            m_new = jnp.maximum(m, jnp.max(sT, axis=0, keepdims=True))
            a_s[hd, :, q0:] = jnp.exp2(m - m_new)
            m_s[hd, :, q0:] = m_new
            if q0:
                s_s[slot + hd, :, :q0] = jnp.full((MLA_TK, q0), -jnp.inf, F32)
                a_s[hd, :, :q0] = jnp.ones((1, q0), F32)

    def value_pass(j, parity):
        slot = parity * MLA_HEADS
        jv = jnp.where(j < 0, n_full + 1, j)
        for hd in range(MLA_HEADS):
            p = jnp.exp2(s_s[slot + hd] - m_s[hd]).astype(BF16)
            lhs = jnp.concatenate([vT_ref[jv, hd * V_DIM:(hd + 1) * V_DIM, :], ones], axis=0)
            pv = _dot(lhs, p)
            a = a_s[hd]
            acc_s[hd] = a * acc_s[hd] + pv[:V_DIM]
            l_s[hd] = a * l_s[hd] + pv[V_DIM:V_DIM + 1]

    def full_tiles(base, count):
        for u in range(count):
            value_pass(base + u - 1, (u - 1) % 2)
            score_pass(base + u, u % 2, None)

    def steady(i, carry):
        full_tiles(i * MLA_TRIP_TILES, MLA_TRIP_TILES)
        return carry

    score_pass(n_full, 0, 0)
    value_pass(n_full, 0)
    score_pass(n_full + 1, 1, 1)
    n_trips = n_full // MLA_TRIP_TILES
    lax.fori_loop(0, n_trips, steady, 0)
    for left in range(2, MLA_TRIP_TILES, 2):
        @pl.when(n_full - n_trips * MLA_TRIP_TILES == left)
        def _(left=left):
            full_tiles(n_full - left, left)
    value_pass(n_full - 1, 1)
    for hd in range(MLA_HEADS):
        oT_ref[hd * V_DIM:(hd + 1) * V_DIM, :] = (acc_s[hd] * (1.0 / l_s[hd])).astype(BF16)


def _mla_attn(qT, kn, kr, vT3, batch, seq):
    assert MLA_TQ == 2 * MLA_TK and MLA_TK % CHUNK == 0 and MLA_TRIP_TILES % 2 == 0
    T = qT.shape[1]
    nq = seq // MLA_TQ
    nkt = seq // MLA_TK
    return pl.pallas_call(
        _mla_attn_kernel,
        grid=(batch, nq),
        in_specs=[pl.BlockSpec((MLA_HEADS * MLA_HEAD_PAD, MLA_TQ), lambda b, i: (0, b * nq + i)),
                  pl.BlockSpec((seq // 2, MLA_HEADS * NOPE_DIM), lambda b, i: (b, 0)),
                  pl.BlockSpec((seq // 2, LANES), lambda b, i: (b, 0)),
                  pl.BlockSpec((nkt, MLA_HEADS * V_DIM, MLA_TK), lambda b, i: (b, 0, 0),
                               pipeline_mode=pl.Buffered(1))],
        out_specs=pl.BlockSpec((MLA_HEADS * V_DIM, MLA_TQ), lambda b, i: (0, b * nq + i)),
        out_shape=jax.ShapeDtypeStruct((MLA_HEADS * V_DIM, T), BF16),
        scratch_shapes=[pltpu.VMEM((MLA_HEADS, 1, MLA_TQ), F32),
                        pltpu.VMEM((MLA_HEADS, 1, MLA_TQ), F32),
                        pltpu.VMEM((MLA_HEADS, 1, MLA_TQ), F32),
                        pltpu.VMEM((MLA_HEADS, V_DIM, MLA_TQ), F32),
                        pltpu.VMEM((2 * MLA_HEADS, MLA_TK, MLA_TQ), F32)],
        compiler_params=pltpu.CompilerParams(
            dimension_semantics=("arbitrary", "arbitrary"),
            vmem_limit_bytes=VMEM_LIMIT),
        name="mla_attn",
    )(qT, kn, kr, vT3)


def _block_tail_kernel(oT_ref, x_ref, wo_ref, gpost_ref, gfpre_ref, wg_ref, wu_ref,
                       wd_ref, gfpost_ref, out_ref):
    sub = out_ref.shape[0] // TAIL_SPLIT
    blocks = [slice(i * sub, (i + 1) * sub) for i in range(TAIL_SPLIT)]
    y = [lax.dot_general(oT_ref[:, r], wo_ref[...], (((0,), (0,)), ((), ())),
                         preferred_element_type=F32) for r in blocks]
    x1 = [x_ref[r, :] + _rms(yi, gpost_ref[...]) for r, yi in zip(blocks, y)]
    h = [_rms(xi, gfpre_ref[...]).astype(BF16) for xi in x1]
    gu = [(_dot(hi, wg_ref[...]), _dot(hi, wu_ref[...])) for hi in h]
    a = [(gi * (1.0 / (1.0 + jnp.exp(-gi))) * ui).astype(BF16) for gi, ui in gu]
    f = [_dot(ai, wd_ref[...]) for ai in a]
    for r, xi, fi in zip(blocks, x1, f):
        out_ref[r, :] = xi + _rms(fi, gfpost_ref[...])


def _block_tail(oT, x2, wo, gpost, gfpre, wg, wu, wd, gfpost, layer):
    T, D = x2.shape
    tm = TAIL_ROWS
    const = lambda i: (0, 0)
    full = lambda a: pl.BlockSpec(a.shape, const, pipeline_mode=pl.Buffered(1))
    of_layer = lambda a: pl.BlockSpec((None,) + a.shape[1:], lambda i: (layer, 0, 0),
                                      pipeline_mode=pl.Buffered(1))
    row = lambda a: pl.BlockSpec((tm, a.shape[1]), lambda i: (i, 0))
    return pl.pallas_call(
        _block_tail_kernel,
        grid=(T // tm,),
        in_specs=[pl.BlockSpec((oT.shape[0], tm), lambda i: (0, i)), row(x2), full(wo),
                  full(gpost), full(gfpre), of_layer(wg), of_layer(wu), of_layer(wd),
                  full(gfpost)],
        out_specs=pl.BlockSpec((tm, D), lambda i: (i, 0)),
        out_shape=jax.ShapeDtypeStruct((T, D), F32),
        compiler_params=pltpu.CompilerParams(
            dimension_semantics=("arbitrary",), vmem_limit_bytes=VMEM_LIMIT),
        name="block_tail",
    )(oT, x2, wo, gpost, gfpre, wg, wu, wd, gfpost)


def _kvq_proj_kernel(x_ref, gsrc_ref, gpre_ref, wqT_ref, wk_ref, wvT_ref, tab_ref,
                     qT_ref, k_ref, vT_ref, bias_ref, g_s, *, scale):
    i = pl.program_id(0)

    @pl.when(i == 0)
    def _():
        _rel_bias_gather(tab_ref[...], g_s)

    blocks = bias_ref.shape[1] // SUBLANES
    for local in range(blocks):
        _rel_bias_rows(g_s, bias_ref, i * blocks + local, local)
    x = x_ref[...]
    xr = x * lax.rsqrt(jnp.mean(x * x, axis=-1, keepdims=True) + EPS)
    xs = (xr * gsrc_ref[...]).astype(BF16)
    xq = (xr * gpre_ref[...]).astype(BF16)
    qT_ref[...] = (_dot_nt(wqT_ref[...], xq) * scale).astype(BF16)
    k_ref[...] = pltpu.bitcast(_dot(xs, wk_ref[...]).astype(BF16), jnp.uint32)
    vT_ref[...] = _dot_nt(wvT_ref[...], xs).astype(BF16)


def _kvq_proj(x2, gsrc, gpre, wqT, wk, wvT, table):
    T, D = x2.shape
    tm = PROJ_ROWS
    steps = T // tm
    hd = B_HEADS * B_HEAD_DIM
    assert QUAD_BAND % (steps * SUBLANES) == 0
    const = lambda i: (0, 0)
    full = lambda a: pl.BlockSpec(a.shape, const)
    scale = float(B_HEAD_DIM ** -0.5 * math.log2(math.e))
    return pl.pallas_call(
        functools.partial(_kvq_proj_kernel, scale=scale),
        grid=(steps,),
        in_specs=[pl.BlockSpec((tm, D), lambda i: (i, 0)),
                  full(gsrc), full(gpre), full(wqT), full(wk), full(wvT), full(table)],
        out_specs=[pl.BlockSpec((hd, tm), lambda i: (0, i)),
                   pl.BlockSpec((tm // 2, hd), lambda i: (i, 0)),
                   pl.BlockSpec((hd, tm), lambda i: (0, i)),
                   pl.BlockSpec((B_HEADS, QUAD_BAND // steps, QUAD), lambda i: (0, i, 0))],
        out_shape=[jax.ShapeDtypeStruct((hd, T), BF16),
                   jax.ShapeDtypeStruct((T // 2, hd), jnp.uint32),
                   jax.ShapeDtypeStruct((hd, T), BF16),
                   jax.ShapeDtypeStruct((B_HEADS, QUAD_BAND, QUAD), F32)],
        scratch_shapes=[pltpu.VMEM((B_HEADS, BIAS_ROLL), F32)],
        compiler_params=pltpu.CompilerParams(
            dimension_semantics=("arbitrary",), vmem_limit_bytes=VMEM_LIMIT),
        name="kvq_proj",
    )(x2, gsrc, gpre, wqT, wk, wvT, table)


def _rel_bias_gather(tab, g_s):
    t_hi = tab.astype(BF16)
    rem = tab - t_hi.astype(F32)
    t_mid = rem.astype(BF16)
    t_lo = (rem - t_mid.astype(F32)).astype(BF16)
    row = lax.broadcasted_iota(jnp.int32, (REL_TABLE, BIAS_ROLL), 0)
    lane = lax.broadcasted_iota(jnp.int32, (REL_TABLE, BIAS_ROLL), 1)
    dist = jnp.where(lane < QUAD, lane, lane - BIAS_ROLL) + LEFT_CHUNKS * CHUNK
    idx = jnp.clip(dist, -(CHUNK - 1), MAX_REL) + (CHUNK - 1)
    onehot = jnp.where(idx == row, 1.0, 0.0).astype(BF16)
    g_s[...] = (_dot(t_hi, onehot) + _dot(t_mid, onehot) + _dot(t_lo, onehot)) * math.log2(math.e)


def _rel_bias_rows(g_s, out_ref, cb, local):
    c0 = cb * SUBLANES
    q_chunk = lax.broadcasted_iota(jnp.int32, (1, QUAD), 1) // CHUNK + LEFT_CHUNKS
    back = q_chunk - cb // (CHUNK // SUBLANES)
    valid = (back >= 0) & (back <= LEFT_CHUNKS)
    for h in range(B_HEADS):
        rows = jnp.broadcast_to(g_s[h:h + 1, :], (SUBLANES, BIAS_ROLL))
        rolled = pltpu.roll(rows, c0, 1, stride=1, stride_axis=0)
        out_ref[h, local * SUBLANES:(local + 1) * SUBLANES, :] = jnp.where(
            valid, rolled[:, :QUAD], NEG)


def _chunk_attn_kernel(qT_ref, kp_ref, kc_ref, vTp_ref, vTc_ref, bias_ref, o_ref,
                       k4_s, vTq_s, qTm_s, s_s, m_s, oT_s):
    g = pl.program_id(1)
    slab = HEAD_GROUP * B_HEAD_DIM
    n_hg = B_HEADS // HEAD_GROUP
    n_phase = (GROUP // QUAD) * (B_HEADS // PHASE_HEADS)
    half = GROUP // 2
    ones = jnp.ones((SUM_ROWS, QUAD_BAND), BF16)

    def stage_operands(first_group):
        for hg in range(n_hg):
            lanes = slice(hg * slab, (hg + 1) * slab)
            k4_s[hg, :half, :] = kp_ref[:, lanes]
            k4_s[hg, half:, :] = kc_ref[:, lanes]
        vTq_s[0, :, :GROUP] = vTp_ref[...]
        vTq_s[0, :, GROUP:] = vTc_ref[:, :QUAD]
        vTq_s[1, :, :QUAD] = vTp_ref[:, QUAD:]
        vTq_s[1, :, QUAD:] = vTc_ref[...]
        if first_group:
            qTm_s[...] = jnp.zeros(qTm_s.shape, BF16)
        for h in range(B_HEADS):
            own = slice((h % HEAD_GROUP) * B_HEAD_DIM, (h % HEAD_GROUP + 1) * B_HEAD_DIM)
            for qd in range(GROUP // QUAD):
                qTm_s[h, qd, own, :] = qT_ref[h * B_HEAD_DIM:(h + 1) * B_HEAD_DIM,
                                              qd * QUAD:(qd + 1) * QUAD]

    def phase_ids(t):
        per_quad = B_HEADS // PHASE_HEADS
        slot, qd, part = (t & 1) * PHASE_HEADS, t >> (per_quad.bit_length() - 1), t & (per_quad - 1)
        return slot, qd, part * PHASE_HEADS, part * (PHASE_HEADS // HEAD_GROUP)

    def score_phase(t, first_group):
        slot, qd, head0, hg0 = phase_ids(t)
        start = qd * (QUAD // 2)
        if not isinstance(start, int):
            start = pl.multiple_of(start, QUAD // 2)
        for i in range(PHASE_HEADS):
            h = head0 + i
            kb = pltpu.bitcast(
                k4_s[hg0 + i // HEAD_GROUP, pl.ds(start, QUAD_BAND // 2), :], BF16)
            s = _dot(kb, qTm_s[h, qd]) + bias_ref[h]
            if first_group:
                key = lax.broadcasted_iota(jnp.int32, (QUAD_BAND, 1), 0)
                s = jnp.where(key >= GROUP - qd * QUAD, s, NEG)
            s_s[slot + i] = s
            m_s[slot + i] = jnp.max(s, axis=0, keepdims=True)

    def value_phase(t):
        slot, qd, head0, _ = phase_ids(t)
        for i in range(PHASE_HEADS):
            row0 = (head0 + i) * B_HEAD_DIM
            if not isinstance(row0, int):
                row0 = pl.multiple_of(row0, B_HEAD_DIM)
            p = jnp.exp2(s_s[slot + i] - m_s[slot + i]).astype(BF16)
            lhs = jnp.concatenate([vTq_s[qd, pl.ds(row0, B_HEAD_DIM), :], ones], axis=0)
            r = _dot(lhs, p)
            oT_s[qd, pl.ds(row0, B_HEAD_DIM), :] = (
                r[:B_HEAD_DIM] * (1.0 / r[B_HEAD_DIM:B_HEAD_DIM + 1]))

    def run(first_group):
        stage_operands(first_group)
        score_phase(0, first_group)

        def steady(t, carry):
            value_phase(t - 1)
            score_phase(t, first_group)
            return carry

        lax.fori_loop(1, n_phase, steady, 0)
        value_phase(n_phase - 1)

    pl.when(g == 0)(lambda: run(True))
    pl.when(g != 0)(lambda: run(False))
    for qd in range(GROUP // QUAD):
        o_ref[:, qd * QUAD:(qd + 1) * QUAD] = oT_s[qd].astype(BF16)


def _chunk_attn(qT, k, vT, bias, batch, seq):
    hd, T = qT.shape
    ng = seq // GROUP
    n_quad = GROUP // QUAD
    prev = lambda b, g: b * ng + jnp.maximum(g - 1, 0)
    cur = lambda b, g: b * ng + g
    return pl.pallas_call(
        _chunk_attn_kernel,
        grid=(batch, ng),
        in_specs=[pl.BlockSpec((hd, GROUP), lambda b, g: (0, cur(b, g))),
                  pl.BlockSpec((GROUP // 2, hd), lambda b, g: (prev(b, g), 0)),
                  pl.BlockSpec((GROUP // 2, hd), lambda b, g: (cur(b, g), 0)),
                  pl.BlockSpec((hd, GROUP), lambda b, g: (0, prev(b, g))),
                  pl.BlockSpec((hd, GROUP), lambda b, g: (0, cur(b, g))),
                  pl.BlockSpec(bias.shape, lambda b, g: (0, 0, 0),
                               pipeline_mode=pl.Buffered(1))],
        out_specs=pl.BlockSpec((hd, GROUP), lambda b, g: (0, cur(b, g))),
        out_shape=jax.ShapeDtypeStruct((hd, T), BF16),
        scratch_shapes=[pltpu.VMEM((B_HEADS // HEAD_GROUP, GROUP, HEAD_GROUP * B_HEAD_DIM),
                                   jnp.uint32),
                        pltpu.VMEM((n_quad, hd, QUAD_BAND), BF16),
                        pltpu.VMEM((B_HEADS, n_quad, HEAD_GROUP * B_HEAD_DIM, QUAD), BF16),
                        pltpu.VMEM((2 * PHASE_HEADS, QUAD_BAND, QUAD), F32),
                        pltpu.VMEM((2 * PHASE_HEADS, 1, QUAD), F32),
                        pltpu.VMEM((n_quad, hd, QUAD), F32)],
        compiler_params=pltpu.CompilerParams(
            dimension_semantics=("arbitrary", "arbitrary"), vmem_limit_bytes=VMEM_LIMIT),
        name="chunk_attn",
    )(qT, k, k, vT, vT, bias)


def kernel(x, positions, attn_pre_g, attn_post_g, ffn_pre_g, ffn_post_g, ffn_w_gate,
           ffn_w_up, ffn_w_down, mla_w_a, mla_g_q, mla_w_uq, mla_g_kv, mla_w_ukv, mla_w_o,
           kv_src_g, w_kv_shared, b_w_q, b_rel_table, b_w_o):
    batch, seq, d = x.shape
    T = batch * seq
    assert attn_pre_g.shape[0] == 2 and mla_w_a.shape[0] == 1 and b_w_q.shape[0] == 1
    assert seq % GROUP == 0 and seq % MLA_TQ == 0 and T % PROJ_ROWS == 0
    row = lambda g: g.reshape(1, -1).astype(F32)

    x2 = x.reshape(T, d)
    pos_row = positions.reshape(1, T).astype(F32)
    inv = 1.0 / (ROPE_THETA ** (jnp.arange(0, ROPE_DIM, 2, dtype=F32) / ROPE_DIM))
    inv_col = inv.reshape(ROPE_DIM // 2, 1)

    wa = mla_w_a[0][:, :Q_LORA + KV_LORA].astype(BF16)
    wkrT = jnp.pad(mla_w_a[0][:, Q_LORA + KV_LORA:].T, ((0, LANES - ROPE_DIM), (0, 0))).astype(BF16)
    wuq = mla_w_uq[0].reshape(Q_LORA, MLA_HEADS, NOPE_DIM + ROPE_DIM)
    wuq = jnp.pad(wuq, ((0, 0), (0, 0), (0, MLA_HEAD_PAD - NOPE_DIM - ROPE_DIM)))
    wuqT = wuq.reshape(Q_LORA, MLA_HEADS * MLA_HEAD_PAD).T.astype(BF16)
    wukv = mla_w_ukv[0].reshape(KV_LORA, MLA_HEADS, NOPE_DIM + V_DIM)
    wuk = wukv[:, :, :NOPE_DIM].reshape(KV_LORA, MLA_HEADS * NOPE_DIM).astype(BF16)
    wuvT = wukv[:, :, NOPE_DIM:].reshape(KV_LORA, MLA_HEADS * V_DIM).T.astype(BF16)
    hd = B_HEADS * B_HEAD_DIM
    wk = w_kv_shared[:, :hd].astype(BF16)
    wvT = w_kv_shared[:, hd:].T.astype(BF16)
    wqT = b_w_q[0].T.astype(BF16)

    qT0, kn0, kr0, vT0 = _mla_proj(x2, pos_row, inv_col, row(attn_pre_g[0]), wa, wkrT,
                                   row(mla_g_q[0]), wuqT, row(mla_g_kv[0]), wuk, wuvT)
    o0 = _mla_attn(qT0, kn0, kr0, vT0, batch, seq)
    wg, wu, wd = ffn_w_gate.astype(BF16), ffn_w_up.astype(BF16), ffn_w_down.astype(BF16)
    x2 = _block_tail(o0, x2, mla_w_o[0].astype(BF16), row(attn_post_g[0]), row(ffn_pre_g[0]),
                     wg, wu, wd, row(ffn_post_g[0]), 0)

    qT1, k1, vT1, bias = _kvq_proj(x2, row(kv_src_g), row(attn_pre_g[1]), wqT, wk, wvT,
                                   b_rel_table[0])
    o1 = _chunk_attn(qT1, k1, vT1, bias, batch, seq)
    x2 = _block_tail(o1, x2, b_w_o[0].astype(BF16), row(attn_post_g[1]), row(ffn_pre_g[1]),
                     wg, wu, wd, row(ffn_post_g[1]), 1)
    return x2.reshape(batch, seq, d)
```

```python
import functools
import math

import jax
import jax.numpy as jnp
from jax import lax
from jax.experimental import pallas as pl
from jax.experimental.pallas import tpu as pltpu

F32 = jnp.float32
BF16 = jnp.bfloat16

CHUNK = 64
MLA_HEADS = 8
Q_LORA = 384
KV_LORA = 256
NOPE_DIM = 128
ROPE_DIM = 64
V_DIM = 128
ROPE_THETA = 10000.0
B_HEADS = 16
B_HEAD_DIM = 64
LEFT_CHUNKS = 8
BAND = (LEFT_CHUNKS + 1) * CHUNK
MAX_REL = 256
REL_TABLE = MAX_REL + CHUNK
EPS = 1e-6

LANES = 128
SUBLANES = 8
MLA_HEAD_PAD = 2 * LANES
NEG = -1e30

PROJ_ROWS = 1024
TAIL_ROWS = 512
TAIL_SPLIT = 2
MLA_TQ = 512
MLA_TK = 256
MLA_TRIP_TILES = 4
SUM_ROWS = 16
QUAD = 4 * CHUNK
QUAD_BAND = BAND + 3 * CHUNK
GROUP = 2 * QUAD
HEAD_GROUP = 2
PHASE_HEADS = 8
BIAS_ROLL = 1024
V7X_VMEM_BYTES = 64 * 1024 * 1024
VMEM_LIMIT = V7X_VMEM_BYTES - 8 * 1024 * 1024


def _rms(x, g):
    ms = jnp.mean(x * x, axis=-1, keepdims=True)
    return x * lax.rsqrt(ms + EPS) * g


def _dot(a, b):
    return jnp.dot(a, b, preferred_element_type=F32)


def _dot_nt(a, b):
    return lax.dot_general(a, b, (((1,), (1,)), ((), ())), preferred_element_type=F32)


def _mla_proj_kernel(x_ref, posr_ref, invc_ref, gpre_ref, wa_ref, wkrT_ref, gq_ref,
                     wuqT_ref, gkv_ref, wuk_ref, wuvT_ref, qT_ref, kn_ref, kr_ref, vT_ref, *,
                     scale):
    h = _rms(x_ref[...], gpre_ref[...]).astype(BF16)
    a = _dot(h, wa_ref[...])
    cqn = _rms(a[:, :Q_LORA], gq_ref[...]).astype(BF16)
    ckvn = _rms(a[:, Q_LORA:], gkv_ref[...]).astype(BF16)
    half = ROPE_DIM // 2

    angT = invc_ref[...] * posr_ref[...]
    cosT, sinT = jnp.cos(angT), jnp.sin(angT)

    krT = _dot_nt(wkrT_ref[...], h)
    k1, k2 = krT[:half], krT[half:ROPE_DIM]
    kr_rotT = jnp.concatenate(
        [k1 * cosT - k2 * sinT, k2 * cosT + k1 * sinT,
         jnp.zeros((LANES - ROPE_DIM, krT.shape[1]), F32)], axis=0)
    kr_ref[...] = pltpu.bitcast(kr_rotT.T.astype(BF16), jnp.uint32)
    kn_ref[...] = pltpu.bitcast(_dot(ckvn, wuk_ref[...]).astype(BF16), jnp.uint32)

    cosT, sinT = cosT * scale, sinT * scale
    qT = _dot_nt(wuqT_ref[...], cqn)
    for hd in range(MLA_HEADS):
        lo = hd * MLA_HEAD_PAD
        t1 = qT[lo + NOPE_DIM:lo + NOPE_DIM + half]
        t2 = qT[lo + NOPE_DIM + half:lo + NOPE_DIM + ROPE_DIM]
        qT_ref[lo:lo + NOPE_DIM] = (qT[lo:lo + NOPE_DIM] * scale).astype(BF16)
        qT_ref[lo + NOPE_DIM:lo + NOPE_DIM + half] = (t1 * cosT - t2 * sinT).astype(BF16)
        qT_ref[lo + NOPE_DIM + half:lo + NOPE_DIM + ROPE_DIM] = (t2 * cosT + t1 * sinT).astype(BF16)
        qT_ref[lo + NOPE_DIM + ROPE_DIM:lo + MLA_HEAD_PAD] = jnp.zeros(
            (MLA_HEAD_PAD - NOPE_DIM - ROPE_DIM, qT.shape[1]), BF16)

    vT = _dot_nt(wuvT_ref[...], ckvn).astype(BF16)
    for t in range(vT_ref.shape[0]):
        vT_ref[t] = vT[:, t * MLA_TK:(t + 1) * MLA_TK]


def _mla_proj(x2, pos_row, inv_col, gpre, wa, wkrT, gq, wuqT, gkv, wuk, wuvT):
    T, D = x2.shape
    tm = PROJ_ROWS
    const = lambda i: (0, 0)
    full = lambda a: pl.BlockSpec(a.shape, const)
    scale = float((NOPE_DIM + ROPE_DIM) ** -0.5 * math.log2(math.e))
    return pl.pallas_call(
        functools.partial(_mla_proj_kernel, scale=scale),
        grid=(T // tm,),
        in_specs=[pl.BlockSpec((tm, D), lambda i: (i, 0)),
                  pl.BlockSpec((1, tm), lambda i: (0, i)),
                  full(inv_col), full(gpre), full(wa), full(wkrT), full(gq), full(wuqT),
                  full(gkv), full(wuk), full(wuvT)],
        out_specs=[pl.BlockSpec((MLA_HEADS * MLA_HEAD_PAD, tm), lambda i: (0, i)),
                   pl.BlockSpec((tm // 2, MLA_HEADS * NOPE_DIM), lambda i: (i, 0)),
                   pl.BlockSpec((tm // 2, LANES), lambda i: (i, 0)),
                   pl.BlockSpec((tm // MLA_TK, MLA_HEADS * V_DIM, MLA_TK), lambda i: (i, 0, 0))],
        out_shape=[jax.ShapeDtypeStruct((MLA_HEADS * MLA_HEAD_PAD, T), BF16),
                   jax.ShapeDtypeStruct((T // 2, MLA_HEADS * NOPE_DIM), jnp.uint32),
                   jax.ShapeDtypeStruct((T // 2, LANES), jnp.uint32),
                   jax.ShapeDtypeStruct((T // MLA_TK, MLA_HEADS * V_DIM, MLA_TK), BF16)],
        compiler_params=pltpu.CompilerParams(
            dimension_semantics=("arbitrary",), vmem_limit_bytes=VMEM_LIMIT),
        name="mla_proj",
    )(x2, pos_row, inv_col, gpre, wa, wkrT, gq, wuqT, gkv, wuk, wuvT)


def _mla_attn_kernel(qT_ref, kn_ref, kr_ref, vT_ref, oT_ref, m_s, l_s, a_s, acc_s, s_s):
    qi = pl.program_id(1)
    n_full = qi * (MLA_TQ // MLA_TK)
    m_s[...] = jnp.full(m_s.shape, NEG, F32)
    l_s[...] = jnp.zeros(l_s.shape, F32)
    a_s[...] = jnp.ones(a_s.shape, F32)
    acc_s[...] = jnp.zeros(acc_s.shape, F32)
    ones = jnp.ones((SUM_ROWS, MLA_TK), BF16)

    def score_pass(j, parity, diag):
        slot = parity * MLA_HEADS
        half = MLA_TK // 2
        rows = pl.ds(pl.multiple_of(j * half, half), half)
        q0 = 0 if diag is None else diag * MLA_TK
        k_rope = pltpu.bitcast(kr_ref[rows, :], BF16)
        for hd in range(MLA_HEADS):
            k_nope = pltpu.bitcast(kn_ref[rows, hd * NOPE_DIM:(hd + 1) * NOPE_DIM], BF16)
            k = jnp.concatenate([k_nope, k_rope], axis=1)
            qT = qT_ref[hd * MLA_HEAD_PAD:(hd + 1) * MLA_HEAD_PAD, q0:]
            sT = _dot(k, qT)
            if diag is not None:
                kc = lax.broadcasted_iota(jnp.int32, sT.shape, 0) // CHUNK
                qc = lax.broadcasted_iota(jnp.int32, sT.shape, 1) // CHUNK
                sT = jnp.where(kc <= qc, sT, NEG)
            s_s[slot + hd, :, q0:] = sT
            m = m_s[hd, :, q0:]
            m_new = jnp.maximum(m, jnp.max(sT, axis=0, keepdims=True))
            a_s[hd, :, q0:] = jnp.exp2(m - m_new)
            m_s[hd, :, q0:] = m_new
            if q0:
                s_s[slot + hd, :, :q0] = jnp.full((MLA_TK, q0), -jnp.inf, F32)
                a_s[hd, :, :q0] = jnp.ones((1, q0), F32)

    def value_pass(j, parity):
        slot = parity * MLA_HEADS
        jv = jnp.where(j < 0, n_full + 1, j)
        for hd in range(MLA_HEADS):
            p = jnp.exp2(s_s[slot + hd] - m_s[hd]).astype(BF16)
            lhs = jnp.concatenate([vT_ref[jv, hd * V_DIM:(hd + 1) * V_DIM, :], ones], axis=0)
            pv = _dot(lhs, p)
            a = a_s[hd]
            acc_s[hd] = a * acc_s[hd] + pv[:V_DIM]
            l_s[hd] = a * l_s[hd] + pv[V_DIM:V_DIM + 1]

    def full_tiles(base, count):
        for u in range(count):
            value_pass(base + u - 1, (u - 1) % 2)
            score_pass(base + u, u % 2, None)

    def steady(i, carry):
        full_tiles(i * MLA_TRIP_TILES, MLA_TRIP_TILES)
        return carry

    score_pass(n_full, 0, 0)
    value_pass(n_full, 0)
    score_pass(n_full + 1, 1, 1)
    n_trips = n_full // MLA_TRIP_TILES
    lax.fori_loop(0, n_trips, steady, 0)
    for left in range(2, MLA_TRIP_TILES, 2):
        @pl.when(n_full - n_trips * MLA_TRIP_TILES == left)
        def _(left=left):
            full_tiles(n_full - left, left)
    value_pass(n_full - 1, 1)
    for hd in range(MLA_HEADS):
        oT_ref[hd * V_DIM:(hd + 1) * V_DIM, :] = (acc_s[hd] * (1.0 / l_s[hd])).astype(BF16)


def _mla_attn(qT, kn, kr, vT3, batch, seq):
    assert MLA_TQ == 2 * MLA_TK and MLA_TK % CHUNK == 0 and MLA_TRIP_TILES % 2 == 0
    T = qT.shape[1]
    nq = seq // MLA_TQ
    nkt = seq // MLA_TK
    return pl.pallas_call(
        _mla_attn_kernel,
        grid=(batch, nq),
        in_specs=[pl.BlockSpec((MLA_HEADS * MLA_HEAD_PAD, MLA_TQ), lambda b, i: (0, b * nq + i)),
                  pl.BlockSpec((seq // 2, MLA_HEADS * NOPE_DIM), lambda b, i: (b, 0)),
                  pl.BlockSpec((seq // 2, LANES), lambda b, i: (b, 0)),
                  pl.BlockSpec((nkt, MLA_HEADS * V_DIM, MLA_TK), lambda b, i: (b, 0, 0),
                               pipeline_mode=pl.Buffered(1))],
        out_specs=pl.BlockSpec((MLA_HEADS * V_DIM, MLA_TQ), lambda b, i: (0, b * nq + i)),
        out_shape=jax.ShapeDtypeStruct((MLA_HEADS * V_DIM, T), BF16),
        scratch_shapes=[pltpu.VMEM((MLA_HEADS, 1, MLA_TQ), F32),
                        pltpu.VMEM((MLA_HEADS, 1, MLA_TQ), F32),
                        pltpu.VMEM((MLA_HEADS, 1, MLA_TQ), F32),
                        pltpu.VMEM((MLA_HEADS, V_DIM, MLA_TQ), F32),
                        pltpu.VMEM((2 * MLA_HEADS, MLA_TK, MLA_TQ), F32)],
        compiler_params=pltpu.CompilerParams(
            dimension_semantics=("arbitrary", "arbitrary"),
            vmem_limit_bytes=VMEM_LIMIT),
        name="mla_attn",
    )(qT, kn, kr, vT3)


def _block_tail_kernel(oT_ref, x_ref, wo_ref, gpost_ref, gfpre_ref, wg_ref, wu_ref,
                       wd_ref, gfpost_ref, out_ref):
    sub = out_ref.shape[0] // TAIL_SPLIT
    blocks = [slice(i * sub, (i + 1) * sub) for i in range(TAIL_SPLIT)]
    y = [lax.dot_general(oT_ref[:, r], wo_ref[...], (((0,), (0,)), ((), ())),
                         preferred_element_type=F32) for r in blocks]
    x1 = [x_ref[r, :] + _rms(yi, gpost_ref[...]) for r, yi in zip(blocks, y)]
    h = [_rms(xi, gfpre_ref[...]).astype(BF16) for xi in x1]
    gu = [(_dot(hi, wg_ref[...]), _dot(hi, wu_ref[...])) for hi in h]
    a = [(gi * (1.0 / (1.0 + jnp.exp(-gi))) * ui).astype(BF16) for gi, ui in gu]
    f = [_dot(ai, wd_ref[...]) for ai in a]
    for r, xi, fi in zip(blocks, x1, f):
        out_ref[r, :] = xi + _rms(fi, gfpost_ref[...])


def _block_tail(oT, x2, wo, gpost, gfpre, wg, wu, wd, gfpost, layer):
    T, D = x2.shape
    tm = TAIL_ROWS
    const = lambda i: (0, 0)
    full = lambda a: pl.BlockSpec(a.shape, const, pipeline_mode=pl.Buffered(1))
    of_layer = lambda a: pl.BlockSpec((None,) + a.shape[1:], lambda i: (layer, 0, 0),
                                      pipeline_mode=pl.Buffered(1))
    row = lambda a: pl.BlockSpec((tm, a.shape[1]), lambda i: (i, 0))
    return pl.pallas_call(
        _block_tail_kernel,
        grid=(T // tm,),
        in_specs=[pl.BlockSpec((oT.shape[0], tm), lambda i: (0, i)), row(x2), full(wo),
                  full(gpost), full(gfpre), of_layer(wg), of_layer(wu), of_layer(wd),
                  full(gfpost)],
        out_specs=pl.BlockSpec((tm, D), lambda i: (i, 0)),
        out_shape=jax.ShapeDtypeStruct((T, D), F32),
        compiler_params=pltpu.CompilerParams(
            dimension_semantics=("arbitrary",), vmem_limit_bytes=VMEM_LIMIT),
        name="block_tail",
    )(oT, x2, wo, gpost, gfpre, wg, wu, wd, gfpost)


def _kvq_proj_kernel(x_ref, gsrc_ref, gpre_ref, wqT_ref, wk_ref, wvT_ref, tab_ref,
                     qT_ref, k_ref, vT_ref, bias_ref, g_s, *, scale):
    i = pl.program_id(0)

    @pl.when(i == 0)
    def _():
        _rel_bias_gather(tab_ref[...], g_s)

    blocks = bias_ref.shape[1] // SUBLANES
    for local in range(blocks):
        _rel_bias_rows(g_s, bias_ref, i * blocks + local, local)
    x = x_ref[...]
    xr = x * lax.rsqrt(jnp.mean(x * x, axis=-1, keepdims=True) + EPS)
    xs = (xr * gsrc_ref[...]).astype(BF16)
    xq = (xr * gpre_ref[...]).astype(BF16)
    qT_ref[...] = (_dot_nt(wqT_ref[...], xq) * scale).astype(BF16)
    k_ref[...] = pltpu.bitcast(_dot(xs, wk_ref[...]).astype(BF16), jnp.uint32)
    vT_ref[...] = _dot_nt(wvT_ref[...], xs).astype(BF16)


def _kvq_proj(x2, gsrc, gpre, wqT, wk, wvT, table):
    T, D = x2.shape
    tm = PROJ_ROWS
    steps = T // tm
    hd = B_HEADS * B_HEAD_DIM
    assert QUAD_BAND % (steps * SUBLANES) == 0
    const = lambda i: (0, 0)
    full = lambda a: pl.BlockSpec(a.shape, const)
    scale = float(B_HEAD_DIM ** -0.5 * math.log2(math.e))
    return pl.pallas_call(
        functools.partial(_kvq_proj_kernel, scale=scale),
        grid=(steps,),
        in_specs=[pl.BlockSpec((tm, D), lambda i: (i, 0)),
                  full(gsrc), full(gpre), full(wqT), full(wk), full(wvT), full(table)],
        out_specs=[pl.BlockSpec((hd, tm), lambda i: (0, i)),
                   pl.BlockSpec((tm // 2, hd), lambda i: (i, 0)),
                   pl.BlockSpec((hd, tm), lambda i: (0, i)),
                   pl.BlockSpec((B_HEADS, QUAD_BAND // steps, QUAD), lambda i: (0, i, 0))],
        out_shape=[jax.ShapeDtypeStruct((hd, T), BF16),
                   jax.ShapeDtypeStruct((T // 2, hd), jnp.uint32),
                   jax.ShapeDtypeStruct((hd, T), BF16),
                   jax.ShapeDtypeStruct((B_HEADS, QUAD_BAND, QUAD), F32)],
        scratch_shapes=[pltpu.VMEM((B_HEADS, BIAS_ROLL), F32)],
        compiler_params=pltpu.CompilerParams(
            dimension_semantics=("arbitrary",), vmem_limit_bytes=VMEM_LIMIT),
        name="kvq_proj",
    )(x2, gsrc, gpre, wqT, wk, wvT, table)


def _rel_bias_gather(tab, g_s):
    t_hi = tab.astype(BF16)
    rem = tab - t_hi.astype(F32)
    t_mid = rem.astype(BF16)
    t_lo = (rem - t_mid.astype(F32)).astype(BF16)
    row = lax.broadcasted_iota(jnp.int32, (REL_TABLE, BIAS_ROLL), 0)
    lane = lax.broadcasted_iota(jnp.int32, (REL_TABLE, BIAS_ROLL), 1)
    dist = jnp.where(lane < QUAD, lane, lane - BIAS_ROLL) + LEFT_CHUNKS * CHUNK
    idx = jnp.clip(dist, -(CHUNK - 1), MAX_REL) + (CHUNK - 1)
    onehot = jnp.where(idx == row, 1.0, 0.0).astype(BF16)
    g_s[...] = (_dot(t_hi, onehot) + _dot(t_mid, onehot) + _dot(t_lo, onehot)) * math.log2(math.e)


def _rel_bias_rows(g_s, out_ref, cb, local):
    c0 = cb * SUBLANES
    q_chunk = lax.broadcasted_iota(jnp.int32, (1, QUAD), 1) // CHUNK + LEFT_CHUNKS
    back = q_chunk - cb // (CHUNK // SUBLANES)
    valid = (back >= 0) & (back <= LEFT_CHUNKS)
    for h in range(B_HEADS):
        rows = jnp.broadcast_to(g_s[h:h + 1, :], (SUBLANES, BIAS_ROLL))
        rolled = pltpu.roll(rows, c0, 1, stride=1, stride_axis=0)
        out_ref[h, local * SUBLANES:(local + 1) * SUBLANES, :] = jnp.where(
            valid, rolled[:, :QUAD], NEG)


def _chunk_attn_kernel(qT_ref, kp_ref, kc_ref, vTp_ref, vTc_ref, bias_ref, o_ref,
                       k4_s, vTq_s, qTm_s, s_s, m_s, oT_s):
    g = pl.program_id(1)
    slab = HEAD_GROUP * B_HEAD_DIM
    n_hg = B_HEADS // HEAD_GROUP
    n_phase = (GROUP // QUAD) * (B_HEADS // PHASE_HEADS)
    half = GROUP // 2
    ones = jnp.ones((SUM_ROWS, QUAD_BAND), BF16)

    def stage_operands(first_group):
        for hg in range(n_hg):
            lanes = slice(hg * slab, (hg + 1) * slab)
            k4_s[hg, :half, :] = kp_ref[:, lanes]
            k4_s[hg, half:, :] = kc_ref[:, lanes]
        vTq_s[0, :, :GROUP] = vTp_ref[...]
        vTq_s[0, :, GROUP:] = vTc_ref[:, :QUAD]
        vTq_s[1, :, :QUAD] = vTp_ref[:, QUAD:]
        vTq_s[1, :, QUAD:] = vTc_ref[...]
        if first_group:
            qTm_s[...] = jnp.zeros(qTm_s.shape, BF16)
        for h in range(B_HEADS):
            own = slice((h % HEAD_GROUP) * B_HEAD_DIM, (h % HEAD_GROUP + 1) * B_HEAD_DIM)
            for qd in range(GROUP // QUAD):
                qTm_s[h, qd, own, :] = qT_ref[h * B_HEAD_DIM:(h + 1) * B_HEAD_DIM,
                                              qd * QUAD:(qd + 1) * QUAD]

    def phase_ids(t):
        per_quad = B_HEADS // PHASE_HEADS
        slot, qd, part = (t & 1) * PHASE_HEADS, t >> (per_quad.bit_length() - 1), t & (per_quad - 1)
        return slot, qd, part * PHASE_HEADS, part * (PHASE_HEADS // HEAD_GROUP)

    def score_phase(t, first_group):
        slot, qd, head0, hg0 = phase_ids(t)
        start = qd * (QUAD // 2)
        if not isinstance(start, int):
            start = pl.multiple_of(start, QUAD // 2)
        for i in range(PHASE_HEADS):
            h = head0 + i
            kb = pltpu.bitcast(
                k4_s[hg0 + i // HEAD_GROUP, pl.ds(start, QUAD_BAND // 2), :], BF16)
            s = _dot(kb, qTm_s[h, qd]) + bias_ref[h]
            if first_group:
                key = lax.broadcasted_iota(jnp.int32, (QUAD_BAND, 1), 0)
                s = jnp.where(key >= GROUP - qd * QUAD, s, NEG)
            s_s[slot + i] = s
            m_s[slot + i] = jnp.max(s, axis=0, keepdims=True)

    def value_phase(t):
        slot, qd, head0, _ = phase_ids(t)
        for i in range(PHASE_HEADS):
            row0 = (head0 + i) * B_HEAD_DIM
            if not isinstance(row0, int):
                row0 = pl.multiple_of(row0, B_HEAD_DIM)
            p = jnp.exp2(s_s[slot + i] - m_s[slot + i]).astype(BF16)
            lhs = jnp.concatenate([vTq_s[qd, pl.ds(row0, B_HEAD_DIM), :], ones], axis=0)
            r = _dot(lhs, p)
            oT_s[qd, pl.ds(row0, B_HEAD_DIM), :] = (
                r[:B_HEAD_DIM] * (1.0 / r[B_HEAD_DIM:B_HEAD_DIM + 1]))

    def run(first_group):
        stage_operands(first_group)
        score_phase(0, first_group)

        def steady(t, carry):
            value_phase(t - 1)
            score_phase(t, first_group)
            return carry

        lax.fori_loop(1, n_phase, steady, 0)
        value_phase(n_phase - 1)

    pl.when(g == 0)(lambda: run(True))
    pl.when(g != 0)(lambda: run(False))
    for qd in range(GROUP // QUAD):
        o_ref[:, qd * QUAD:(qd + 1) * QUAD] = oT_s[qd].astype(BF16)


def _chunk_attn(qT, k, vT, bias, batch, seq):
    hd, T = qT.shape
    ng = seq // GROUP
    n_quad = GROUP // QUAD
    prev = lambda b, g: b * ng + jnp.maximum(g - 1, 0)
    cur = lambda b, g: b * ng + g
    return pl.pallas_call(
        _chunk_attn_kernel,
        grid=(batch, ng),
        in_specs=[pl.BlockSpec((hd, GROUP), lambda b, g: (0, cur(b, g))),
                  pl.BlockSpec((GROUP // 2, hd), lambda b, g: (prev(b, g), 0)),
                  pl.BlockSpec((GROUP // 2, hd), lambda b, g: (cur(b, g), 0)),
                  pl.BlockSpec((hd, GROUP), lambda b, g: (0, prev(b, g))),
                  pl.BlockSpec((hd, GROUP), lambda b, g: (0, cur(b, g))),
                  pl.BlockSpec(bias.shape, lambda b, g: (0, 0, 0),
                               pipeline_mode=pl.Buffered(1))],
        out_specs=pl.BlockSpec((hd, GROUP), lambda b, g: (0, cur(b, g))),
        out_shape=jax.ShapeDtypeStruct((hd, T), BF16),
        scratch_shapes=[pltpu.VMEM((B_HEADS // HEAD_GROUP, GROUP, HEAD_GROUP * B_HEAD_DIM),
                                   jnp.uint32),
                        pltpu.VMEM((n_quad, hd, QUAD_BAND), BF16),
                        pltpu.VMEM((B_HEADS, n_quad, HEAD_GROUP * B_HEAD_DIM, QUAD), BF16),
                        pltpu.VMEM((2 * PHASE_HEADS, QUAD_BAND, QUAD), F32),
                        pltpu.VMEM((2 * PHASE_HEADS, 1, QUAD), F32),
                        pltpu.VMEM((n_quad, hd, QUAD), F32)],
        compiler_params=pltpu.CompilerParams(
            dimension_semantics=("arbitrary", "arbitrary"), vmem_limit_bytes=VMEM_LIMIT),
        name="chunk_attn",
    )(qT, k, k, vT, vT, bias)


def kernel(x, positions, attn_pre_g, attn_post_g, ffn_pre_g, ffn_post_g, ffn_w_gate,
           ffn_w_up, ffn_w_down, mla_w_a, mla_g_q, mla_w_uq, mla_g_kv, mla_w_ukv, mla_w_o,
           kv_src_g, w_kv_shared, b_w_q, b_rel_table, b_w_o):
    batch, seq, d = x.shape
    T = batch * seq
    assert attn_pre_g.shape[0] == 2 and mla_w_a.shape[0] == 1 and b_w_q.shape[0] == 1
    assert seq % GROUP == 0 and seq % MLA_TQ == 0 and T % PROJ_ROWS == 0
    row = lambda g: g.reshape(1, -1).astype(F32)

    x2 = x.reshape(T, d)
    pos_row = positions.reshape(1, T).astype(F32)
    inv = 1.0 / (ROPE_THETA ** (jnp.arange(0, ROPE_DIM, 2, dtype=F32) / ROPE_DIM))
    inv_col = inv.reshape(ROPE_DIM // 2, 1)

    wa = mla_w_a[0][:, :Q_LORA + KV_LORA].astype(BF16)
    wkrT = jnp.pad(mla_w_a[0][:, Q_LORA + KV_LORA:].T, ((0, LANES - ROPE_DIM), (0, 0))).astype(BF16)
    wuq = mla_w_uq[0].reshape(Q_LORA, MLA_HEADS, NOPE_DIM + ROPE_DIM)
    wuq = jnp.pad(wuq, ((0, 0), (0, 0), (0, MLA_HEAD_PAD - NOPE_DIM - ROPE_DIM)))
    wuqT = wuq.reshape(Q_LORA, MLA_HEADS * MLA_HEAD_PAD).T.astype(BF16)
    wukv = mla_w_ukv[0].reshape(KV_LORA, MLA_HEADS, NOPE_DIM + V_DIM)
    wuk = wukv[:, :, :NOPE_DIM].reshape(KV_LORA, MLA_HEADS * NOPE_DIM).astype(BF16)
    wuvT = wukv[:, :, NOPE_DIM:].reshape(KV_LORA, MLA_HEADS * V_DIM).T.astype(BF16)
    hd = B_HEADS * B_HEAD_DIM
    wk = w_kv_shared[:, :hd].astype(BF16)
    wvT = w_kv_shared[:, hd:].T.astype(BF16)
    wqT = b_w_q[0].T.astype(BF16)

    qT0, kn0, kr0, vT0 = _mla_proj(x2, pos_row, inv_col, row(attn_pre_g[0]), wa, wkrT,
                                   row(mla_g_q[0]), wuqT, row(mla_g_kv[0]), wuk, wuvT)
    o0 = _mla_attn(qT0, kn0, kr0, vT0, batch, seq)
    wg, wu, wd = ffn_w_gate.astype(BF16), ffn_w_up.astype(BF16), ffn_w_down.astype(BF16)
    x2 = _block_tail(o0, x2, mla_w_o[0].astype(BF16), row(attn_post_g[0]), row(ffn_pre_g[0]),
                     wg, wu, wd, row(ffn_post_g[0]), 0)

    qT1, k1, vT1, bias = _kvq_proj(x2, row(kv_src_g), row(attn_pre_g[1]), wqT, wk, wvT,
                                   b_rel_table[0])
    o1 = _chunk_attn(qT1, k1, vT1, bias, batch, seq)
    x2 = _block_tail(o1, x2, b_w_o[0].astype(BF16), row(attn_post_g[1]), row(ffn_pre_g[1]),
                     wg, wu, wd, row(ffn_post_g[1]), 1)
    return x2.reshape(batch, seq, d)
```

```python
import functools
import math

import jax
import jax.numpy as jnp
from jax import lax
from jax.experimental import pallas as pl
from jax.experimental.pallas import tpu as pltpu

F32 = jnp.float32
BF16 = jnp.bfloat16

CHUNK = 64
MLA_HEADS = 8
Q_LORA = 384
KV_LORA = 256
NOPE_DIM = 128
ROPE_DIM = 64
V_DIM = 128
ROPE_THETA = 10000.0
B_HEADS = 16
B_HEAD_DIM = 64
LEFT_CHUNKS = 8
BAND = (LEFT_CHUNKS + 1) * CHUNK
MAX_REL = 256
REL_TABLE = MAX_REL + CHUNK
EPS = 1e-6

LANES = 128
SUBLANES = 8
MLA_HEAD_PAD = 2 * LANES
NEG = -1e30

PROJ_ROWS = 1024
TAIL_ROWS = 512
TAIL_SPLIT = 2
MLA_TQ = 512
MLA_TK = 256
MLA_TRIP_TILES = 4
SUM_ROWS = 16
QUAD = 4 * CHUNK
QUAD_BAND = BAND + 3 * CHUNK
GROUP = 2 * QUAD
HEAD_GROUP = 2
PHASE_HEADS = 8
BIAS_ROLL = 1024
V7X_VMEM_BYTES = 64 * 1024 * 1024
VMEM_LIMIT = V7X_VMEM_BYTES - 8 * 1024 * 1024


def _rms(x, g):
    ms = jnp.mean(x * x, axis=-1, keepdims=True)
    return x * lax.rsqrt(ms + EPS) * g


def _dot(a, b):
    return jnp.dot(a, b, preferred_element_type=F32)


def _dot_nt(a, b):
    return lax.dot_general(a, b, (((1,), (1,)), ((), ())), preferred_element_type=F32)


def _mla_proj_kernel(x_ref, posr_ref, invc_ref, gpre_ref, wa_ref, wkrT_ref, gq_ref,
                     wuqT_ref, gkv_ref, wuk_ref, wuvT_ref, qT_ref, kn_ref, kr_ref, vT_ref, *,
                     scale):
    h = _rms(x_ref[...], gpre_ref[...]).astype(BF16)
    a = _dot(h, wa_ref[...])
    cqn = _rms(a[:, :Q_LORA], gq_ref[...]).astype(BF16)
    ckvn = _rms(a[:, Q_LORA:], gkv_ref[...]).astype(BF16)
    half = ROPE_DIM // 2

    angT = invc_ref[...] * posr_ref[...]
    cosT, sinT = jnp.cos(angT), jnp.sin(angT)

    krT = _dot_nt(wkrT_ref[...], h)
    k1, k2 = krT[:half], krT[half:ROPE_DIM]
    kr_rotT = jnp.concatenate(
        [k1 * cosT - k2 * sinT, k2 * cosT + k1 * sinT,
         jnp.zeros((LANES - ROPE_DIM, krT.shape[1]), F32)], axis=0)
    kr_ref[...] = pltpu.bitcast(kr_rotT.T.astype(BF16), jnp.uint32)
    kn_ref[...] = pltpu.bitcast(_dot(ckvn, wuk_ref[...]).astype(BF16), jnp.uint32)

    cosT, sinT = cosT * scale, sinT * scale
    qT = _dot_nt(wuqT_ref[...], cqn)
    for hd in range(MLA_HEADS):
        lo = hd * MLA_HEAD_PAD
        t1 = qT[lo + NOPE_DIM:lo + NOPE_DIM + half]
        t2 = qT[lo + NOPE_DIM + half:lo + NOPE_DIM + ROPE_DIM]
        qT_ref[lo:lo + NOPE_DIM] = (qT[lo:lo + NOPE_DIM] * scale).astype(BF16)
        qT_ref[lo + NOPE_DIM:lo + NOPE_DIM + half] = (t1 * cosT - t2 * sinT).astype(BF16)
        qT_ref[lo + NOPE_DIM + half:lo + NOPE_DIM + ROPE_DIM] = (t2 * cosT + t1 * sinT).astype(BF16)
        qT_ref[lo + NOPE_DIM + ROPE_DIM:lo + MLA_HEAD_PAD] = jnp.zeros(
            (MLA_HEAD_PAD - NOPE_DIM - ROPE_DIM, qT.shape[1]), BF16)

    vT = _dot_nt(wuvT_ref[...], ckvn).astype(BF16)
    for t in range(vT_ref.shape[0]):
        vT_ref[t] = vT[:, t * MLA_TK:(t + 1) * MLA_TK]


def _mla_proj(x2, pos_row, inv_col, gpre, wa, wkrT, gq, wuqT, gkv, wuk, wuvT):
    T, D = x2.shape
    tm = PROJ_ROWS
    const = lambda i: (0, 0)
    full = lambda a: pl.BlockSpec(a.shape, const)
    scale = float((NOPE_DIM + ROPE_DIM) ** -0.5 * math.log2(math.e))
    return pl.pallas_call(
        functools.partial(_mla_proj_kernel, scale=scale),
        grid=(T // tm,),
        in_specs=[pl.BlockSpec((tm, D), lambda i: (i, 0)),
                  pl.BlockSpec((1, tm), lambda i: (0, i)),
                  full(inv_col), full(gpre), full(wa), full(wkrT), full(gq), full(wuqT),
                  full(gkv), full(wuk), full(wuvT)],
        out_specs=[pl.BlockSpec((MLA_HEADS * MLA_HEAD_PAD, tm), lambda i: (0, i)),
                   pl.BlockSpec((tm // 2, MLA_HEADS * NOPE_DIM), lambda i: (i, 0)),
                   pl.BlockSpec((tm // 2, LANES), lambda i: (i, 0)),
                   pl.BlockSpec((tm // MLA_TK, MLA_HEADS * V_DIM, MLA_TK), lambda i: (i, 0, 0))],
        out_shape=[jax.ShapeDtypeStruct((MLA_HEADS * MLA_HEAD_PAD, T), BF16),
                   jax.ShapeDtypeStruct((T // 2, MLA_HEADS * NOPE_DIM), jnp.uint32),
                   jax.ShapeDtypeStruct((T // 2, LANES), jnp.uint32),
                   jax.ShapeDtypeStruct((T // MLA_TK, MLA_HEADS * V_DIM, MLA_TK), BF16)],
        compiler_params=pltpu.CompilerParams(
            dimension_semantics=("arbitrary",), vmem_limit_bytes=VMEM_LIMIT),
        name="mla_proj",
    )(x2, pos_row, inv_col, gpre, wa, wkrT, gq, wuqT, gkv, wuk, wuvT)


def _mla_attn_kernel(qT_ref, kn_ref, kr_ref, vT_ref, oT_ref, m_s, l_s, a_s, acc_s, s_s):
    qi = pl.program_id(1)
    n_full = qi * (MLA_TQ // MLA_TK)
    m_s[...] = jnp.full(m_s.shape, NEG, F32)
    l_s[...] = jnp.zeros(l_s.shape, F32)
    a_s[...] = jnp.ones(a_s.shape, F32)
    acc_s[...] = jnp.zeros(acc_s.shape, F32)
    ones = jnp.ones((SUM_ROWS, MLA_TK), BF16)

    def score_pass(j, parity, diag):
        slot = parity * MLA_HEADS
        half = MLA_TK // 2
        rows = pl.ds(pl.multiple_of(j * half, half), half)
        q0 = 0 if diag is None else diag * MLA_TK
        k_rope = pltpu.bitcast(kr_ref[rows, :], BF16)
        for hd in range(MLA_HEADS):
            k_nope = pltpu.bitcast(kn_ref[rows, hd * NOPE_DIM:(hd + 1) * NOPE_DIM], BF16)
            k = jnp.concatenate([k_nope, k_rope], axis=1)
            qT = qT_ref[hd * MLA_HEAD_PAD:(hd + 1) * MLA_HEAD_PAD, q0:]
            sT = _dot(k, qT)
            if diag is not None:
                kc = lax.broadcasted_iota(jnp.int32, sT.shape, 0) // CHUNK
                qc = lax.broadcasted_iota(jnp.int32, sT.shape, 1) // CHUNK
                sT = jnp.where(kc <= qc, sT, NEG)
            s_s[slot + hd, :, q0:] = sT
            m = m_s[hd, :, q0:]
            m_new = jnp.maximum(m, jnp.max(sT, axis=0, keepdims=True))
            a_s[hd, :, q0:] = jnp.exp2(m - m_new)
            m_s[hd, :, q0:] = m_new
            if q0:
                s_s[slot + hd, :, :q0] = jnp.full((MLA_TK, q0), -jnp.inf, F32)
                a_s[hd, :, :q0] = jnp.ones((1, q0), F32)

    def value_pass(j, parity):
        slot = parity * MLA_HEADS
        jv = jnp.where(j < 0, n_full + 1, j)
        for hd in range(MLA_HEADS):
            p = jnp.exp2(s_s[slot + hd] - m_s[hd]).astype(BF16)
            lhs = jnp.concatenate([vT_ref[jv, hd * V_DIM:(hd + 1) * V_DIM, :], ones], axis=0)
            pv = _dot(lhs, p)
            a = a_s[hd]
            acc_s[hd] = a * acc_s[hd] + pv[:V_DIM]
            l_s[hd] = a * l_s[hd] + pv[V_DIM:V_DIM + 1]

    def full_tiles(base, count):
        for u in range(count):
            value_pass(base + u - 1, (u - 1) % 2)
            score_pass(base + u, u % 2, None)

    def steady(i, carry):
        full_tiles(i * MLA_TRIP_TILES, MLA_TRIP_TILES)
        return carry

    score_pass(n_full, 0, 0)
    value_pass(n_full, 0)
    score_pass(n_full + 1, 1, 1)
    n_trips = n_full // MLA_TRIP_TILES
    lax.fori_loop(0, n_trips, steady, 0)
    for left in range(2, MLA_TRIP_TILES, 2):
        @pl.when(n_full - n_trips * MLA_TRIP_TILES == left)
        def _(left=left):
            full_tiles(n_full - left, left)
    value_pass(n_full - 1, 1)
    for hd in range(MLA_HEADS):
        oT_ref[hd * V_DIM:(hd + 1) * V_DIM, :] = (acc_s[hd] * (1.0 / l_s[hd])).astype(BF16)


def _mla_attn(qT, kn, kr, vT3, batch, seq):
    assert MLA_TQ == 2 * MLA_TK and MLA_TK % CHUNK == 0 and MLA_TRIP_TILES % 2 == 0
    T = qT.shape[1]
    nq = seq // MLA_TQ
    nkt = seq // MLA_TK
    return pl.pallas_call(
        _mla_attn_kernel,
        grid=(batch, nq),
        in_specs=[pl.BlockSpec((MLA_HEADS * MLA_HEAD_PAD, MLA_TQ), lambda b, i: (0, b * nq + i)),
                  pl.BlockSpec((seq // 2, MLA_HEADS * NOPE_DIM), lambda b, i: (b, 0)),
                  pl.BlockSpec((seq // 2, LANES), lambda b, i: (b, 0)),
                  pl.BlockSpec((nkt, MLA_HEADS * V_DIM, MLA_TK), lambda b, i: (b, 0, 0))],
        out_specs=pl.BlockSpec((MLA_HEADS * V_DIM, MLA_TQ), lambda b, i: (0, b * nq + i)),
        out_shape=jax.ShapeDtypeStruct((MLA_HEADS * V_DIM, T), BF16),
        scratch_shapes=[pltpu.VMEM((MLA_HEADS, 1, MLA_TQ), F32),
                        pltpu.VMEM((MLA_HEADS, 1, MLA_TQ), F32),
                        pltpu.VMEM((MLA_HEADS, 1, MLA_TQ), F32),
                        pltpu.VMEM((MLA_HEADS, V_DIM, MLA_TQ), F32),
                        pltpu.VMEM((2 * MLA_HEADS, MLA_TK, MLA_TQ), F32)],
        compiler_params=pltpu.CompilerParams(
            dimension_semantics=("arbitrary", "arbitrary"),
            vmem_limit_bytes=VMEM_LIMIT),
        name="mla_attn",
    )(qT, kn, kr, vT3)


def _block_tail_kernel(oT_ref, x_ref, wo_ref, gpost_ref, gfpre_ref, wg_ref, wu_ref,
                       wd_ref, gfpost_ref, out_ref):
    sub = out_ref.shape[0] // TAIL_SPLIT
    blocks = [slice(i * sub, (i + 1) * sub) for i in range(TAIL_SPLIT)]
    y = [lax.dot_general(oT_ref[:, r], wo_ref[...], (((0,), (0,)), ((), ())),
                         preferred_element_type=F32) for r in blocks]
    x1 = [x_ref[r, :] + _rms(yi, gpost_ref[...]) for r, yi in zip(blocks, y)]
    h = [_rms(xi, gfpre_ref[...]).astype(BF16) for xi in x1]
    gu = [(_dot(hi, wg_ref[...]), _dot(hi, wu_ref[...])) for hi in h]
    a = [(gi * (1.0 / (1.0 + jnp.exp(-gi))) * ui).astype(BF16) for gi, ui in gu]
    f = [_dot(ai, wd_ref[...]) for ai in a]
    for r, xi, fi in zip(blocks, x1, f):
        out_ref[r, :] = xi + _rms(fi, gfpost_ref[...])


def _block_tail(oT, x2, wo, gpost, gfpre, wg, wu, wd, gfpost, layer):
    T, D = x2.shape
    tm = TAIL_ROWS
    const = lambda i: (0, 0)
    full = lambda a: pl.BlockSpec(a.shape, const, pipeline_mode=pl.Buffered(1))
    of_layer = lambda a: pl.BlockSpec((None,) + a.shape[1:], lambda i: (layer, 0, 0),
                                      pipeline_mode=pl.Buffered(1))
    row = lambda a: pl.BlockSpec((tm, a.shape[1]), lambda i: (i, 0))
    return pl.pallas_call(
        _block_tail_kernel,
        grid=(T // tm,),
        in_specs=[pl.BlockSpec((oT.shape[0], tm), lambda i: (0, i)), row(x2), full(wo),
                  full(gpost), full(gfpre), of_layer(wg), of_layer(wu), of_layer(wd),
                  full(gfpost)],
        out_specs=pl.BlockSpec((tm, D), lambda i: (i, 0)),
        out_shape=jax.ShapeDtypeStruct((T, D), F32),
        compiler_params=pltpu.CompilerParams(
            dimension_semantics=("arbitrary",), vmem_limit_bytes=VMEM_LIMIT),
        name="block_tail",
    )(oT, x2, wo, gpost, gfpre, wg, wu, wd, gfpost)


def _kvq_proj_kernel(x_ref, gsrc_ref, gpre_ref, wqT_ref, wk_ref, wvT_ref, tab_ref,
                     qT_ref, k_ref, vT_ref, bias_ref, g_s, *, scale):
    i = pl.program_id(0)

    @pl.when(i == 0)
    def _():
        _rel_bias_gather(tab_ref[...], g_s)

    blocks = bias_ref.shape[1] // SUBLANES
    for local in range(blocks):
        _rel_bias_rows(g_s, bias_ref, i * blocks + local, local)
    x = x_ref[...]
    xr = x * lax.rsqrt(jnp.mean(x * x, axis=-1, keepdims=True) + EPS)
    xs = (xr * gsrc_ref[...]).astype(BF16)
    xq = (xr * gpre_ref[...]).astype(BF16)
    qT_ref[...] = (_dot_nt(wqT_ref[...], xq) * scale).astype(BF16)
    k_ref[...] = pltpu.bitcast(_dot(xs, wk_ref[...]).astype(BF16), jnp.uint32)
    vT_ref[...] = _dot_nt(wvT_ref[...], xs).astype(BF16)


def _kvq_proj(x2, gsrc, gpre, wqT, wk, wvT, table):
    T, D = x2.shape
    tm = PROJ_ROWS
    steps = T // tm
    hd = B_HEADS * B_HEAD_DIM
    assert QUAD_BAND % (steps * SUBLANES) == 0
    const = lambda i: (0, 0)
    full = lambda a: pl.BlockSpec(a.shape, const)
    scale = float(B_HEAD_DIM ** -0.5 * math.log2(math.e))
    return pl.pallas_call(
        functools.partial(_kvq_proj_kernel, scale=scale),
        grid=(steps,),
        in_specs=[pl.BlockSpec((tm, D), lambda i: (i, 0)),
                  full(gsrc), full(gpre), full(wqT), full(wk), full(wvT), full(table)],
        out_specs=[pl.BlockSpec((hd, tm), lambda i: (0, i)),
                   pl.BlockSpec((tm // 2, hd), lambda i: (i, 0)),
                   pl.BlockSpec((hd, tm), lambda i: (0, i)),
                   pl.BlockSpec((B_HEADS, QUAD_BAND // steps, QUAD), lambda i: (0, i, 0))],
        out_shape=[jax.ShapeDtypeStruct((hd, T), BF16),
                   jax.ShapeDtypeStruct((T // 2, hd), jnp.uint32),
                   jax.ShapeDtypeStruct((hd, T), BF16),
                   jax.ShapeDtypeStruct((B_HEADS, QUAD_BAND, QUAD), F32)],
        scratch_shapes=[pltpu.VMEM((B_HEADS, BIAS_ROLL), F32)],
        compiler_params=pltpu.CompilerParams(
            dimension_semantics=("arbitrary",), vmem_limit_bytes=VMEM_LIMIT),
        name="kvq_proj",
    )(x2, gsrc, gpre, wqT, wk, wvT, table)


def _rel_bias_gather(tab, g_s):
    t_hi = tab.astype(BF16)
    rem = tab - t_hi.astype(F32)
    t_mid = rem.astype(BF16)
    t_lo = (rem - t_mid.astype(F32)).astype(BF16)
    row = lax.broadcasted_iota(jnp.int32, (REL_TABLE, BIAS_ROLL), 0)
    lane = lax.broadcasted_iota(jnp.int32, (REL_TABLE, BIAS_ROLL), 1)
    dist = jnp.where(lane < QUAD, lane, lane - BIAS_ROLL) + LEFT_CHUNKS * CHUNK
    idx = jnp.clip(dist, -(CHUNK - 1), MAX_REL) + (CHUNK - 1)
    onehot = jnp.where(idx == row, 1.0, 0.0).astype(BF16)
    g_s[...] = (_dot(t_hi, onehot) + _dot(t_mid, onehot) + _dot(t_lo, onehot)) * math.log2(math.e)


def _rel_bias_rows(g_s, out_ref, cb, local):
    c0 = cb * SUBLANES
    q_chunk = lax.broadcasted_iota(jnp.int32, (1, QUAD), 1) // CHUNK + LEFT_CHUNKS
    back = q_chunk - cb // (CHUNK // SUBLANES)
    valid = (back >= 0) & (back <= LEFT_CHUNKS)
    for h in range(B_HEADS):
        rows = jnp.broadcast_to(g_s[h:h + 1, :], (SUBLANES, BIAS_ROLL))
        rolled = pltpu.roll(rows, c0, 1, stride=1, stride_axis=0)
        out_ref[h, local * SUBLANES:(local + 1) * SUBLANES, :] = jnp.where(
            valid, rolled[:, :QUAD], NEG)


def _chunk_attn_kernel(qT_ref, kp_ref, kc_ref, vTp_ref, vTc_ref, bias_ref, o_ref,
                       k4_s, vTq_s, qTm_s, s_s, m_s, oT_s):
    g = pl.program_id(1)
    slab = HEAD_GROUP * B_HEAD_DIM
    n_hg = B_HEADS // HEAD_GROUP
    n_phase = (GROUP // QUAD) * (B_HEADS // PHASE_HEADS)
    half = GROUP // 2
    ones = jnp.ones((SUM_ROWS, QUAD_BAND), BF16)

    def stage_operands(first_group):
        for hg in range(n_hg):
            lanes = slice(hg * slab, (hg + 1) * slab)
            k4_s[hg, :half, :] = kp_ref[:, lanes]
            k4_s[hg, half:, :] = kc_ref[:, lanes]
        vTq_s[0, :, :GROUP] = vTp_ref[...]
        vTq_s[0, :, GROUP:] = vTc_ref[:, :QUAD]
        vTq_s[1, :, :QUAD] = vTp_ref[:, QUAD:]
        vTq_s[1, :, QUAD:] = vTc_ref[...]
        if first_group:
            qTm_s[...] = jnp.zeros(qTm_s.shape, BF16)
        for h in range(B_HEADS):
            own = slice((h % HEAD_GROUP) * B_HEAD_DIM, (h % HEAD_GROUP + 1) * B_HEAD_DIM)
            for qd in range(GROUP // QUAD):
                qTm_s[h, qd, own, :] = qT_ref[h * B_HEAD_DIM:(h + 1) * B_HEAD_DIM,
                                              qd * QUAD:(qd + 1) * QUAD]

    def phase_ids(t):
        per_quad = B_HEADS // PHASE_HEADS
        slot, qd, part = (t & 1) * PHASE_HEADS, t >> (per_quad.bit_length() - 1), t & (per_quad - 1)
        return slot, qd, part * PHASE_HEADS, part * (PHASE_HEADS // HEAD_GROUP)

    def score_unit(t, i, first_group):
        slot, qd, head0, hg0 = phase_ids(t)
        start = qd * (QUAD // 2)
        if not isinstance(start, int):
            start = pl.multiple_of(start, QUAD // 2)
        h = head0 + i
        kb = pltpu.bitcast(
            k4_s[hg0 + i // HEAD_GROUP, pl.ds(start, QUAD_BAND // 2), :], BF16)
        s = _dot(kb, qTm_s[h, qd]) + bias_ref[h]
        if first_group:
            key = lax.broadcasted_iota(jnp.int32, (QUAD_BAND, 1), 0)
            s = jnp.where(key >= GROUP - qd * QUAD, s, NEG)
        s_s[slot + i] = s
        m_s[slot + i] = jnp.max(s, axis=0, keepdims=True)

    def value_unit(t, i):
        slot, qd, head0, _ = phase_ids(t)
        row0 = (head0 + i) * B_HEAD_DIM
        if not isinstance(row0, int):
            row0 = pl.multiple_of(row0, B_HEAD_DIM)
        p = jnp.exp2(s_s[slot + i] - m_s[slot + i]).astype(BF16)
        lhs = jnp.concatenate([vTq_s[qd, pl.ds(row0, B_HEAD_DIM), :], ones], axis=0)
        r = _dot(lhs, p)
        oT_s[qd, pl.ds(row0, B_HEAD_DIM), :] = (
            r[:B_HEAD_DIM] * (1.0 / r[B_HEAD_DIM:B_HEAD_DIM + 1]))

    def run(first_group):
        stage_operands(first_group)
        for i in range(PHASE_HEADS):
            score_unit(0, i, first_group)

        for t in range(1, n_phase):
            for i in range(PHASE_HEADS):
                value_unit(t - 1, i)
                score_unit(t, i, first_group)
        for i in range(PHASE_HEADS):
            value_unit(n_phase - 1, i)

    pl.when(g == 0)(lambda: run(True))
    pl.when(g != 0)(lambda: run(False))
    for qd in range(GROUP // QUAD):
        o_ref[:, qd * QUAD:(qd + 1) * QUAD] = oT_s[qd].astype(BF16)


def _chunk_attn(qT, k, vT, bias, batch, seq):
    hd, T = qT.shape
    ng = seq // GROUP
    n_quad = GROUP // QUAD
    prev = lambda b, g: b * ng + jnp.maximum(g - 1, 0)
    cur = lambda b, g: b * ng + g
    return pl.pallas_call(
        _chunk_attn_kernel,
        grid=(batch, ng),
        in_specs=[pl.BlockSpec((hd, GROUP), lambda b, g: (0, cur(b, g))),
                  pl.BlockSpec((GROUP // 2, hd), lambda b, g: (prev(b, g), 0)),
                  pl.BlockSpec((GROUP // 2, hd), lambda b, g: (cur(b, g), 0)),
                  pl.BlockSpec((hd, GROUP), lambda b, g: (0, prev(b, g))),
                  pl.BlockSpec((hd, GROUP), lambda b, g: (0, cur(b, g))),
                  pl.BlockSpec(bias.shape, lambda b, g: (0, 0, 0),
                               pipeline_mode=pl.Buffered(1))],
        out_specs=pl.BlockSpec((hd, GROUP), lambda b, g: (0, cur(b, g))),
        out_shape=jax.ShapeDtypeStruct((hd, T), BF16),
        scratch_shapes=[pltpu.VMEM((B_HEADS // HEAD_GROUP, GROUP, HEAD_GROUP * B_HEAD_DIM),
                                   jnp.uint32),
                        pltpu.VMEM((n_quad, hd, QUAD_BAND), BF16),
                        pltpu.VMEM((B_HEADS, n_quad, HEAD_GROUP * B_HEAD_DIM, QUAD), BF16),
                        pltpu.VMEM((2 * PHASE_HEADS, QUAD_BAND, QUAD), F32),
                        pltpu.VMEM((2 * PHASE_HEADS, 1, QUAD), F32),
                        pltpu.VMEM((n_quad, hd, QUAD), F32)],
        compiler_params=pltpu.CompilerParams(
            dimension_semantics=("arbitrary", "arbitrary"), vmem_limit_bytes=VMEM_LIMIT),
        name="chunk_attn",
    )(qT, k, k, vT, vT, bias)


def kernel(x, positions, attn_pre_g, attn_post_g, ffn_pre_g, ffn_post_g, ffn_w_gate,
           ffn_w_up, ffn_w_down, mla_w_a, mla_g_q, mla_w_uq, mla_g_kv, mla_w_ukv, mla_w_o,
           kv_src_g, w_kv_shared, b_w_q, b_rel_table, b_w_o):
    batch, seq, d = x.shape
    T = batch * seq
    assert attn_pre_g.shape[0] == 2 and mla_w_a.shape[0] == 1 and b_w_q.shape[0] == 1
    assert seq % GROUP == 0 and seq % MLA_TQ == 0 and T % PROJ_ROWS == 0
    row = lambda g: g.reshape(1, -1).astype(F32)

    x2 = x.reshape(T, d)
    pos_row = positions.reshape(1, T).astype(F32)
    inv = 1.0 / (ROPE_THETA ** (jnp.arange(0, ROPE_DIM, 2, dtype=F32) / ROPE_DIM))
    inv_col = inv.reshape(ROPE_DIM // 2, 1)

    wa = mla_w_a[0][:, :Q_LORA + KV_LORA].astype(BF16)
    wkrT = jnp.pad(mla_w_a[0][:, Q_LORA + KV_LORA:].T, ((0, LANES - ROPE_DIM), (0, 0))).astype(BF16)
    wuq = mla_w_uq[0].reshape(Q_LORA, MLA_HEADS, NOPE_DIM + ROPE_DIM)
    wuq = jnp.pad(wuq, ((0, 0), (0, 0), (0, MLA_HEAD_PAD - NOPE_DIM - ROPE_DIM)))
    wuqT = wuq.reshape(Q_LORA, MLA_HEADS * MLA_HEAD_PAD).T.astype(BF16)
    wukv = mla_w_ukv[0].reshape(KV_LORA, MLA_HEADS, NOPE_DIM + V_DIM)
    wuk = wukv[:, :, :NOPE_DIM].reshape(KV_LORA, MLA_HEADS * NOPE_DIM).astype(BF16)
    wuvT = wukv[:, :, NOPE_DIM:].reshape(KV_LORA, MLA_HEADS * V_DIM).T.astype(BF16)
    hd = B_HEADS * B_HEAD_DIM
    wk = w_kv_shared[:, :hd].astype(BF16)
    wvT = w_kv_shared[:, hd:].T.astype(BF16)
    wqT = b_w_q[0].T.astype(BF16)

    qT0, kn0, kr0, vT0 = _mla_proj(x2, pos_row, inv_col, row(attn_pre_g[0]), wa, wkrT,
                                   row(mla_g_q[0]), wuqT, row(mla_g_kv[0]), wuk, wuvT)
    o0 = _mla_attn(qT0, kn0, kr0, vT0, batch, seq)
    wg, wu, wd = ffn_w_gate.astype(BF16), ffn_w_up.astype(BF16), ffn_w_down.astype(BF16)
    x2 = _block_tail(o0, x2, mla_w_o[0].astype(BF16), row(attn_post_g[0]), row(ffn_pre_g[0]),
                     wg, wu, wd, row(ffn_post_g[0]), 0)

    qT1, k1, vT1, bias = _kvq_proj(x2, row(kv_src_g), row(attn_pre_g[1]), wqT, wk, wvT,
                                   b_rel_table[0])
    o1 = _chunk_attn(qT1, k1, vT1, bias, batch, seq)
    x2 = _block_tail(o1, x2, b_w_o[0].astype(BF16), row(attn_post_g[1]), row(ffn_pre_g[1]),
                     wg, wu, wd, row(ffn_post_g[1]), 1)
    return x2.reshape(batch, seq, d)
```

```python
import functools
import math

import jax
import jax.numpy as jnp
from jax import lax
from jax.experimental import pallas as pl
from jax.experimental.pallas import tpu as pltpu

F32 = jnp.float32
BF16 = jnp.bfloat16

CHUNK = 64
MLA_HEADS = 8
Q_LORA = 384
KV_LORA = 256
NOPE_DIM = 128
ROPE_DIM = 64
V_DIM = 128
ROPE_THETA = 10000.0
B_HEADS = 16
B_HEAD_DIM = 64
LEFT_CHUNKS = 8
BAND = (LEFT_CHUNKS + 1) * CHUNK
MAX_REL = 256
REL_TABLE = MAX_REL + CHUNK
EPS = 1e-6

LANES = 128
SUBLANES = 8
MLA_HEAD_PAD = 2 * LANES
NEG = -1e30

PROJ_ROWS = 1024
TAIL_ROWS = 512
TAIL_SPLIT = 2
MLA_TQ = 512
MLA_TK = 256
MLA_SLOTS = 4
SUM_ROWS = 16
QUAD = 4 * CHUNK
QUAD_BAND = BAND + 3 * CHUNK
GROUP = 2 * QUAD
HEAD_GROUP = 2
UNIT_SKEW = 4
BIAS_ROLL = 1024
V7X_VMEM_BYTES = 64 * 1024 * 1024
VMEM_LIMIT = V7X_VMEM_BYTES - 8 * 1024 * 1024


def _rms(x, g):
    ms = jnp.mean(x * x, axis=-1, keepdims=True)
    return x * lax.rsqrt(ms + EPS) * g


def _dot(a, b):
    return jnp.dot(a, b, preferred_element_type=F32)


def _dot_nt(a, b):
    return lax.dot_general(a, b, (((1,), (1,)), ((), ())), preferred_element_type=F32)


def _mla_proj_kernel(x_ref, posr_ref, invc_ref, gpre_ref, wa_ref, wkrT_ref, gq_ref,
                     wuqT_ref, gkv_ref, wuk_ref, wuvT_ref, qT_ref, kn_ref, kr_ref, vT_ref, *,
                     scale):
    h = _rms(x_ref[...], gpre_ref[...]).astype(BF16)
    a = _dot(h, wa_ref[...])
    cqn = _rms(a[:, :Q_LORA], gq_ref[...]).astype(BF16)
    ckvn = _rms(a[:, Q_LORA:], gkv_ref[...]).astype(BF16)
    half = ROPE_DIM // 2

    angT = invc_ref[...] * posr_ref[...]
    cosT, sinT = jnp.cos(angT), jnp.sin(angT)

    krT = _dot_nt(wkrT_ref[...], h)
    k1, k2 = krT[:half], krT[half:ROPE_DIM]
    kr_rotT = jnp.concatenate(
        [k1 * cosT - k2 * sinT, k2 * cosT + k1 * sinT,
         jnp.zeros((LANES - ROPE_DIM, krT.shape[1]), F32)], axis=0)
    kr_ref[...] = pltpu.bitcast(kr_rotT.T.astype(BF16), jnp.uint32)
    kn_ref[...] = pltpu.bitcast(_dot(ckvn, wuk_ref[...]).astype(BF16), jnp.uint32)

    cosT, sinT = cosT * scale, sinT * scale
    qT = _dot_nt(wuqT_ref[...], cqn)
    for hd in range(MLA_HEADS):
        lo = hd * MLA_HEAD_PAD
        t1 = qT[lo + NOPE_DIM:lo + NOPE_DIM + half]
        t2 = qT[lo + NOPE_DIM + half:lo + NOPE_DIM + ROPE_DIM]
        qT_ref[lo:lo + NOPE_DIM] = (qT[lo:lo + NOPE_DIM] * scale).astype(BF16)
        qT_ref[lo + NOPE_DIM:lo + NOPE_DIM + half] = (t1 * cosT - t2 * sinT).astype(BF16)
        qT_ref[lo + NOPE_DIM + half:lo + NOPE_DIM + ROPE_DIM] = (t2 * cosT + t1 * sinT).astype(BF16)
        qT_ref[lo + NOPE_DIM + ROPE_DIM:lo + MLA_HEAD_PAD] = jnp.zeros(
            (MLA_HEAD_PAD - NOPE_DIM - ROPE_DIM, qT.shape[1]), BF16)

    vT = _dot_nt(wuvT_ref[...], ckvn).astype(BF16)
    for t in range(vT_ref.shape[0]):
        vT_ref[t] = vT[:, t * MLA_TK:(t + 1) * MLA_TK]


def _mla_proj(x2, pos_row, inv_col, gpre, wa, wkrT, gq, wuqT, gkv, wuk, wuvT):
    T, D = x2.shape
    tm = PROJ_ROWS
    const = lambda i: (0, 0)
    full = lambda a: pl.BlockSpec(a.shape, const)
    scale = float((NOPE_DIM + ROPE_DIM) ** -0.5 * math.log2(math.e))
    return pl.pallas_call(
        functools.partial(_mla_proj_kernel, scale=scale),
        grid=(T // tm,),
        in_specs=[pl.BlockSpec((tm, D), lambda i: (i, 0)),
                  pl.BlockSpec((1, tm), lambda i: (0, i)),
                  full(inv_col), full(gpre), full(wa), full(wkrT), full(gq), full(wuqT),
                  full(gkv), full(wuk), full(wuvT)],
        out_specs=[pl.BlockSpec((MLA_HEADS * MLA_HEAD_PAD, tm), lambda i: (0, i)),
                   pl.BlockSpec((tm // 2, MLA_HEADS * NOPE_DIM), lambda i: (i, 0)),
                   pl.BlockSpec((tm // 2, LANES), lambda i: (i, 0)),
                   pl.BlockSpec((tm // MLA_TK, MLA_HEADS * V_DIM, MLA_TK), lambda i: (i, 0, 0))],
        out_shape=[jax.ShapeDtypeStruct((MLA_HEADS * MLA_HEAD_PAD, T), BF16),
                   jax.ShapeDtypeStruct((T // 2, MLA_HEADS * NOPE_DIM), jnp.uint32),
                   jax.ShapeDtypeStruct((T // 2, LANES), jnp.uint32),
                   jax.ShapeDtypeStruct((T // MLA_TK, MLA_HEADS * V_DIM, MLA_TK), BF16)],
        compiler_params=pltpu.CompilerParams(
            dimension_semantics=("arbitrary",), vmem_limit_bytes=VMEM_LIMIT),
        name="mla_proj",
    )(x2, pos_row, inv_col, gpre, wa, wkrT, gq, wuqT, gkv, wuk, wuvT)


def _mla_attn_kernel(qT_ref, kn_ref, kr_ref, vT_ref, oT_ref, m_s, l_s, a_s, acc_s, s_s):
    qi = pl.program_id(1)
    n_full = qi * (MLA_TQ // MLA_TK)
    m_s[...] = jnp.full(m_s.shape, NEG, F32)
    l_s[...] = jnp.zeros(l_s.shape, F32)
    a_s[...] = jnp.ones(a_s.shape, F32)
    acc_s[...] = jnp.zeros(acc_s.shape, F32)
    ones = jnp.ones((SUM_ROWS, MLA_TK), BF16)
    half = MLA_TK // 2

    def keys(j, tiles, hd):
        rows = pl.ds(pl.multiple_of(j * half, half), tiles * half)
        k_nope = pltpu.bitcast(kn_ref[rows, hd * NOPE_DIM:(hd + 1) * NOPE_DIM], BF16)
        k_rope = pltpu.bitcast(kr_ref[rows, :], BF16)
        return jnp.concatenate([k_nope, k_rope], axis=1)

    def diag_score(d, hd):
        slot = (d - 2) % MLA_SLOTS * MLA_HEADS + hd
        q0 = d * MLA_TK
        qT = qT_ref[hd * MLA_HEAD_PAD:(hd + 1) * MLA_HEAD_PAD, q0:]
        sT = _dot(keys(n_full + d, 1, hd), qT)
        kc = lax.broadcasted_iota(jnp.int32, sT.shape, 0) // CHUNK
        qc = lax.broadcasted_iota(jnp.int32, sT.shape, 1) // CHUNK
        sT = jnp.where(kc <= qc, sT, NEG)
        s_s[slot, :, q0:] = sT
        m_s[hd, :, q0:] = jnp.maximum(m_s[hd, :, q0:], jnp.max(sT, axis=0, keepdims=True))
        if q0:
            s_s[slot, :, :q0] = jnp.full((MLA_TK, q0), -jnp.inf, F32)

    def pair_score(j, slot0, hd):
        qT = qT_ref[hd * MLA_HEAD_PAD:(hd + 1) * MLA_HEAD_PAD, :]
        sT = _dot(keys(j, 2, hd), qT)
        m = m_s[hd]
        m_new = jnp.maximum(m, jnp.max(sT, axis=0, keepdims=True))
        s_s[slot0 * MLA_HEADS + hd] = sT[:MLA_TK]
        s_s[(slot0 + 1) * MLA_HEADS + hd] = sT[MLA_TK:]
        a_s[slot0 * MLA_HEADS + hd] = jnp.exp2(m - m_new)
        a_s[(slot0 + 1) * MLA_HEADS + hd] = jnp.ones((1, MLA_TQ), F32)
        m_s[hd] = m_new

    def value_unit(j, slot, hd):
        jv = jnp.where(j < 0, n_full + 2 + j, j)
        p = jnp.exp2(s_s[slot * MLA_HEADS + hd] - m_s[hd]).astype(BF16)
        lhs = jnp.concatenate([vT_ref[jv, hd * V_DIM:(hd + 1) * V_DIM, :], ones], axis=0)
        pv = _dot(lhs, p)
        a = a_s[slot * MLA_HEADS + hd]
        acc_s[hd] = a * acc_s[hd] + pv[:V_DIM]
        l_s[hd] = a * l_s[hd] + pv[V_DIM:V_DIM + 1]

    def full_pairs(base, pairs):
        for u in range(0, 2 * pairs, 2):
            for hd in range(MLA_HEADS):
                value_unit(base + u - 2, (u - 2) % MLA_SLOTS, hd)
                value_unit(base + u - 1, (u - 1) % MLA_SLOTS, hd)
                pair_score(base + u, u % MLA_SLOTS, hd)

    def steady(i, carry):
        full_pairs(i * MLA_SLOTS, MLA_SLOTS // 2)
        return carry

    for hd in range(MLA_HEADS):
        diag_score(0, hd)
        diag_score(1, hd)
    n_trips = n_full // MLA_SLOTS
    lax.fori_loop(0, n_trips, steady, 0)

    @pl.when(n_full - n_trips * MLA_SLOTS == 2)
    def _():
        full_pairs(n_full - 2, 1)

    for hd in range(MLA_HEADS):
        value_unit(n_full - 2, (n_full - 2) & (MLA_SLOTS - 1), hd)
        value_unit(n_full - 1, (n_full - 1) & (MLA_SLOTS - 1), hd)
    for hd in range(MLA_HEADS):
        oT_ref[hd * V_DIM:(hd + 1) * V_DIM, :] = (acc_s[hd] * (1.0 / l_s[hd])).astype(BF16)


def _mla_attn(qT, kn, kr, vT3, batch, seq):
    assert MLA_TQ == 2 * MLA_TK and MLA_TK % CHUNK == 0 and MLA_SLOTS == 4
    T = qT.shape[1]
    nq = seq // MLA_TQ
    nkt = seq // MLA_TK
    return pl.pallas_call(
        _mla_attn_kernel,
        grid=(batch, nq),
        in_specs=[pl.BlockSpec((MLA_HEADS * MLA_HEAD_PAD, MLA_TQ), lambda b, i: (0, b * nq + i)),
                  pl.BlockSpec((seq // 2, MLA_HEADS * NOPE_DIM), lambda b, i: (b, 0)),
                  pl.BlockSpec((seq // 2, LANES), lambda b, i: (b, 0)),
                  pl.BlockSpec((nkt, MLA_HEADS * V_DIM, MLA_TK), lambda b, i: (b, 0, 0),
                               pipeline_mode=pl.Buffered(1))],
        out_specs=pl.BlockSpec((MLA_HEADS * V_DIM, MLA_TQ), lambda b, i: (0, b * nq + i)),
        out_shape=jax.ShapeDtypeStruct((MLA_HEADS * V_DIM, T), BF16),
        scratch_shapes=[pltpu.VMEM((MLA_HEADS, 1, MLA_TQ), F32),
                        pltpu.VMEM((MLA_HEADS, 1, MLA_TQ), F32),
                        pltpu.VMEM((MLA_SLOTS * MLA_HEADS, 1, MLA_TQ), F32),
                        pltpu.VMEM((MLA_HEADS, V_DIM, MLA_TQ), F32),
                        pltpu.VMEM((MLA_SLOTS * MLA_HEADS, MLA_TK, MLA_TQ), F32)],
        compiler_params=pltpu.CompilerParams(
            dimension_semantics=("arbitrary", "arbitrary"),
            vmem_limit_bytes=VMEM_LIMIT),
        name="mla_attn",
    )(qT, kn, kr, vT3)


def _block_tail_kernel(oT_ref, x_ref, wo_ref, gpost_ref, gfpre_ref, wg_ref, wu_ref,
                       wd_ref, gfpost_ref, out_ref):
    sub = out_ref.shape[0] // TAIL_SPLIT
    blocks = [slice(i * sub, (i + 1) * sub) for i in range(TAIL_SPLIT)]
    y = [lax.dot_general(oT_ref[:, r], wo_ref[...], (((0,), (0,)), ((), ())),
                         preferred_element_type=F32) for r in blocks]
    x1 = [x_ref[r, :] + _rms(yi, gpost_ref[...]) for r, yi in zip(blocks, y)]
    h = [_rms(xi, gfpre_ref[...]).astype(BF16) for xi in x1]
    gu = [(_dot(hi, wg_ref[...]), _dot(hi, wu_ref[...])) for hi in h]
    a = [(gi * (1.0 / (1.0 + jnp.exp(-gi))) * ui).astype(BF16) for gi, ui in gu]
    f = [_dot(ai, wd_ref[...]) for ai in a]
    for r, xi, fi in zip(blocks, x1, f):
        out_ref[r, :] = xi + _rms(fi, gfpost_ref[...])


def _block_tail(oT, x2, wo, gpost, gfpre, wg, wu, wd, gfpost, layer):
    T, D = x2.shape
    tm = TAIL_ROWS
    const = lambda i: (0, 0)
    full = lambda a: pl.BlockSpec(a.shape, const, pipeline_mode=pl.Buffered(1))
    of_layer = lambda a: pl.BlockSpec((None,) + a.shape[1:], lambda i: (layer, 0, 0),
                                      pipeline_mode=pl.Buffered(1))
    row = lambda a: pl.BlockSpec((tm, a.shape[1]), lambda i: (i, 0))
    return pl.pallas_call(
        _block_tail_kernel,
        grid=(T // tm,),
        in_specs=[pl.BlockSpec((oT.shape[0], tm), lambda i: (0, i)), row(x2), full(wo),
                  full(gpost), full(gfpre), of_layer(wg), of_layer(wu), of_layer(wd),
                  full(gfpost)],
        out_specs=pl.BlockSpec((tm, D), lambda i: (i, 0)),
        out_shape=jax.ShapeDtypeStruct((T, D), F32),
        compiler_params=pltpu.CompilerParams(
            dimension_semantics=("arbitrary",), vmem_limit_bytes=VMEM_LIMIT),
        name="block_tail",
    )(oT, x2, wo, gpost, gfpre, wg, wu, wd, gfpost)


def _kvq_proj_kernel(x_ref, gsrc_ref, gpre_ref, wqT_ref, wk_ref, wvT_ref, tab_ref,
                     qT_ref, k_ref, vT_ref, bias_ref, g_s, *, scale):
    i = pl.program_id(0)

    @pl.when(i == 0)
    def _():
        _rel_bias_gather(tab_ref[...], g_s)

    blocks = bias_ref.shape[1] // SUBLANES
    for local in range(blocks):
        _rel_bias_rows(g_s, bias_ref, i * blocks + local, local)
    x = x_ref[...]
    xr = x * lax.rsqrt(jnp.mean(x * x, axis=-1, keepdims=True) + EPS)
    xs = (xr * gsrc_ref[...]).astype(BF16)
    xq = (xr * gpre_ref[...]).astype(BF16)
    qT_ref[...] = (_dot_nt(wqT_ref[...], xq) * scale).astype(BF16)
    k_ref[...] = pltpu.bitcast(_dot(xs, wk_ref[...]).astype(BF16), jnp.uint32)
    vT_ref[...] = _dot_nt(wvT_ref[...], xs).astype(BF16)


def _kvq_proj(x2, gsrc, gpre, wqT, wk, wvT, table):
    T, D = x2.shape
    tm = PROJ_ROWS
    steps = T // tm
    hd = B_HEADS * B_HEAD_DIM
    assert QUAD_BAND % (steps * SUBLANES) == 0
    const = lambda i: (0, 0)
    full = lambda a: pl.BlockSpec(a.shape, const)
    scale = float(B_HEAD_DIM ** -0.5 * math.log2(math.e))
    return pl.pallas_call(
        functools.partial(_kvq_proj_kernel, scale=scale),
        grid=(steps,),
        in_specs=[pl.BlockSpec((tm, D), lambda i: (i, 0)),
                  full(gsrc), full(gpre), full(wqT), full(wk), full(wvT), full(table)],
        out_specs=[pl.BlockSpec((hd, tm), lambda i: (0, i)),
                   pl.BlockSpec((tm // 2, hd), lambda i: (i, 0)),
                   pl.BlockSpec((hd, tm), lambda i: (0, i)),
                   pl.BlockSpec((B_HEADS, QUAD_BAND // steps, QUAD), lambda i: (0, i, 0))],
        out_shape=[jax.ShapeDtypeStruct((hd, T), BF16),
                   jax.ShapeDtypeStruct((T // 2, hd), jnp.uint32),
                   jax.ShapeDtypeStruct((hd, T), BF16),
                   jax.ShapeDtypeStruct((B_HEADS, QUAD_BAND, QUAD), F32)],
        scratch_shapes=[pltpu.VMEM((B_HEADS, BIAS_ROLL), F32)],
        compiler_params=pltpu.CompilerParams(
            dimension_semantics=("arbitrary",), vmem_limit_bytes=VMEM_LIMIT),
        name="kvq_proj",
    )(x2, gsrc, gpre, wqT, wk, wvT, table)


def _rel_bias_gather(tab, g_s):
    t_hi = tab.astype(BF16)
    rem = tab - t_hi.astype(F32)
    t_mid = rem.astype(BF16)
    t_lo = (rem - t_mid.astype(F32)).astype(BF16)
    row = lax.broadcasted_iota(jnp.int32, (REL_TABLE, BIAS_ROLL), 0)
    lane = lax.broadcasted_iota(jnp.int32, (REL_TABLE, BIAS_ROLL), 1)
    dist = jnp.where(lane < QUAD, lane, lane - BIAS_ROLL) + LEFT_CHUNKS * CHUNK
    idx = jnp.clip(dist, -(CHUNK - 1), MAX_REL) + (CHUNK - 1)
    onehot = jnp.where(idx == row, 1.0, 0.0).astype(BF16)
    g_s[...] = (_dot(t_hi, onehot) + _dot(t_mid, onehot) + _dot(t_lo, onehot)) * math.log2(math.e)


def _rel_bias_rows(g_s, out_ref, cb, local):
    c0 = cb * SUBLANES
    q_chunk = lax.broadcasted_iota(jnp.int32, (1, QUAD), 1) // CHUNK + LEFT_CHUNKS
    back = q_chunk - cb // (CHUNK // SUBLANES)
    valid = (back >= 0) & (back <= LEFT_CHUNKS)
    for h in range(B_HEADS):
        rows = jnp.broadcast_to(g_s[h:h + 1, :], (SUBLANES, BIAS_ROLL))
        rolled = pltpu.roll(rows, c0, 1, stride=1, stride_axis=0)
        out_ref[h, local * SUBLANES:(local + 1) * SUBLANES, :] = jnp.where(
            valid, rolled[:, :QUAD], NEG)


def _chunk_attn_kernel(qT_ref, kp_ref, kc_ref, vTp_ref, vTc_ref, bias_ref, o_ref,
                       k4_s, vTq_s, qTm_s, s_s, m_s, oT_s):
    g = pl.program_id(1)
    slab = HEAD_GROUP * B_HEAD_DIM
    n_hg = B_HEADS // HEAD_GROUP
    half = GROUP // 2
    ones = jnp.ones((SUM_ROWS, QUAD_BAND), BF16)

    def stage_operands(first_group):
        for hg in range(n_hg):
            lanes = slice(hg * slab, (hg + 1) * slab)
            k4_s[hg, :half, :] = kp_ref[:, lanes]
            k4_s[hg, half:, :] = kc_ref[:, lanes]
        vTq_s[0, :, :GROUP] = vTp_ref[...]
        vTq_s[0, :, GROUP:] = vTc_ref[:, :QUAD]
        vTq_s[1, :, :QUAD] = vTp_ref[:, QUAD:]
        vTq_s[1, :, QUAD:] = vTc_ref[...]
        if first_group:
            qTm_s[...] = jnp.zeros(qTm_s.shape, BF16)
        for h in range(B_HEADS):
            own = slice((h % HEAD_GROUP) * B_HEAD_DIM, (h % HEAD_GROUP + 1) * B_HEAD_DIM)
            for qd in range(GROUP // QUAD):
                qTm_s[h, qd, own, :] = qT_ref[h * B_HEAD_DIM:(h + 1) * B_HEAD_DIM,
                                              qd * QUAD:(qd + 1) * QUAD]

    n_units = (GROUP // QUAD) * B_HEADS

    def score_unit(u, first_group):
        qd, h = divmod(u, B_HEADS)
        start = qd * (QUAD // 2)
        kb = pltpu.bitcast(
            k4_s[h // HEAD_GROUP, start:start + QUAD_BAND // 2, :], BF16)
        s = _dot(kb, qTm_s[h, qd]) + bias_ref[h]
        if first_group:
            key = lax.broadcasted_iota(jnp.int32, (QUAD_BAND, 1), 0)
            s = jnp.where(key >= GROUP - qd * QUAD, s, NEG)
        slot = u % (UNIT_SKEW + 1)
        s_s[slot] = s
        m_s[slot] = jnp.max(s, axis=0, keepdims=True)

    def value_unit(u):
        qd, h = divmod(u, B_HEADS)
        slot = u % (UNIT_SKEW + 1)
        rows = slice(h * B_HEAD_DIM, (h + 1) * B_HEAD_DIM)
        p = jnp.exp2(s_s[slot] - m_s[slot]).astype(BF16)
        lhs = jnp.concatenate([vTq_s[qd, rows, :], ones], axis=0)
        r = _dot(lhs, p)
        oT_s[qd, rows, :] = r[:B_HEAD_DIM] * (1.0 / r[B_HEAD_DIM:B_HEAD_DIM + 1])

    def run(first_group):
        stage_operands(first_group)
        for u in range(n_units + UNIT_SKEW):
            if u >= UNIT_SKEW:
                value_unit(u - UNIT_SKEW)
            if u < n_units:
                score_unit(u, first_group)

    pl.when(g == 0)(lambda: run(True))
    pl.when(g != 0)(lambda: run(False))
    for qd in range(GROUP // QUAD):
        o_ref[:, qd * QUAD:(qd + 1) * QUAD] = oT_s[qd].astype(BF16)


def _chunk_attn(qT, k, vT, bias, batch, seq):
    hd, T = qT.shape
    ng = seq // GROUP
    n_quad = GROUP // QUAD
    prev = lambda b, g: b * ng + jnp.maximum(g - 1, 0)
    cur = lambda b, g: b * ng + g
    return pl.pallas_call(
        _chunk_attn_kernel,
        grid=(batch, ng),
        in_specs=[pl.BlockSpec((hd, GROUP), lambda b, g: (0, cur(b, g))),
                  pl.BlockSpec((GROUP // 2, hd), lambda b, g: (prev(b, g), 0)),
                  pl.BlockSpec((GROUP // 2, hd), lambda b, g: (cur(b, g), 0)),
                  pl.BlockSpec((hd, GROUP), lambda b, g: (0, prev(b, g))),
                  pl.BlockSpec((hd, GROUP), lambda b, g: (0, cur(b, g))),
                  pl.BlockSpec(bias.shape, lambda b, g: (0, 0, 0),
                               pipeline_mode=pl.Buffered(1))],
        out_specs=pl.BlockSpec((hd, GROUP), lambda b, g: (0, cur(b, g))),
        out_shape=jax.ShapeDtypeStruct((hd, T), BF16),
        scratch_shapes=[pltpu.VMEM((B_HEADS // HEAD_GROUP, GROUP, HEAD_GROUP * B_HEAD_DIM),
                                   jnp.uint32),
                        pltpu.VMEM((n_quad, hd, QUAD_BAND), BF16),
                        pltpu.VMEM((B_HEADS, n_quad, HEAD_GROUP * B_HEAD_DIM, QUAD), BF16),
                        pltpu.VMEM((UNIT_SKEW + 1, QUAD_BAND, QUAD), F32),
                        pltpu.VMEM((UNIT_SKEW + 1, 1, QUAD), F32),
                        pltpu.VMEM((n_quad, hd, QUAD), F32)],
        compiler_params=pltpu.CompilerParams(
            dimension_semantics=("arbitrary", "arbitrary"), vmem_limit_bytes=VMEM_LIMIT),
        name="chunk_attn",
    )(qT, k, k, vT, vT, bias)


def kernel(x, positions, attn_pre_g, attn_post_g, ffn_pre_g, ffn_post_g, ffn_w_gate,
           ffn_w_up, ffn_w_down, mla_w_a, mla_g_q, mla_w_uq, mla_g_kv, mla_w_ukv, mla_w_o,
           kv_src_g, w_kv_shared, b_w_q, b_rel_table, b_w_o):
    batch, seq, d = x.shape
    T = batch * seq
    assert attn_pre_g.shape[0] == 2 and mla_w_a.shape[0] == 1 and b_w_q.shape[0] == 1
    assert seq % GROUP == 0 and seq % MLA_TQ == 0 and T % PROJ_ROWS == 0
    row = lambda g: g.reshape(1, -1).astype(F32)

    x2 = x.reshape(T, d)
    pos_row = positions.reshape(1, T).astype(F32)
    inv = 1.0 / (ROPE_THETA ** (jnp.arange(0, ROPE_DIM, 2, dtype=F32) / ROPE_DIM))
    inv_col = inv.reshape(ROPE_DIM // 2, 1)

    wa = mla_w_a[0][:, :Q_LORA + KV_LORA].astype(BF16)
    wkrT = jnp.pad(mla_w_a[0][:, Q_LORA + KV_LORA:].T, ((0, LANES - ROPE_DIM), (0, 0))).astype(BF16)
    wuq = mla_w_uq[0].reshape(Q_LORA, MLA_HEADS, NOPE_DIM + ROPE_DIM)
    wuq = jnp.pad(wuq, ((0, 0), (0, 0), (0, MLA_HEAD_PAD - NOPE_DIM - ROPE_DIM)))
    wuqT = wuq.reshape(Q_LORA, MLA_HEADS * MLA_HEAD_PAD).T.astype(BF16)
    wukv = mla_w_ukv[0].reshape(KV_LORA, MLA_HEADS, NOPE_DIM + V_DIM)
    wuk = wukv[:, :, :NOPE_DIM].reshape(KV_LORA, MLA_HEADS * NOPE_DIM).astype(BF16)
    wuvT = wukv[:, :, NOPE_DIM:].reshape(KV_LORA, MLA_HEADS * V_DIM).T.astype(BF16)
    hd = B_HEADS * B_HEAD_DIM
    wk = w_kv_shared[:, :hd].astype(BF16)
    wvT = w_kv_shared[:, hd:].T.astype(BF16)
    wqT = b_w_q[0].T.astype(BF16)

    qT0, kn0, kr0, vT0 = _mla_proj(x2, pos_row, inv_col, row(attn_pre_g[0]), wa, wkrT,
                                   row(mla_g_q[0]), wuqT, row(mla_g_kv[0]), wuk, wuvT)
    o0 = _mla_attn(qT0, kn0, kr0, vT0, batch, seq)
    wg, wu, wd = ffn_w_gate.astype(BF16), ffn_w_up.astype(BF16), ffn_w_down.astype(BF16)
    x2 = _block_tail(o0, x2, mla_w_o[0].astype(BF16), row(attn_post_g[0]), row(ffn_pre_g[0]),
                     wg, wu, wd, row(ffn_post_g[0]), 0)

    qT1, k1, vT1, bias = _kvq_proj(x2, row(kv_src_g), row(attn_pre_g[1]), wqT, wk, wvT,
                                   b_rel_table[0])
    o1 = _chunk_attn(qT1, k1, vT1, bias, batch, seq)
    x2 = _block_tail(o1, x2, b_w_o[0].astype(BF16), row(attn_post_g[1]), row(ffn_pre_g[1]),
                     wg, wu, wd, row(ffn_post_g[1]), 1)
    return x2.reshape(batch, seq, d)
```

```python
import functools
import math

import jax
import jax.numpy as jnp
from jax import lax
from jax.experimental import pallas as pl
from jax.experimental.pallas import tpu as pltpu

F32 = jnp.float32
BF16 = jnp.bfloat16

CHUNK = 64
MLA_HEADS = 8
Q_LORA = 384
KV_LORA = 256
NOPE_DIM = 128
ROPE_DIM = 64
V_DIM = 128
ROPE_THETA = 10000.0
B_HEADS = 16
B_HEAD_DIM = 64
LEFT_CHUNKS = 8
BAND = (LEFT_CHUNKS + 1) * CHUNK
MAX_REL = 256
REL_TABLE = MAX_REL + CHUNK
EPS = 1e-6

LANES = 128
SUBLANES = 8
MLA_HEAD_PAD = 2 * LANES
NEG = -1e30

PROJ_ROWS = 1024
TAIL_ROWS = 512
TAIL_SPLIT = 2
MLA_TQ = 512
MLA_TK = 256
MLA_SLOTS = 4
SUM_ROWS = 16
QUAD = 4 * CHUNK
QUAD_BAND = BAND + 3 * CHUNK
GROUP = 2 * QUAD
HEAD_GROUP = 2
UNIT_SKEW = 4
BIAS_ROLL = 1024
V7X_VMEM_BYTES = 64 * 1024 * 1024
VMEM_LIMIT = V7X_VMEM_BYTES - 8 * 1024 * 1024


def _rms(x, g):
    ms = jnp.mean(x * x, axis=-1, keepdims=True)
    return x * lax.rsqrt(ms + EPS) * g


def _dot(a, b):
    return jnp.dot(a, b, preferred_element_type=F32)


def _dot_nt(a, b):
    return lax.dot_general(a, b, (((1,), (1,)), ((), ())), preferred_element_type=F32)


def _mla_proj_kernel(x_ref, posr_ref, invc_ref, gpre_ref, wa_ref, wkrT_ref, gq_ref,
                     wuqT_ref, gkv_ref, wuk_ref, wuvT_ref, qT_ref, kn_ref, kr_ref, vT_ref, *,
                     scale):
    h = _rms(x_ref[...], gpre_ref[...]).astype(BF16)
    a = _dot(h, wa_ref[...])
    cqn = _rms(a[:, :Q_LORA], gq_ref[...]).astype(BF16)
    ckvn = _rms(a[:, Q_LORA:], gkv_ref[...]).astype(BF16)
    half = ROPE_DIM // 2

    angT = invc_ref[...] * posr_ref[...]
    cosT, sinT = jnp.cos(angT), jnp.sin(angT)

    krT = _dot_nt(wkrT_ref[...], h)
    k1, k2 = krT[:half], krT[half:ROPE_DIM]
    kr_rotT = jnp.concatenate(
        [k1 * cosT - k2 * sinT, k2 * cosT + k1 * sinT,
         jnp.zeros((LANES - ROPE_DIM, krT.shape[1]), F32)], axis=0)
    kr_ref[...] = pltpu.bitcast(kr_rotT.T.astype(BF16), jnp.uint32)
    kn_ref[...] = pltpu.bitcast(_dot(ckvn, wuk_ref[...]).astype(BF16), jnp.uint32)

    cosT, sinT = cosT * scale, sinT * scale
    qT = _dot_nt(wuqT_ref[...], cqn)
    for hd in range(MLA_HEADS):
        lo = hd * MLA_HEAD_PAD
        t1 = qT[lo + NOPE_DIM:lo + NOPE_DIM + half]
        t2 = qT[lo + NOPE_DIM + half:lo + NOPE_DIM + ROPE_DIM]
        qT_ref[lo:lo + NOPE_DIM] = (qT[lo:lo + NOPE_DIM] * scale).astype(BF16)
        qT_ref[lo + NOPE_DIM:lo + NOPE_DIM + half] = (t1 * cosT - t2 * sinT).astype(BF16)
        qT_ref[lo + NOPE_DIM + half:lo + NOPE_DIM + ROPE_DIM] = (t2 * cosT + t1 * sinT).astype(BF16)
        qT_ref[lo + NOPE_DIM + ROPE_DIM:lo + MLA_HEAD_PAD] = jnp.zeros(
            (MLA_HEAD_PAD - NOPE_DIM - ROPE_DIM, qT.shape[1]), BF16)

    vT = _dot_nt(wuvT_ref[...], ckvn).astype(BF16)
    for t in range(vT_ref.shape[0]):
        vT_ref[t] = vT[:, t * MLA_TK:(t + 1) * MLA_TK]


def _mla_proj(x2, pos_row, inv_col, gpre, wa, wkrT, gq, wuqT, gkv, wuk, wuvT):
    T, D = x2.shape
    tm = PROJ_ROWS
    const = lambda i: (0, 0)
    full = lambda a: pl.BlockSpec(a.shape, const)
    scale = float((NOPE_DIM + ROPE_DIM) ** -0.5 * math.log2(math.e))
    return pl.pallas_call(
        functools.partial(_mla_proj_kernel, scale=scale),
        grid=(T // tm,),
        in_specs=[pl.BlockSpec((tm, D), lambda i: (i, 0)),
                  pl.BlockSpec((1, tm), lambda i: (0, i)),
                  full(inv_col), full(gpre), full(wa), full(wkrT), full(gq), full(wuqT),
                  full(gkv), full(wuk), full(wuvT)],
        out_specs=[pl.BlockSpec((MLA_HEADS * MLA_HEAD_PAD, tm), lambda i: (0, i)),
                   pl.BlockSpec((tm // 2, MLA_HEADS * NOPE_DIM), lambda i: (i, 0)),
                   pl.BlockSpec((tm // 2, LANES), lambda i: (i, 0)),
                   pl.BlockSpec((tm // MLA_TK, MLA_HEADS * V_DIM, MLA_TK), lambda i: (i, 0, 0))],
        out_shape=[jax.ShapeDtypeStruct((MLA_HEADS * MLA_HEAD_PAD, T), BF16),
                   jax.ShapeDtypeStruct((T // 2, MLA_HEADS * NOPE_DIM), jnp.uint32),
                   jax.ShapeDtypeStruct((T // 2, LANES), jnp.uint32),
                   jax.ShapeDtypeStruct((T // MLA_TK, MLA_HEADS * V_DIM, MLA_TK), BF16)],
        compiler_params=pltpu.CompilerParams(
            dimension_semantics=("arbitrary",), vmem_limit_bytes=VMEM_LIMIT),
        name="mla_proj",
    )(x2, pos_row, inv_col, gpre, wa, wkrT, gq, wuqT, gkv, wuk, wuvT)


def _mla_attn_kernel(qT_ref, kn_ref, kr_ref, vT_ref, oT_ref, m_s, l_s, a_s, acc_s, s_s):
    qi = pl.program_id(1)
    n_full = qi * (MLA_TQ // MLA_TK)
    m_s[...] = jnp.full(m_s.shape, NEG, F32)
    l_s[...] = jnp.zeros(l_s.shape, F32)
    a_s[...] = jnp.ones(a_s.shape, F32)
    acc_s[...] = jnp.zeros(acc_s.shape, F32)
    ones = jnp.ones((SUM_ROWS, MLA_TK), BF16)
    half = MLA_TK // 2

    def keys(j, tiles, hd):
        rows = pl.ds(pl.multiple_of(j * half, half), tiles * half)
        k_nope = pltpu.bitcast(kn_ref[rows, hd * NOPE_DIM:(hd + 1) * NOPE_DIM], BF16)
        k_rope = pltpu.bitcast(kr_ref[rows, :], BF16)
        return jnp.concatenate([k_nope, k_rope], axis=1)

    def diag_score(d, hd):
        slot = (d - 2) % MLA_SLOTS * MLA_HEADS + hd
        q0 = d * MLA_TK
        qT = qT_ref[hd * MLA_HEAD_PAD:(hd + 1) * MLA_HEAD_PAD, q0:]
        sT = _dot(keys(n_full + d, 1, hd), qT)
        kc = lax.broadcasted_iota(jnp.int32, sT.shape, 0) // CHUNK
        qc = lax.broadcasted_iota(jnp.int32, sT.shape, 1) // CHUNK
        sT = jnp.where(kc <= qc, sT, NEG)
        s_s[slot, :, q0:] = sT
        m_s[hd, :, q0:] = jnp.maximum(m_s[hd, :, q0:], jnp.max(sT, axis=0, keepdims=True))
        if q0:
            s_s[slot, :, :q0] = jnp.full((MLA_TK, q0), -jnp.inf, F32)

    def pair_score(j, slot0, hd):
        qT = qT_ref[hd * MLA_HEAD_PAD:(hd + 1) * MLA_HEAD_PAD, :]
        sT = _dot(keys(j, 2, hd), qT)
        m = m_s[hd]
        m_new = jnp.maximum(m, jnp.max(sT, axis=0, keepdims=True))
        s_s[slot0 * MLA_HEADS + hd] = sT[:MLA_TK]
        s_s[(slot0 + 1) * MLA_HEADS + hd] = sT[MLA_TK:]
        a_s[slot0 * MLA_HEADS + hd] = jnp.exp2(m - m_new)
        a_s[(slot0 + 1) * MLA_HEADS + hd] = jnp.ones((1, MLA_TQ), F32)
        m_s[hd] = m_new

    def value_unit(j, slot, hd):
        jv = jnp.where(j < 0, n_full + 2 + j, j)
        p = jnp.exp2(s_s[slot * MLA_HEADS + hd] - m_s[hd]).astype(BF16)
        lhs = jnp.concatenate([vT_ref[jv, hd * V_DIM:(hd + 1) * V_DIM, :], ones], axis=0)
        pv = _dot(lhs, p)
        a = a_s[slot * MLA_HEADS + hd]
        acc_s[hd] = a * acc_s[hd] + pv[:V_DIM]
        l_s[hd] = a * l_s[hd] + pv[V_DIM:V_DIM + 1]

    def full_pairs(base, pairs):
        for u in range(0, 2 * pairs, 2):
            for hd in range(MLA_HEADS):
                value_unit(base + u - 2, (u - 2) % MLA_SLOTS, hd)
                value_unit(base + u - 1, (u - 1) % MLA_SLOTS, hd)
                pair_score(base + u, u % MLA_SLOTS, hd)

    def steady(i, carry):
        full_pairs(i * MLA_SLOTS, MLA_SLOTS // 2)
        return carry

    for hd in range(MLA_HEADS):
        diag_score(0, hd)
        diag_score(1, hd)
    n_trips = n_full // MLA_SLOTS
    lax.fori_loop(0, n_trips, steady, 0)

    @pl.when(n_full - n_trips * MLA_SLOTS == 2)
    def _():
        full_pairs(n_full - 2, 1)

    for hd in range(MLA_HEADS):
        value_unit(n_full - 2, (n_full - 2) & (MLA_SLOTS - 1), hd)
        value_unit(n_full - 1, (n_full - 1) & (MLA_SLOTS - 1), hd)
    for hd in range(MLA_HEADS):
        oT_ref[hd * V_DIM:(hd + 1) * V_DIM, :] = (acc_s[hd] * (1.0 / l_s[hd])).astype(BF16)


def _mla_attn(qT, kn, kr, vT3, batch, seq):
    assert MLA_TQ == 2 * MLA_TK and MLA_TK % CHUNK == 0 and MLA_SLOTS == 4
    T = qT.shape[1]
    nq = seq // MLA_TQ
    nkt = seq // MLA_TK
    return pl.pallas_call(
        _mla_attn_kernel,
        grid=(batch, nq),
        in_specs=[pl.BlockSpec((MLA_HEADS * MLA_HEAD_PAD, MLA_TQ), lambda b, i: (0, b * nq + i)),
                  pl.BlockSpec((seq // 2, MLA_HEADS * NOPE_DIM), lambda b, i: (b, 0)),
                  pl.BlockSpec((seq // 2, LANES), lambda b, i: (b, 0)),
                  pl.BlockSpec((nkt, MLA_HEADS * V_DIM, MLA_TK), lambda b, i: (b, 0, 0),
                               pipeline_mode=pl.Buffered(1))],
        out_specs=pl.BlockSpec((MLA_HEADS * V_DIM, MLA_TQ), lambda b, i: (0, b * nq + i)),
        out_shape=jax.ShapeDtypeStruct((MLA_HEADS * V_DIM, T), BF16),
        scratch_shapes=[pltpu.VMEM((MLA_HEADS, 1, MLA_TQ), F32),
                        pltpu.VMEM((MLA_HEADS, 1, MLA_TQ), F32),
                        pltpu.VMEM((MLA_SLOTS * MLA_HEADS, 1, MLA_TQ), F32),
                        pltpu.VMEM((MLA_HEADS, V_DIM, MLA_TQ), F32),
                        pltpu.VMEM((MLA_SLOTS * MLA_HEADS, MLA_TK, MLA_TQ), F32)],
        compiler_params=pltpu.CompilerParams(
            dimension_semantics=("arbitrary", "arbitrary"),
            vmem_limit_bytes=VMEM_LIMIT),
        name="mla_attn",
    )(qT, kn, kr, vT3)


def _block_tail_kernel(oT_ref, x_ref, wo_ref, gpost_ref, gfpre_ref, wg_ref, wu_ref,
                       wd_ref, gfpost_ref, out_ref):
    sub = out_ref.shape[0] // TAIL_SPLIT
    blocks = [slice(i * sub, (i + 1) * sub) for i in range(TAIL_SPLIT)]
    y = [lax.dot_general(oT_ref[:, r], wo_ref[...], (((0,), (0,)), ((), ())),
                         preferred_element_type=F32) for r in blocks]
    x1 = [x_ref[r, :] + _rms(yi, gpost_ref[...]) for r, yi in zip(blocks, y)]
    h = [_rms(xi, gfpre_ref[...]).astype(BF16) for xi in x1]
    gu = [(_dot(hi, wg_ref[...]), _dot(hi, wu_ref[...])) for hi in h]
    a = [(gi * (1.0 / (1.0 + jnp.exp(-gi))) * ui).astype(BF16) for gi, ui in gu]
    f = [_dot(ai, wd_ref[...]) for ai in a]
    for r, xi, fi in zip(blocks, x1, f):
        out_ref[r, :] = xi + _rms(fi, gfpost_ref[...])


def _block_tail(oT, x2, wo, gpost, gfpre, wg, wu, wd, gfpost, layer):
    T, D = x2.shape
    tm = TAIL_ROWS
    const = lambda i: (0, 0)
    full = lambda a: pl.BlockSpec(a.shape, const, pipeline_mode=pl.Buffered(1))
    of_layer = lambda a: pl.BlockSpec((None,) + a.shape[1:], lambda i: (layer, 0, 0),
                                      pipeline_mode=pl.Buffered(1))
    row = lambda a: pl.BlockSpec((tm, a.shape[1]), lambda i: (i, 0))
    return pl.pallas_call(
        _block_tail_kernel,
        grid=(T // tm,),
        in_specs=[pl.BlockSpec((oT.shape[0], tm), lambda i: (0, i)), row(x2), full(wo),
                  full(gpost), full(gfpre), of_layer(wg), of_layer(wu), of_layer(wd),
                  full(gfpost)],
        out_specs=pl.BlockSpec((tm, D), lambda i: (i, 0)),
        out_shape=jax.ShapeDtypeStruct((T, D), F32),
        compiler_params=pltpu.CompilerParams(
            dimension_semantics=("arbitrary",), vmem_limit_bytes=VMEM_LIMIT),
        name="block_tail",
    )(oT, x2, wo, gpost, gfpre, wg, wu, wd, gfpost)


def _kvq_proj_kernel(x_ref, gsrc_ref, gpre_ref, wqT_ref, wk_ref, wvT_ref, tab_ref,
                     qT_ref, k_ref, vT_ref, bias_ref, g_s, *, scale):
    i = pl.program_id(0)

    @pl.when(i == 0)
    def _():
        _rel_bias_gather(tab_ref[...], g_s)

    blocks = bias_ref.shape[1] // SUBLANES
    for local in range(blocks):
        _rel_bias_rows(g_s, bias_ref, i * blocks + local, local)
    x = x_ref[...]
    xr = x * lax.rsqrt(jnp.mean(x * x, axis=-1, keepdims=True) + EPS)
    xs = (xr * gsrc_ref[...]).astype(BF16)
    xq = (xr * gpre_ref[...]).astype(BF16)
    qT_ref[...] = (_dot_nt(wqT_ref[...], xq) * scale).astype(BF16)
    k_ref[...] = pltpu.bitcast(_dot(xs, wk_ref[...]).astype(BF16), jnp.uint32)
    vT_ref[...] = _dot_nt(wvT_ref[...], xs).astype(BF16)


def _kvq_proj(x2, gsrc, gpre, wqT, wk, wvT, table):
    T, D = x2.shape
    tm = PROJ_ROWS
    steps = T // tm
    hd = B_HEADS * B_HEAD_DIM
    assert QUAD_BAND % (steps * SUBLANES) == 0
    const = lambda i: (0, 0)
    full = lambda a: pl.BlockSpec(a.shape, const)
    scale = float(B_HEAD_DIM ** -0.5 * math.log2(math.e))
    return pl.pallas_call(
        functools.partial(_kvq_proj_kernel, scale=scale),
        grid=(steps,),
        in_specs=[pl.BlockSpec((tm, D), lambda i: (i, 0)),
                  full(gsrc), full(gpre), full(wqT), full(wk), full(wvT), full(table)],
        out_specs=[pl.BlockSpec((hd, tm), lambda i: (0, i)),
                   pl.BlockSpec((tm // 2, hd), lambda i: (i, 0)),
                   pl.BlockSpec((hd, tm), lambda i: (0, i)),
                   pl.BlockSpec((B_HEADS, QUAD_BAND // steps, QUAD), lambda i: (0, i, 0))],
        out_shape=[jax.ShapeDtypeStruct((hd, T), BF16),
                   jax.ShapeDtypeStruct((T // 2, hd), jnp.uint32),
                   jax.ShapeDtypeStruct((hd, T), BF16),
                   jax.ShapeDtypeStruct((B_HEADS, QUAD_BAND, QUAD), F32)],
        scratch_shapes=[pltpu.VMEM((B_HEADS, BIAS_ROLL), F32)],
        compiler_params=pltpu.CompilerParams(
            dimension_semantics=("arbitrary",), vmem_limit_bytes=VMEM_LIMIT),
        name="kvq_proj",
    )(x2, gsrc, gpre, wqT, wk, wvT, table)


def _rel_bias_gather(tab, g_s):
    t_hi = tab.astype(BF16)
    rem = tab - t_hi.astype(F32)
    t_mid = rem.astype(BF16)
    t_lo = (rem - t_mid.astype(F32)).astype(BF16)
    row = lax.broadcasted_iota(jnp.int32, (REL_TABLE, BIAS_ROLL), 0)
    lane = lax.broadcasted_iota(jnp.int32, (REL_TABLE, BIAS_ROLL), 1)
    dist = jnp.where(lane < QUAD, lane, lane - BIAS_ROLL) + LEFT_CHUNKS * CHUNK
    idx = jnp.clip(dist, -(CHUNK - 1), MAX_REL) + (CHUNK - 1)
    onehot = jnp.where(idx == row, 1.0, 0.0).astype(BF16)
    g_s[...] = (_dot(t_hi, onehot) + _dot(t_mid, onehot) + _dot(t_lo, onehot)) * math.log2(math.e)


def _rel_bias_rows(g_s, out_ref, cb, local):
    c0 = cb * SUBLANES
    q_chunk = lax.broadcasted_iota(jnp.int32, (1, QUAD), 1) // CHUNK + LEFT_CHUNKS
    back = q_chunk - cb // (CHUNK // SUBLANES)
    valid = (back >= 0) & (back <= LEFT_CHUNKS)
    for h in range(B_HEADS):
        rows = jnp.broadcast_to(g_s[h:h + 1, :], (SUBLANES, BIAS_ROLL))
        rolled = pltpu.roll(rows, c0, 1, stride=1, stride_axis=0)
        out_ref[h, local * SUBLANES:(local + 1) * SUBLANES, :] = jnp.where(
            valid, rolled[:, :QUAD], NEG)


def _chunk_attn_kernel(qT_ref, kp_ref, kc_ref, vTp_ref, vTc_ref, bias_ref, o_ref,
                       s_s, m_s, oT_s):
    g = pl.program_id(1)
    slab = HEAD_GROUP * B_HEAD_DIM
    ones = jnp.ones((SUM_ROWS, QUAD_BAND), BF16)
    no_q = jnp.zeros((B_HEAD_DIM, QUAD), BF16)
    n_units = (GROUP // QUAD) * B_HEADS

    def score_unit(u, first_group):
        qd, h = divmod(u, B_HEADS)
        lanes = slice(h // HEAD_GROUP * slab, (h // HEAD_GROUP + 1) * slab)
        kb = pltpu.bitcast(jnp.concatenate(
            [kp_ref[qd * (QUAD // 2):, lanes], kc_ref[:(qd + 1) * (QUAD // 2), lanes]], axis=0),
            BF16)
        q_own = qT_ref[h * B_HEAD_DIM:(h + 1) * B_HEAD_DIM, qd * QUAD:(qd + 1) * QUAD]
        pos = h % HEAD_GROUP
        q_slab = jnp.concatenate([no_q] * pos + [q_own] + [no_q] * (HEAD_GROUP - 1 - pos), axis=0)
        s = _dot(kb, q_slab) + bias_ref[h]
        if first_group:
            key = lax.broadcasted_iota(jnp.int32, (QUAD_BAND, 1), 0)
            s = jnp.where(key >= GROUP - qd * QUAD, s, NEG)
        slot = u % (UNIT_SKEW + 1)
        s_s[slot] = s
        m_s[slot] = jnp.max(s, axis=0, keepdims=True)

    def value_unit(u):
        qd, h = divmod(u, B_HEADS)
        slot = u % (UNIT_SKEW + 1)
        rows = slice(h * B_HEAD_DIM, (h + 1) * B_HEAD_DIM)
        p = jnp.exp2(s_s[slot] - m_s[slot]).astype(BF16)
        v_band = jnp.concatenate(
            [vTp_ref[rows, qd * QUAD:], vTc_ref[rows, :(qd + 1) * QUAD]], axis=1)
        r = _dot(jnp.concatenate([v_band, ones], axis=0), p)
        oT_s[qd, rows, :] = r[:B_HEAD_DIM] * (1.0 / r[B_HEAD_DIM:B_HEAD_DIM + 1])

    def run(first_group):
        for u in range(n_units + UNIT_SKEW):
            if u >= UNIT_SKEW:
                value_unit(u - UNIT_SKEW)
            if u < n_units:
                score_unit(u, first_group)

    pl.when(g == 0)(lambda: run(True))
    pl.when(g != 0)(lambda: run(False))
    for qd in range(GROUP // QUAD):
        o_ref[:, qd * QUAD:(qd + 1) * QUAD] = oT_s[qd].astype(BF16)


def _chunk_attn(qT, k, vT, bias, batch, seq):
    hd, T = qT.shape
    ng = seq // GROUP
    n_quad = GROUP // QUAD
    prev = lambda b, g: b * ng + jnp.maximum(g - 1, 0)
    cur = lambda b, g: b * ng + g
    return pl.pallas_call(
        _chunk_attn_kernel,
        grid=(batch, ng),
        in_specs=[pl.BlockSpec((hd, GROUP), lambda b, g: (0, cur(b, g))),
                  pl.BlockSpec((GROUP // 2, hd), lambda b, g: (prev(b, g), 0)),
                  pl.BlockSpec((GROUP // 2, hd), lambda b, g: (cur(b, g), 0)),
                  pl.BlockSpec((hd, GROUP), lambda b, g: (0, prev(b, g))),
                  pl.BlockSpec((hd, GROUP), lambda b, g: (0, cur(b, g))),
                  pl.BlockSpec(bias.shape, lambda b, g: (0, 0, 0),
                               pipeline_mode=pl.Buffered(1))],
        out_specs=pl.BlockSpec((hd, GROUP), lambda b, g: (0, cur(b, g))),
        out_shape=jax.ShapeDtypeStruct((hd, T), BF16),
        scratch_shapes=[pltpu.VMEM((UNIT_SKEW + 1, QUAD_BAND, QUAD), F32),
                        pltpu.VMEM((UNIT_SKEW + 1, 1, QUAD), F32),
                        pltpu.VMEM((n_quad, hd, QUAD), F32)],
        compiler_params=pltpu.CompilerParams(
            dimension_semantics=("arbitrary", "arbitrary"), vmem_limit_bytes=VMEM_LIMIT),
        name="chunk_attn",
    )(qT, k, k, vT, vT, bias)


def kernel(x, positions, attn_pre_g, attn_post_g, ffn_pre_g, ffn_post_g, ffn_w_gate,
           ffn_w_up, ffn_w_down, mla_w_a, mla_g_q, mla_w_uq, mla_g_kv, mla_w_ukv, mla_w_o,
           kv_src_g, w_kv_shared, b_w_q, b_rel_table, b_w_o):
    batch, seq, d = x.shape
    T = batch * seq
    assert attn_pre_g.shape[0] == 2 and mla_w_a.shape[0] == 1 and b_w_q.shape[0] == 1
    assert seq % GROUP == 0 and seq % MLA_TQ == 0 and T % PROJ_ROWS == 0
    row = lambda g: g.reshape(1, -1).astype(F32)

    x2 = x.reshape(T, d)
    pos_row = positions.reshape(1, T).astype(F32)
    inv = 1.0 / (ROPE_THETA ** (jnp.arange(0, ROPE_DIM, 2, dtype=F32) / ROPE_DIM))
    inv_col = inv.reshape(ROPE_DIM // 2, 1)

    wa = mla_w_a[0][:, :Q_LORA + KV_LORA].astype(BF16)
    wkrT = jnp.pad(mla_w_a[0][:, Q_LORA + KV_LORA:].T, ((0, LANES - ROPE_DIM), (0, 0))).astype(BF16)
    wuq = mla_w_uq[0].reshape(Q_LORA, MLA_HEADS, NOPE_DIM + ROPE_DIM)
    wuq = jnp.pad(wuq, ((0, 0), (0, 0), (0, MLA_HEAD_PAD - NOPE_DIM - ROPE_DIM)))
    wuqT = wuq.reshape(Q_LORA, MLA_HEADS * MLA_HEAD_PAD).T.astype(BF16)
    wukv = mla_w_ukv[0].reshape(KV_LORA, MLA_HEADS, NOPE_DIM + V_DIM)
    wuk = wukv[:, :, :NOPE_DIM].reshape(KV_LORA, MLA_HEADS * NOPE_DIM).astype(BF16)
    wuvT = wukv[:, :, NOPE_DIM:].reshape(KV_LORA, MLA_HEADS * V_DIM).T.astype(BF16)
    hd = B_HEADS * B_HEAD_DIM
    wk = w_kv_shared[:, :hd].astype(BF16)
    wvT = w_kv_shared[:, hd:].T.astype(BF16)
    wqT = b_w_q[0].T.astype(BF16)

    qT0, kn0, kr0, vT0 = _mla_proj(x2, pos_row, inv_col, row(attn_pre_g[0]), wa, wkrT,
                                   row(mla_g_q[0]), wuqT, row(mla_g_kv[0]), wuk, wuvT)
    o0 = _mla_attn(qT0, kn0, kr0, vT0, batch, seq)
    wg, wu, wd = ffn_w_gate.astype(BF16), ffn_w_up.astype(BF16), ffn_w_down.astype(BF16)
    x2 = _block_tail(o0, x2, mla_w_o[0].astype(BF16), row(attn_post_g[0]), row(ffn_pre_g[0]),
                     wg, wu, wd, row(ffn_post_g[0]), 0)

    qT1, k1, vT1, bias = _kvq_proj(x2, row(kv_src_g), row(attn_pre_g[1]), wqT, wk, wvT,
                                   b_rel_table[0])
    o1 = _chunk_attn(qT1, k1, vT1, bias, batch, seq)
    x2 = _block_tail(o1, x2, b_w_o[0].astype(BF16), row(attn_post_g[1]), row(ffn_pre_g[1]),
                     wg, wu, wd, row(ffn_post_g[1]), 1)
    return x2.reshape(batch, seq, d)
```

```python
import functools
import math

import jax
import jax.numpy as jnp
from jax import lax
from jax.experimental import pallas as pl
from jax.experimental.pallas import tpu as pltpu

F32 = jnp.float32
BF16 = jnp.bfloat16

CHUNK = 64
MLA_HEADS = 8
Q_LORA = 384
KV_LORA = 256
NOPE_DIM = 128
ROPE_DIM = 64
V_DIM = 128
ROPE_THETA = 10000.0
B_HEADS = 16
B_HEAD_DIM = 64
LEFT_CHUNKS = 8
BAND = (LEFT_CHUNKS + 1) * CHUNK
MAX_REL = 256
REL_TABLE = MAX_REL + CHUNK
EPS = 1e-6

LANES = 128
SUBLANES = 8
MLA_HEAD_PAD = 2 * LANES
NEG = -1e30

PROJ_ROWS = 1024
TAIL_ROWS = 512
TAIL_SPLIT = 2
MLA_TQ = 512
MLA_TK = 256
MLA_SLOTS = 4
SUM_ROWS = 16
QUAD = 4 * CHUNK
QUAD_BAND = BAND + 3 * CHUNK
GROUP = 2 * QUAD
HEAD_GROUP = 2
UNIT_SKEW = 5
BIAS_ROLL = 1024
V7X_VMEM_BYTES = 64 * 1024 * 1024
VMEM_LIMIT = V7X_VMEM_BYTES - 8 * 1024 * 1024


def _rms(x, g):
    ms = jnp.mean(x * x, axis=-1, keepdims=True)
    return x * lax.rsqrt(ms + EPS) * g


def _dot(a, b):
    return jnp.dot(a, b, preferred_element_type=F32)


def _dot_nt(a, b):
    return lax.dot_general(a, b, (((1,), (1,)), ((), ())), preferred_element_type=F32)


def _mla_proj_kernel(x_ref, posr_ref, invc_ref, gpre_ref, wa_ref, wkrT_ref, gq_ref,
                     wuqT_ref, gkv_ref, wuk_ref, wuvT_ref, qT_ref, kn_ref, kr_ref, vT_ref, *,
                     scale):
    h = _rms(x_ref[...], gpre_ref[...]).astype(BF16)
    a = _dot(h, wa_ref[...])
    cqn = _rms(a[:, :Q_LORA], gq_ref[...]).astype(BF16)
    ckvn = _rms(a[:, Q_LORA:], gkv_ref[...]).astype(BF16)
    half = ROPE_DIM // 2

    angT = invc_ref[...] * posr_ref[...]
    cosT, sinT = jnp.cos(angT), jnp.sin(angT)

    krT = _dot_nt(wkrT_ref[...], h)
    k1, k2 = krT[:half], krT[half:ROPE_DIM]
    kr_rotT = jnp.concatenate(
        [k1 * cosT - k2 * sinT, k2 * cosT + k1 * sinT,
         jnp.zeros((LANES - ROPE_DIM, krT.shape[1]), F32)], axis=0)
    kr_ref[...] = pltpu.bitcast(kr_rotT.T.astype(BF16), jnp.uint32)
    kn_ref[...] = pltpu.bitcast(_dot(ckvn, wuk_ref[...]).astype(BF16), jnp.uint32)

    cosT, sinT = cosT * scale, sinT * scale
    qT = _dot_nt(wuqT_ref[...], cqn)
    for hd in range(MLA_HEADS):
        lo = hd * MLA_HEAD_PAD
        t1 = qT[lo + NOPE_DIM:lo + NOPE_DIM + half]
        t2 = qT[lo + NOPE_DIM + half:lo + NOPE_DIM + ROPE_DIM]
        qT_ref[lo:lo + NOPE_DIM] = (qT[lo:lo + NOPE_DIM] * scale).astype(BF16)
        qT_ref[lo + NOPE_DIM:lo + NOPE_DIM + half] = (t1 * cosT - t2 * sinT).astype(BF16)
        qT_ref[lo + NOPE_DIM + half:lo + NOPE_DIM + ROPE_DIM] = (t2 * cosT + t1 * sinT).astype(BF16)
        qT_ref[lo + NOPE_DIM + ROPE_DIM:lo + MLA_HEAD_PAD] = jnp.zeros(
            (MLA_HEAD_PAD - NOPE_DIM - ROPE_DIM, qT.shape[1]), BF16)

    vT = _dot_nt(wuvT_ref[...], ckvn).astype(BF16)
    for t in range(vT_ref.shape[0]):
        vT_ref[t] = vT[:, t * MLA_TK:(t + 1) * MLA_TK]


def _mla_proj(x2, pos_row, inv_col, gpre, wa, wkrT, gq, wuqT, gkv, wuk, wuvT):
    T, D = x2.shape
    tm = PROJ_ROWS
    const = lambda i: (0, 0)
    full = lambda a: pl.BlockSpec(a.shape, const)
    scale = float((NOPE_DIM + ROPE_DIM) ** -0.5 * math.log2(math.e))
    return pl.pallas_call(
        functools.partial(_mla_proj_kernel, scale=scale),
        grid=(T // tm,),
        in_specs=[pl.BlockSpec((tm, D), lambda i: (i, 0)),
                  pl.BlockSpec((1, tm), lambda i: (0, i)),
                  full(inv_col), full(gpre), full(wa), full(wkrT), full(gq), full(wuqT),
                  full(gkv), full(wuk), full(wuvT)],
        out_specs=[pl.BlockSpec((MLA_HEADS * MLA_HEAD_PAD, tm), lambda i: (0, i)),
                   pl.BlockSpec((tm // 2, MLA_HEADS * NOPE_DIM), lambda i: (i, 0)),
                   pl.BlockSpec((tm // 2, LANES), lambda i: (i, 0)),
                   pl.BlockSpec((tm // MLA_TK, MLA_HEADS * V_DIM, MLA_TK), lambda i: (i, 0, 0))],
        out_shape=[jax.ShapeDtypeStruct((MLA_HEADS * MLA_HEAD_PAD, T), BF16),
                   jax.ShapeDtypeStruct((T // 2, MLA_HEADS * NOPE_DIM), jnp.uint32),
                   jax.ShapeDtypeStruct((T // 2, LANES), jnp.uint32),
                   jax.ShapeDtypeStruct((T // MLA_TK, MLA_HEADS * V_DIM, MLA_TK), BF16)],
        compiler_params=pltpu.CompilerParams(
            dimension_semantics=("arbitrary",), vmem_limit_bytes=VMEM_LIMIT),
        name="mla_proj",
    )(x2, pos_row, inv_col, gpre, wa, wkrT, gq, wuqT, gkv, wuk, wuvT)


def _mla_attn_kernel(qT_ref, kn_ref, kr_ref, vT_ref, oT_ref, m_s, l_s, a_s, acc_s, s_s):
    qi = pl.program_id(1)
    n_full = qi * (MLA_TQ // MLA_TK)
    m_s[...] = jnp.full(m_s.shape, NEG, F32)
    l_s[...] = jnp.zeros(l_s.shape, F32)
    a_s[...] = jnp.ones(a_s.shape, F32)
    acc_s[...] = jnp.zeros(acc_s.shape, F32)
    ones = jnp.ones((SUM_ROWS, MLA_TK), BF16)
    half = MLA_TK // 2

    def keys(j, tiles, hd):
        rows = pl.ds(pl.multiple_of(j * half, half), tiles * half)
        k_nope = pltpu.bitcast(kn_ref[rows, hd * NOPE_DIM:(hd + 1) * NOPE_DIM], BF16)
        k_rope = pltpu.bitcast(kr_ref[rows, :], BF16)
        return jnp.concatenate([k_nope, k_rope], axis=1)

    def diag_score(d, hd):
        slot = (d - 2) % MLA_SLOTS * MLA_HEADS + hd
        q0 = d * MLA_TK
        qT = qT_ref[hd * MLA_HEAD_PAD:(hd + 1) * MLA_HEAD_PAD, q0:]
        sT = _dot(keys(n_full + d, 1, hd), qT)
        kc = lax.broadcasted_iota(jnp.int32, sT.shape, 0) // CHUNK
        qc = lax.broadcasted_iota(jnp.int32, sT.shape, 1) // CHUNK
        sT = jnp.where(kc <= qc, sT, NEG)
        s_s[slot, :, q0:] = sT
        m_s[hd, :, q0:] = jnp.maximum(m_s[hd, :, q0:], jnp.max(sT, axis=0, keepdims=True))
        if q0:
            s_s[slot, :, :q0] = jnp.full((MLA_TK, q0), -jnp.inf, F32)

    def pair_score(j, slot0, hd):
        qT = qT_ref[hd * MLA_HEAD_PAD:(hd + 1) * MLA_HEAD_PAD, :]
        sT = _dot(keys(j, 2, hd), qT)
        m = m_s[hd]
        m_new = jnp.maximum(m, jnp.max(sT, axis=0, keepdims=True))
        s_s[slot0 * MLA_HEADS + hd] = sT[:MLA_TK]
        s_s[(slot0 + 1) * MLA_HEADS + hd] = sT[MLA_TK:]
        a_s[slot0 * MLA_HEADS + hd] = jnp.exp2(m - m_new)
        a_s[(slot0 + 1) * MLA_HEADS + hd] = jnp.ones((1, MLA_TQ), F32)
        m_s[hd] = m_new

    def value_unit(j, slot, hd):
        jv = jnp.where(j < 0, n_full + 2 + j, j)
        p = jnp.exp2(s_s[slot * MLA_HEADS + hd] - m_s[hd]).astype(BF16)
        lhs = jnp.concatenate([vT_ref[jv, hd * V_DIM:(hd + 1) * V_DIM, :], ones], axis=0)
        pv = _dot(lhs, p)
        a = a_s[slot * MLA_HEADS + hd]
        acc_s[hd] = a * acc_s[hd] + pv[:V_DIM]
        l_s[hd] = a * l_s[hd] + pv[V_DIM:V_DIM + 1]

    def full_pairs(base, pairs):
        for u in range(0, 2 * pairs, 2):
            for hd in range(MLA_HEADS):
                value_unit(base + u - 2, (u - 2) % MLA_SLOTS, hd)
                value_unit(base + u - 1, (u - 1) % MLA_SLOTS, hd)
                pair_score(base + u, u % MLA_SLOTS, hd)

    def steady(i, carry):
        full_pairs(i * MLA_SLOTS, MLA_SLOTS // 2)
        return carry

    for hd in range(MLA_HEADS):
        diag_score(0, hd)
        diag_score(1, hd)
    n_trips = n_full // MLA_SLOTS
    lax.fori_loop(0, n_trips, steady, 0)

    @pl.when(n_full - n_trips * MLA_SLOTS == 2)
    def _():
        full_pairs(n_full - 2, 1)

    for hd in range(MLA_HEADS):
        value_unit(n_full - 2, (n_full - 2) & (MLA_SLOTS - 1), hd)
        value_unit(n_full - 1, (n_full - 1) & (MLA_SLOTS - 1), hd)
    for hd in range(MLA_HEADS):
        oT_ref[hd * V_DIM:(hd + 1) * V_DIM, :] = (acc_s[hd] * (1.0 / l_s[hd])).astype(BF16)


def _mla_attn(qT, kn, kr, vT3, batch, seq):
    assert MLA_TQ == 2 * MLA_TK and MLA_TK % CHUNK == 0 and MLA_SLOTS == 4
    T = qT.shape[1]
    nq = seq // MLA_TQ
    nkt = seq // MLA_TK
    return pl.pallas_call(
        _mla_attn_kernel,
        grid=(batch, nq),
        in_specs=[pl.BlockSpec((MLA_HEADS * MLA_HEAD_PAD, MLA_TQ), lambda b, i: (0, b * nq + i)),
                  pl.BlockSpec((seq // 2, MLA_HEADS * NOPE_DIM), lambda b, i: (b, 0)),
                  pl.BlockSpec((seq // 2, LANES), lambda b, i: (b, 0)),
                  pl.BlockSpec((nkt, MLA_HEADS * V_DIM, MLA_TK), lambda b, i: (b, 0, 0),
                               pipeline_mode=pl.Buffered(1))],
        out_specs=pl.BlockSpec((MLA_HEADS * V_DIM, MLA_TQ), lambda b, i: (0, b * nq + i)),
        out_shape=jax.ShapeDtypeStruct((MLA_HEADS * V_DIM, T), BF16),
        scratch_shapes=[pltpu.VMEM((MLA_HEADS, 1, MLA_TQ), F32),
                        pltpu.VMEM((MLA_HEADS, 1, MLA_TQ), F32),
                        pltpu.VMEM((MLA_SLOTS * MLA_HEADS, 1, MLA_TQ), F32),
                        pltpu.VMEM((MLA_HEADS, V_DIM, MLA_TQ), F32),
                        pltpu.VMEM((MLA_SLOTS * MLA_HEADS, MLA_TK, MLA_TQ), F32)],
        compiler_params=pltpu.CompilerParams(
            dimension_semantics=("arbitrary", "arbitrary"),
            vmem_limit_bytes=VMEM_LIMIT),
        name="mla_attn",
    )(qT, kn, kr, vT3)


def _block_tail_kernel(oT_ref, x_ref, wo_ref, gpost_ref, gfpre_ref, wg_ref, wu_ref,
                       wd_ref, gfpost_ref, out_ref):
    sub = out_ref.shape[0] // TAIL_SPLIT
    blocks = [slice(i * sub, (i + 1) * sub) for i in range(TAIL_SPLIT)]
    y = [lax.dot_general(oT_ref[:, r], wo_ref[...], (((0,), (0,)), ((), ())),
                         preferred_element_type=F32) for r in blocks]
    x1 = [x_ref[r, :] + _rms(yi, gpost_ref[...]) for r, yi in zip(blocks, y)]
    h = [_rms(xi, gfpre_ref[...]).astype(BF16) for xi in x1]
    gu = [(_dot(hi, wg_ref[...]), _dot(hi, wu_ref[...])) for hi in h]
    a = [(gi * (1.0 / (1.0 + jnp.exp(-gi))) * ui).astype(BF16) for gi, ui in gu]
    f = [_dot(ai, wd_ref[...]) for ai in a]
    for r, xi, fi in zip(blocks, x1, f):
        out_ref[r, :] = xi + _rms(fi, gfpost_ref[...])


def _block_tail(oT, x2, wo, gpost, gfpre, wg, wu, wd, gfpost, layer):
    T, D = x2.shape
    tm = TAIL_ROWS
    const = lambda i: (0, 0)
    full = lambda a: pl.BlockSpec(a.shape, const, pipeline_mode=pl.Buffered(1))
    of_layer = lambda a: pl.BlockSpec((None,) + a.shape[1:], lambda i: (layer, 0, 0),
                                      pipeline_mode=pl.Buffered(1))
    row = lambda a: pl.BlockSpec((tm, a.shape[1]), lambda i: (i, 0))
    return pl.pallas_call(
        _block_tail_kernel,
        grid=(T // tm,),
        in_specs=[pl.BlockSpec((oT.shape[0], tm), lambda i: (0, i)), row(x2), full(wo),
                  full(gpost), full(gfpre), of_layer(wg), of_layer(wu), of_layer(wd),
                  full(gfpost)],
        out_specs=pl.BlockSpec((tm, D), lambda i: (i, 0)),
        out_shape=jax.ShapeDtypeStruct((T, D), F32),
        compiler_params=pltpu.CompilerParams(
            dimension_semantics=("arbitrary",), vmem_limit_bytes=VMEM_LIMIT),
        name="block_tail",
    )(oT, x2, wo, gpost, gfpre, wg, wu, wd, gfpost)


def _kvq_proj_kernel(x_ref, gsrc_ref, gpre_ref, wqT_ref, wk_ref, wvT_ref, tab_ref,
                     qT_ref, k_ref, vT_ref, bias_ref, g_s, *, scale):
    i = pl.program_id(0)

    @pl.when(i == 0)
    def _():
        _rel_bias_gather(tab_ref[...], g_s)

    blocks = bias_ref.shape[1] // SUBLANES
    for local in range(blocks):
        _rel_bias_rows(g_s, bias_ref, i * blocks + local, local)
    x = x_ref[...]
    xr = x * lax.rsqrt(jnp.mean(x * x, axis=-1, keepdims=True) + EPS)
    xs = (xr * gsrc_ref[...]).astype(BF16)
    xq = (xr * gpre_ref[...]).astype(BF16)
    qT_ref[...] = (_dot_nt(wqT_ref[...], xq) * scale).astype(BF16)
    k_ref[...] = pltpu.bitcast(_dot(xs, wk_ref[...]).astype(BF16), jnp.uint32)
    vT_ref[...] = _dot_nt(wvT_ref[...], xs).astype(BF16)


def _kvq_proj(x2, gsrc, gpre, wqT, wk, wvT, table):
    T, D = x2.shape
    tm = PROJ_ROWS
    steps = T // tm
    hd = B_HEADS * B_HEAD_DIM
    assert QUAD_BAND % (steps * SUBLANES) == 0
    const = lambda i: (0, 0)
    full = lambda a: pl.BlockSpec(a.shape, const)
    scale = float(B_HEAD_DIM ** -0.5 * math.log2(math.e))
    return pl.pallas_call(
        functools.partial(_kvq_proj_kernel, scale=scale),
        grid=(steps,),
        in_specs=[pl.BlockSpec((tm, D), lambda i: (i, 0)),
                  full(gsrc), full(gpre), full(wqT), full(wk), full(wvT), full(table)],
        out_specs=[pl.BlockSpec((hd, tm), lambda i: (0, i)),
                   pl.BlockSpec((tm // 2, hd), lambda i: (i, 0)),
                   pl.BlockSpec((hd, tm), lambda i: (0, i)),
                   pl.BlockSpec((B_HEADS, QUAD_BAND // steps, QUAD), lambda i: (0, i, 0))],
        out_shape=[jax.ShapeDtypeStruct((hd, T), BF16),
                   jax.ShapeDtypeStruct((T // 2, hd), jnp.uint32),
                   jax.ShapeDtypeStruct((hd, T), BF16),
                   jax.ShapeDtypeStruct((B_HEADS, QUAD_BAND, QUAD), F32)],
        scratch_shapes=[pltpu.VMEM((B_HEADS, BIAS_ROLL), F32)],
        compiler_params=pltpu.CompilerParams(
            dimension_semantics=("arbitrary",), vmem_limit_bytes=VMEM_LIMIT),
        name="kvq_proj",
    )(x2, gsrc, gpre, wqT, wk, wvT, table)


def _rel_bias_gather(tab, g_s):
    t_hi = tab.astype(BF16)
    rem = tab - t_hi.astype(F32)
    t_mid = rem.astype(BF16)
    t_lo = (rem - t_mid.astype(F32)).astype(BF16)
    row = lax.broadcasted_iota(jnp.int32, (REL_TABLE, BIAS_ROLL), 0)
    lane = lax.broadcasted_iota(jnp.int32, (REL_TABLE, BIAS_ROLL), 1)
    dist = jnp.where(lane < QUAD, lane, lane - BIAS_ROLL) + LEFT_CHUNKS * CHUNK
    idx = jnp.clip(dist, -(CHUNK - 1), MAX_REL) + (CHUNK - 1)
    onehot = jnp.where(idx == row, 1.0, 0.0).astype(BF16)
    g_s[...] = (_dot(t_hi, onehot) + _dot(t_mid, onehot) + _dot(t_lo, onehot)) * math.log2(math.e)


def _rel_bias_rows(g_s, out_ref, cb, local):
    c0 = cb * SUBLANES
    q_chunk = lax.broadcasted_iota(jnp.int32, (1, QUAD), 1) // CHUNK + LEFT_CHUNKS
    back = q_chunk - cb // (CHUNK // SUBLANES)
    valid = (back >= 0) & (back <= LEFT_CHUNKS)
    for h in range(B_HEADS):
        rows = jnp.broadcast_to(g_s[h:h + 1, :], (SUBLANES, BIAS_ROLL))
        rolled = pltpu.roll(rows, c0, 1, stride=1, stride_axis=0)
        out_ref[h, local * SUBLANES:(local + 1) * SUBLANES, :] = jnp.where(
            valid, rolled[:, :QUAD], NEG)


def _chunk_attn_kernel(qT_ref, kp_ref, kc_ref, vTp_ref, vTc_ref, bias_ref, o_ref,
                       s_s, m_s, oT_s):
    g = pl.program_id(1)
    slab = HEAD_GROUP * B_HEAD_DIM
    ones = jnp.ones((SUM_ROWS, QUAD_BAND), BF16)
    no_q = jnp.zeros((B_HEAD_DIM, QUAD), BF16)
    n_units = (GROUP // QUAD) * B_HEADS

    def score_unit(u, first_group):
        qd, h = divmod(u, B_HEADS)
        lanes = slice(h // HEAD_GROUP * slab, (h // HEAD_GROUP + 1) * slab)
        kb = pltpu.bitcast(jnp.concatenate(
            [kp_ref[qd * (QUAD // 2):, lanes], kc_ref[:(qd + 1) * (QUAD // 2), lanes]], axis=0),
            BF16)
        q_own = qT_ref[h * B_HEAD_DIM:(h + 1) * B_HEAD_DIM, qd * QUAD:(qd + 1) * QUAD]
        pos = h % HEAD_GROUP
        q_slab = jnp.concatenate([no_q] * pos + [q_own] + [no_q] * (HEAD_GROUP - 1 - pos), axis=0)
        s = _dot(kb, q_slab) + bias_ref[h]
        if first_group:
            key = lax.broadcasted_iota(jnp.int32, (QUAD_BAND, 1), 0)
            s = jnp.where(key >= GROUP - qd * QUAD, s, NEG)
        slot = u % (UNIT_SKEW + 1)
        s_s[slot] = s
        m_s[slot] = jnp.max(s, axis=0, keepdims=True)

    def value_unit(u):
        qd, h = divmod(u, B_HEADS)
        slot = u % (UNIT_SKEW + 1)
        rows = slice(h * B_HEAD_DIM, (h + 1) * B_HEAD_DIM)
        p = jnp.exp2(s_s[slot] - m_s[slot]).astype(BF16)
        v_band = jnp.concatenate(
            [vTp_ref[rows, qd * QUAD:], vTc_ref[rows, :(qd + 1) * QUAD]], axis=1)
        r = _dot(jnp.concatenate([v_band, ones], axis=0), p)
        oT_s[qd, rows, :] = r[:B_HEAD_DIM] * (1.0 / r[B_HEAD_DIM:B_HEAD_DIM + 1])

    def run(first_group):
        for u in range(n_units + UNIT_SKEW):
            if u >= UNIT_SKEW:
                value_unit(u - UNIT_SKEW)
            if u < n_units:
                score_unit(u, first_group)

    pl.when(g == 0)(lambda: run(True))
    pl.when(g != 0)(lambda: run(False))
    for qd in range(GROUP // QUAD):
        o_ref[:, qd * QUAD:(qd + 1) * QUAD] = oT_s[qd].astype(BF16)


def _chunk_attn(qT, k, vT, bias, batch, seq):
    hd, T = qT.shape
    ng = seq // GROUP
    n_quad = GROUP // QUAD
    prev = lambda b, g: b * ng + jnp.maximum(g - 1, 0)
    cur = lambda b, g: b * ng + g
    return pl.pallas_call(
        _chunk_attn_kernel,
        grid=(batch, ng),
        in_specs=[pl.BlockSpec((hd, GROUP), lambda b, g: (0, cur(b, g))),
                  pl.BlockSpec((GROUP // 2, hd), lambda b, g: (prev(b, g), 0)),
                  pl.BlockSpec((GROUP // 2, hd), lambda b, g: (cur(b, g), 0)),
                  pl.BlockSpec((hd, GROUP), lambda b, g: (0, prev(b, g))),
                  pl.BlockSpec((hd, GROUP), lambda b, g: (0, cur(b, g))),
                  pl.BlockSpec(bias.shape, lambda b, g: (0, 0, 0),
                               pipeline_mode=pl.Buffered(1))],
        out_specs=pl.BlockSpec((hd, GROUP), lambda b, g: (0, cur(b, g))),
        out_shape=jax.ShapeDtypeStruct((hd, T), BF16),
        scratch_shapes=[pltpu.VMEM((UNIT_SKEW + 1, QUAD_BAND, QUAD), F32),
                        pltpu.VMEM((UNIT_SKEW + 1, 1, QUAD), F32),
                        pltpu.VMEM((n_quad, hd, QUAD), F32)],
        compiler_params=pltpu.CompilerParams(
            dimension_semantics=("arbitrary", "arbitrary"), vmem_limit_bytes=VMEM_LIMIT),
        name="chunk_attn",
    )(qT, k, k, vT, vT, bias)


def kernel(x, positions, attn_pre_g, attn_post_g, ffn_pre_g, ffn_post_g, ffn_w_gate,
           ffn_w_up, ffn_w_down, mla_w_a, mla_g_q, mla_w_uq, mla_g_kv, mla_w_ukv, mla_w_o,
           kv_src_g, w_kv_shared, b_w_q, b_rel_table, b_w_o):
    batch, seq, d = x.shape
    T = batch * seq
    assert attn_pre_g.shape[0] == 2 and mla_w_a.shape[0] == 1 and b_w_q.shape[0] == 1
    assert seq % GROUP == 0 and seq % MLA_TQ == 0 and T % PROJ_ROWS == 0
    row = lambda g: g.reshape(1, -1).astype(F32)

    x2 = x.reshape(T, d)
    pos_row = positions.reshape(1, T).astype(F32)
    inv = 1.0 / (ROPE_THETA ** (jnp.arange(0, ROPE_DIM, 2, dtype=F32) / ROPE_DIM))
    inv_col = inv.reshape(ROPE_DIM // 2, 1)

    wa = mla_w_a[0][:, :Q_LORA + KV_LORA].astype(BF16)
    wkrT = jnp.pad(mla_w_a[0][:, Q_LORA + KV_LORA:].T, ((0, LANES - ROPE_DIM), (0, 0))).astype(BF16)
    wuq = mla_w_uq[0].reshape(Q_LORA, MLA_HEADS, NOPE_DIM + ROPE_DIM)
    wuq = jnp.pad(wuq, ((0, 0), (0, 0), (0, MLA_HEAD_PAD - NOPE_DIM - ROPE_DIM)))
    wuqT = wuq.reshape(Q_LORA, MLA_HEADS * MLA_HEAD_PAD).astype(BF16).T
    wukv = mla_w_ukv[0].reshape(KV_LORA, MLA_HEADS, NOPE_DIM + V_DIM)
    wuk = wukv[:, :, :NOPE_DIM].reshape(KV_LORA, MLA_HEADS * NOPE_DIM).astype(BF16)
    wuvT = wukv[:, :, NOPE_DIM:].reshape(KV_LORA, MLA_HEADS * V_DIM).astype(BF16).T
    hd = B_HEADS * B_HEAD_DIM
    wk = w_kv_shared[:, :hd].astype(BF16)
    wvT = w_kv_shared[:, hd:].astype(BF16).T
    wqT = b_w_q[0].astype(BF16).T

    qT0, kn0, kr0, vT0 = _mla_proj(x2, pos_row, inv_col, row(attn_pre_g[0]), wa, wkrT,
                                   row(mla_g_q[0]), wuqT, row(mla_g_kv[0]), wuk, wuvT)
    o0 = _mla_attn(qT0, kn0, kr0, vT0, batch, seq)
    wg, wu, wd = ffn_w_gate.astype(BF16), ffn_w_up.astype(BF16), ffn_w_down.astype(BF16)
    x2 = _block_tail(o0, x2, mla_w_o[0].astype(BF16), row(attn_post_g[0]), row(ffn_pre_g[0]),
                     wg, wu, wd, row(ffn_post_g[0]), 0)

    qT1, k1, vT1, bias = _kvq_proj(x2, row(kv_src_g), row(attn_pre_g[1]), wqT, wk, wvT,
                                   b_rel_table[0])
    o1 = _chunk_attn(qT1, k1, vT1, bias, batch, seq)
    x2 = _block_tail(o1, x2, b_w_o[0].astype(BF16), row(attn_post_g[1]), row(ffn_pre_g[1]),
                     wg, wu, wd, row(ffn_post_g[1]), 1)
    return x2.reshape(batch, seq, d)
```

```python
import functools
import math

import jax
import jax.numpy as jnp
from jax import lax
from jax.experimental import pallas as pl
from jax.experimental.pallas import tpu as pltpu

F32 = jnp.float32
BF16 = jnp.bfloat16

CHUNK = 64
MLA_HEADS = 8
Q_LORA = 384
KV_LORA = 256
NOPE_DIM = 128
ROPE_DIM = 64
V_DIM = 128
ROPE_THETA = 10000.0
B_HEADS = 16
B_HEAD_DIM = 64
LEFT_CHUNKS = 8
BAND = (LEFT_CHUNKS + 1) * CHUNK
MAX_REL = 256
REL_TABLE = MAX_REL + CHUNK
EPS = 1e-6

LANES = 128
SUBLANES = 8
MLA_HEAD_PAD = 2 * LANES
NEG = -1e30

PROJ_ROWS = 1024
TAIL_ROWS = 512
TAIL_SPLIT = 2
MLA_TQ = 512
MLA_TK = 256
MLA_SLOTS = 4
SUM_ROWS = 16
QUAD = 4 * CHUNK
QUAD_BAND = BAND + 3 * CHUNK
GROUP = 2 * QUAD
HEAD_GROUP = 2
UNIT_SKEW = 5
BIAS_ROLL = 1024
V7X_VMEM_BYTES = 64 * 1024 * 1024
VMEM_LIMIT = V7X_VMEM_BYTES - 8 * 1024 * 1024


def _rms(x, g):
    ms = jnp.mean(x * x, axis=-1, keepdims=True)
    return x * lax.rsqrt(ms + EPS) * g


def _dot(a, b):
    return jnp.dot(a, b, preferred_element_type=F32)


def _dot_nt(a, b):
    return lax.dot_general(a, b, (((1,), (1,)), ((), ())), preferred_element_type=F32)


def _mla_proj_kernel(x_ref, posr_ref, invc_ref, gpre_ref, wa_ref, wkrT_ref, gq_ref,
                     wuqT_ref, gkv_ref, wuk_ref, wuvT_ref, qT_ref, kn_ref, kr_ref, vT_ref, *,
                     scale):
    h = _rms(x_ref[...], gpre_ref[...]).astype(BF16)
    a = _dot(h, wa_ref[...])
    cqn = _rms(a[:, :Q_LORA], gq_ref[...]).astype(BF16)
    ckvn = _rms(a[:, Q_LORA:], gkv_ref[...]).astype(BF16)
    half = ROPE_DIM // 2

    angT = invc_ref[...] * posr_ref[...]
    cosT, sinT = jnp.cos(angT), jnp.sin(angT)

    krT = _dot_nt(wkrT_ref[...], h)
    k1, k2 = krT[:half], krT[half:ROPE_DIM]
    kr_rotT = jnp.concatenate(
        [k1 * cosT - k2 * sinT, k2 * cosT + k1 * sinT,
         jnp.zeros((LANES - ROPE_DIM, krT.shape[1]), F32)], axis=0)
    kr_ref[...] = pltpu.bitcast(kr_rotT.T.astype(BF16), jnp.uint32)
    kn_ref[...] = pltpu.bitcast(_dot(ckvn, wuk_ref[...]).astype(BF16), jnp.uint32)

    cosT, sinT = cosT * scale, sinT * scale
    qT = _dot_nt(wuqT_ref[...], cqn)
    for hd in range(MLA_HEADS):
        src = hd * (NOPE_DIM + ROPE_DIM)
        lo = hd * MLA_HEAD_PAD
        t1 = qT[src + NOPE_DIM:src + NOPE_DIM + half]
        t2 = qT[src + NOPE_DIM + half:src + NOPE_DIM + ROPE_DIM]
        qT_ref[lo:lo + NOPE_DIM] = (qT[src:src + NOPE_DIM] * scale).astype(BF16)
        qT_ref[lo + NOPE_DIM:lo + NOPE_DIM + half] = (t1 * cosT - t2 * sinT).astype(BF16)
        qT_ref[lo + NOPE_DIM + half:lo + NOPE_DIM + ROPE_DIM] = (t2 * cosT + t1 * sinT).astype(BF16)
        qT_ref[lo + NOPE_DIM + ROPE_DIM:lo + MLA_HEAD_PAD] = jnp.zeros(
            (MLA_HEAD_PAD - NOPE_DIM - ROPE_DIM, qT.shape[1]), BF16)

    vT = _dot_nt(wuvT_ref[...], ckvn).astype(BF16)
    for t in range(vT_ref.shape[0]):
        vT_ref[t] = vT[:, t * MLA_TK:(t + 1) * MLA_TK]


def _mla_proj(x2, pos_row, inv_col, gpre, wa, wkrT, gq, wuqT, gkv, wuk, wuvT):
    T, D = x2.shape
    tm = PROJ_ROWS
    const = lambda i: (0, 0)
    full = lambda a: pl.BlockSpec(a.shape, const)
    scale = float((NOPE_DIM + ROPE_DIM) ** -0.5 * math.log2(math.e))
    return pl.pallas_call(
        functools.partial(_mla_proj_kernel, scale=scale),
        grid=(T // tm,),
        in_specs=[pl.BlockSpec((tm, D), lambda i: (i, 0)),
                  pl.BlockSpec((1, tm), lambda i: (0, i)),
                  full(inv_col), full(gpre), full(wa), full(wkrT), full(gq), full(wuqT),
                  full(gkv), full(wuk), full(wuvT)],
        out_specs=[pl.BlockSpec((MLA_HEADS * MLA_HEAD_PAD, tm), lambda i: (0, i)),
                   pl.BlockSpec((tm // 2, MLA_HEADS * NOPE_DIM), lambda i: (i, 0)),
                   pl.BlockSpec((tm // 2, LANES), lambda i: (i, 0)),
                   pl.BlockSpec((tm // MLA_TK, MLA_HEADS * V_DIM, MLA_TK), lambda i: (i, 0, 0))],
        out_shape=[jax.ShapeDtypeStruct((MLA_HEADS * MLA_HEAD_PAD, T), BF16),
                   jax.ShapeDtypeStruct((T // 2, MLA_HEADS * NOPE_DIM), jnp.uint32),
                   jax.ShapeDtypeStruct((T // 2, LANES), jnp.uint32),
                   jax.ShapeDtypeStruct((T // MLA_TK, MLA_HEADS * V_DIM, MLA_TK), BF16)],
        compiler_params=pltpu.CompilerParams(
            dimension_semantics=("arbitrary",), vmem_limit_bytes=VMEM_LIMIT),
        name="mla_proj",
    )(x2, pos_row, inv_col, gpre, wa, wkrT, gq, wuqT, gkv, wuk, wuvT)


def _mla_attn_kernel(qT_ref, kn_ref, kr_ref, vT_ref, oT_ref, m_s, l_s, a_s, acc_s, s_s):
    qi = pl.program_id(1)
    n_full = qi * (MLA_TQ // MLA_TK)
    m_s[...] = jnp.full(m_s.shape, NEG, F32)
    l_s[...] = jnp.zeros(l_s.shape, F32)
    a_s[...] = jnp.ones(a_s.shape, F32)
    acc_s[...] = jnp.zeros(acc_s.shape, F32)
    ones = jnp.ones((SUM_ROWS, MLA_TK), BF16)
    half = MLA_TK // 2

    def keys(j, tiles, hd):
        rows = pl.ds(pl.multiple_of(j * half, half), tiles * half)
        k_nope = pltpu.bitcast(kn_ref[rows, hd * NOPE_DIM:(hd + 1) * NOPE_DIM], BF16)
        k_rope = pltpu.bitcast(kr_ref[rows, :], BF16)
        return jnp.concatenate([k_nope, k_rope], axis=1)

    def diag_score(d, hd):
        slot = (d - 2) % MLA_SLOTS * MLA_HEADS + hd
        q0 = d * MLA_TK
        qT = qT_ref[hd * MLA_HEAD_PAD:(hd + 1) * MLA_HEAD_PAD, q0:]
        sT = _dot(keys(n_full + d, 1, hd), qT)
        kc = lax.broadcasted_iota(jnp.int32, sT.shape, 0) // CHUNK
        qc = lax.broadcasted_iota(jnp.int32, sT.shape, 1) // CHUNK
        sT = jnp.where(kc <= qc, sT, NEG)
        s_s[slot, :, q0:] = sT
        m_s[hd, :, q0:] = jnp.maximum(m_s[hd, :, q0:], jnp.max(sT, axis=0, keepdims=True))
        if q0:
            s_s[slot, :, :q0] = jnp.full((MLA_TK, q0), -jnp.inf, F32)

    def pair_score(j, slot0, hd):
        qT = qT_ref[hd * MLA_HEAD_PAD:(hd + 1) * MLA_HEAD_PAD, :]
        sT = _dot(keys(j, 2, hd), qT)
        m = m_s[hd]
        m_new = jnp.maximum(m, jnp.max(sT, axis=0, keepdims=True))
        s_s[slot0 * MLA_HEADS + hd] = sT[:MLA_TK]
        s_s[(slot0 + 1) * MLA_HEADS + hd] = sT[MLA_TK:]
        a_s[slot0 * MLA_HEADS + hd] = jnp.exp2(m - m_new)
        a_s[(slot0 + 1) * MLA_HEADS + hd] = jnp.ones((1, MLA_TQ), F32)
        m_s[hd] = m_new

    def value_unit(j, slot, hd):
        jv = jnp.where(j < 0, n_full + 2 + j, j)
        p = jnp.exp2(s_s[slot * MLA_HEADS + hd] - m_s[hd]).astype(BF16)
        lhs = jnp.concatenate([vT_ref[jv, hd * V_DIM:(hd + 1) * V_DIM, :], ones], axis=0)
        pv = _dot(lhs, p)
        a = a_s[slot * MLA_HEADS + hd]
        acc_s[hd] = a * acc_s[hd] + pv[:V_DIM]
        l_s[hd] = a * l_s[hd] + pv[V_DIM:V_DIM + 1]

    def full_pairs(base, pairs):
        for u in range(0, 2 * pairs, 2):
            for hd in range(MLA_HEADS):
                value_unit(base + u - 2, (u - 2) % MLA_SLOTS, hd)
                value_unit(base + u - 1, (u - 1) % MLA_SLOTS, hd)
                pair_score(base + u, u % MLA_SLOTS, hd)

    def steady(i, carry):
        full_pairs(i * MLA_SLOTS, MLA_SLOTS // 2)
        return carry

    for hd in range(MLA_HEADS):
        diag_score(0, hd)
        diag_score(1, hd)
    n_trips = n_full // MLA_SLOTS
    lax.fori_loop(0, n_trips, steady, 0)

    @pl.when(n_full - n_trips * MLA_SLOTS == 2)
    def _():
        full_pairs(n_full - 2, 1)

    for hd in range(MLA_HEADS):
        value_unit(n_full - 2, (n_full - 2) & (MLA_SLOTS - 1), hd)
        value_unit(n_full - 1, (n_full - 1) & (MLA_SLOTS - 1), hd)
    for hd in range(MLA_HEADS):
        oT_ref[hd * V_DIM:(hd + 1) * V_DIM, :] = (acc_s[hd] * (1.0 / l_s[hd])).astype(BF16)


def _mla_attn(qT, kn, kr, vT3, batch, seq):
    assert MLA_TQ == 2 * MLA_TK and MLA_TK % CHUNK == 0 and MLA_SLOTS == 4
    T = qT.shape[1]
    nq = seq // MLA_TQ
    nkt = seq // MLA_TK
    return pl.pallas_call(
        _mla_attn_kernel,
        grid=(batch, nq),
        in_specs=[pl.BlockSpec((MLA_HEADS * MLA_HEAD_PAD, MLA_TQ), lambda b, i: (0, b * nq + i)),
                  pl.BlockSpec((seq // 2, MLA_HEADS * NOPE_DIM), lambda b, i: (b, 0)),
                  pl.BlockSpec((seq // 2, LANES), lambda b, i: (b, 0)),
                  pl.BlockSpec((nkt, MLA_HEADS * V_DIM, MLA_TK), lambda b, i: (b, 0, 0),
                               pipeline_mode=pl.Buffered(1))],
        out_specs=pl.BlockSpec((MLA_HEADS * V_DIM, MLA_TQ), lambda b, i: (0, b * nq + i)),
        out_shape=jax.ShapeDtypeStruct((MLA_HEADS * V_DIM, T), BF16),
        scratch_shapes=[pltpu.VMEM((MLA_HEADS, 1, MLA_TQ), F32),
                        pltpu.VMEM((MLA_HEADS, 1, MLA_TQ), F32),
                        pltpu.VMEM((MLA_SLOTS * MLA_HEADS, 1, MLA_TQ), F32),
                        pltpu.VMEM((MLA_HEADS, V_DIM, MLA_TQ), F32),
                        pltpu.VMEM((MLA_SLOTS * MLA_HEADS, MLA_TK, MLA_TQ), F32)],
        compiler_params=pltpu.CompilerParams(
            dimension_semantics=("arbitrary", "arbitrary"),
            vmem_limit_bytes=VMEM_LIMIT),
        name="mla_attn",
    )(qT, kn, kr, vT3)


def _block_tail_kernel(oT_ref, x_ref, wo_ref, gpost_ref, gfpre_ref, wg_ref, wu_ref,
                       wd_ref, gfpost_ref, *rest, kvq_scale=None):
    if kvq_scale is None:
        (out_ref,) = rest
    else:
        (gsrc_ref, gpre_ref, wqT_ref, wk_ref, wvT_ref, tab_ref,
         out_ref, qT_ref, k_ref, vT_ref, bias_ref, g_s) = rest
        step = pl.program_id(0)

        @pl.when(step == 0)
        def _():
            _rel_bias_gather(tab_ref[...], g_s)

        n_bias = bias_ref.shape[1] // SUBLANES
        for local in range(n_bias):
            _rel_bias_rows(g_s, bias_ref, step * n_bias + local, local)
    sub = out_ref.shape[0] // TAIL_SPLIT
    blocks = [slice(i * sub, (i + 1) * sub) for i in range(TAIL_SPLIT)]
    y = [lax.dot_general(oT_ref[:, r], wo_ref[...], (((0,), (0,)), ((), ())),
                         preferred_element_type=F32) for r in blocks]
    x1 = [x_ref[r, :] + _rms(yi, gpost_ref[...]) for r, yi in zip(blocks, y)]
    h = [_rms(xi, gfpre_ref[...]).astype(BF16) for xi in x1]
    gu = [(_dot(hi, wg_ref[...]), _dot(hi, wu_ref[...])) for hi in h]
    a = [(gi * (1.0 / (1.0 + jnp.exp(-gi))) * ui).astype(BF16) for gi, ui in gu]
    f = [_dot(ai, wd_ref[...]) for ai in a]
    x_out = [xi + _rms(fi, gfpost_ref[...]) for xi, fi in zip(x1, f)]
    for r, xo in zip(blocks, x_out):
        out_ref[r, :] = xo
    if kvq_scale is not None:
        xr = [xo * lax.rsqrt(jnp.mean(xo * xo, axis=-1, keepdims=True) + EPS) for xo in x_out]
        xs = [(v * gsrc_ref[...]).astype(BF16) for v in xr]
        xq = [(v * gpre_ref[...]).astype(BF16) for v in xr]
        for n, r in enumerate(blocks):
            qT_ref[:, r] = (_dot_nt(wqT_ref[...], xq[n]) * kvq_scale).astype(BF16)
            k_ref[r.start // 2:r.stop // 2, :] = pltpu.bitcast(
                _dot(xs[n], wk_ref[...]).astype(BF16), jnp.uint32)
            vT_ref[:, r] = _dot_nt(wvT_ref[...], xs[n]).astype(BF16)


def _block_tail(oT, x2, wo, gpost, gfpre, wg, wu, wd, gfpost, layer, kvq=None):
    T, D = x2.shape
    tm = TAIL_ROWS
    steps = T // tm
    const = lambda i: (0, 0)
    full = lambda a: pl.BlockSpec(a.shape, const, pipeline_mode=pl.Buffered(1))
    of_layer = lambda a: pl.BlockSpec((None,) + a.shape[1:], lambda i: (layer, 0, 0),
                                      pipeline_mode=pl.Buffered(1))
    row = lambda a: pl.BlockSpec((tm, a.shape[1]), lambda i: (i, 0))
    in_specs = [pl.BlockSpec((oT.shape[0], tm), lambda i: (0, i)), row(x2), full(wo),
                full(gpost), full(gfpre), of_layer(wg), of_layer(wu), of_layer(wd),
                full(gfpost)]
    out_specs = [pl.BlockSpec((tm, D), lambda i: (i, 0))]
    out_shape = [jax.ShapeDtypeStruct((T, D), F32)]
    operands = [oT, x2, wo, gpost, gfpre, wg, wu, wd, gfpost]
    scratch, scale = [], None
    if kvq is not None:
        hd = B_HEADS * B_HEAD_DIM
        assert QUAD_BAND % (steps * SUBLANES) == 0
        scale = float(B_HEAD_DIM ** -0.5 * math.log2(math.e))
        in_specs += [full(a) for a in kvq]
        operands += list(kvq)
        out_specs += [pl.BlockSpec((hd, tm), lambda i: (0, i)),
                      pl.BlockSpec((tm // 2, hd), lambda i: (i, 0)),
                      pl.BlockSpec((hd, tm), lambda i: (0, i)),
                      pl.BlockSpec((B_HEADS, QUAD_BAND // steps, QUAD), lambda i: (0, i, 0))]
        out_shape += [jax.ShapeDtypeStruct((hd, T), BF16),
                      jax.ShapeDtypeStruct((T // 2, hd), jnp.uint32),
                      jax.ShapeDtypeStruct((hd, T), BF16),
                      jax.ShapeDtypeStruct((B_HEADS, QUAD_BAND, QUAD), F32)]
        scratch = [pltpu.VMEM((B_HEADS, BIAS_ROLL), F32)]
    return pl.pallas_call(
        functools.partial(_block_tail_kernel, kvq_scale=scale),
        grid=(steps,),
        in_specs=in_specs,
        out_specs=out_specs,
        out_shape=out_shape,
        scratch_shapes=scratch,
        compiler_params=pltpu.CompilerParams(
            dimension_semantics=("arbitrary",), vmem_limit_bytes=VMEM_LIMIT),
        name="block_tail",
    )(*operands)


def _rel_bias_gather(tab, g_s):
    t_hi = tab.astype(BF16)
    rem = tab - t_hi.astype(F32)
    t_mid = rem.astype(BF16)
    t_lo = (rem - t_mid.astype(F32)).astype(BF16)
    row = lax.broadcasted_iota(jnp.int32, (REL_TABLE, BIAS_ROLL), 0)
    lane = lax.broadcasted_iota(jnp.int32, (REL_TABLE, BIAS_ROLL), 1)
    dist = jnp.where(lane < QUAD, lane, lane - BIAS_ROLL) + LEFT_CHUNKS * CHUNK
    idx = jnp.clip(dist, -(CHUNK - 1), MAX_REL) + (CHUNK - 1)
    onehot = jnp.where(idx == row, 1.0, 0.0).astype(BF16)
    g_s[...] = (_dot(t_hi, onehot) + _dot(t_mid, onehot) + _dot(t_lo, onehot)) * math.log2(math.e)


def _rel_bias_rows(g_s, out_ref, cb, local):
    c0 = cb * SUBLANES
    q_chunk = lax.broadcasted_iota(jnp.int32, (1, QUAD), 1) // CHUNK + LEFT_CHUNKS
    back = q_chunk - cb // (CHUNK // SUBLANES)
    valid = (back >= 0) & (back <= LEFT_CHUNKS)
    for h in range(B_HEADS):
        rows = jnp.broadcast_to(g_s[h:h + 1, :], (SUBLANES, BIAS_ROLL))
        rolled = pltpu.roll(rows, c0, 1, stride=1, stride_axis=0)
        out_ref[h, local * SUBLANES:(local + 1) * SUBLANES, :] = jnp.where(
            valid, rolled[:, :QUAD], NEG)


def _chunk_attn_kernel(qT_ref, kp_ref, kc_ref, vTp_ref, vTc_ref, bias_ref, o_ref,
                       s_s, m_s, oT_s):
    g = pl.program_id(1)
    slab = HEAD_GROUP * B_HEAD_DIM
    ones = jnp.ones((SUM_ROWS, QUAD_BAND), BF16)
    no_q = jnp.zeros((B_HEAD_DIM, QUAD), BF16)
    n_units = (GROUP // QUAD) * B_HEADS

    def score_unit(u, first_group):
        qd, h = divmod(u, B_HEADS)
        lanes = slice(h // HEAD_GROUP * slab, (h // HEAD_GROUP + 1) * slab)
        kb = pltpu.bitcast(jnp.concatenate(
            [kp_ref[qd * (QUAD // 2):, lanes], kc_ref[:(qd + 1) * (QUAD // 2), lanes]], axis=0),
            BF16)
        q_own = qT_ref[h * B_HEAD_DIM:(h + 1) * B_HEAD_DIM, qd * QUAD:(qd + 1) * QUAD]
        pos = h % HEAD_GROUP
        q_slab = jnp.concatenate([no_q] * pos + [q_own] + [no_q] * (HEAD_GROUP - 1 - pos), axis=0)
        s = _dot(kb, q_slab) + bias_ref[h]
        if first_group:
            key = lax.broadcasted_iota(jnp.int32, (QUAD_BAND, 1), 0)
            s = jnp.where(key >= GROUP - qd * QUAD, s, NEG)
        slot = u % (UNIT_SKEW + 1)
        s_s[slot] = s
        m_s[slot] = jnp.max(s, axis=0, keepdims=True)

    def value_unit(u):
        qd, h = divmod(u, B_HEADS)
        slot = u % (UNIT_SKEW + 1)
        rows = slice(h * B_HEAD_DIM, (h + 1) * B_HEAD_DIM)
        p = jnp.exp2(s_s[slot] - m_s[slot]).astype(BF16)
        v_band = jnp.concatenate(
            [vTp_ref[rows, qd * QUAD:], vTc_ref[rows, :(qd + 1) * QUAD]], axis=1)
        r = _dot(jnp.concatenate([v_band, ones], axis=0), p)
        oT_s[qd, rows, :] = r[:B_HEAD_DIM] * (1.0 / r[B_HEAD_DIM:B_HEAD_DIM + 1])

    def run(first_group):
        for u in range(n_units + UNIT_SKEW):
            if u >= UNIT_SKEW:
                value_unit(u - UNIT_SKEW)
            if u < n_units:
                score_unit(u, first_group)

    pl.when(g == 0)(lambda: run(True))
    pl.when(g != 0)(lambda: run(False))
    for qd in range(GROUP // QUAD):
        o_ref[:, qd * QUAD:(qd + 1) * QUAD] = oT_s[qd].astype(BF16)


def _chunk_attn(qT, k, vT, bias, batch, seq):
    hd, T = qT.shape
    ng = seq // GROUP
    n_quad = GROUP // QUAD
    prev = lambda b, g: b * ng + jnp.maximum(g - 1, 0)
    cur = lambda b, g: b * ng + g
    return pl.pallas_call(
        _chunk_attn_kernel,
        grid=(batch, ng),
        in_specs=[pl.BlockSpec((hd, GROUP), lambda b, g: (0, cur(b, g))),
                  pl.BlockSpec((GROUP // 2, hd), lambda b, g: (prev(b, g), 0)),
                  pl.BlockSpec((GROUP // 2, hd), lambda b, g: (cur(b, g), 0)),
                  pl.BlockSpec((hd, GROUP), lambda b, g: (0, prev(b, g))),
                  pl.BlockSpec((hd, GROUP), lambda b, g: (0, cur(b, g))),
                  pl.BlockSpec(bias.shape, lambda b, g: (0, 0, 0),
                               pipeline_mode=pl.Buffered(1))],
        out_specs=pl.BlockSpec((hd, GROUP), lambda b, g: (0, cur(b, g))),
        out_shape=jax.ShapeDtypeStruct((hd, T), BF16),
        scratch_shapes=[pltpu.VMEM((UNIT_SKEW + 1, QUAD_BAND, QUAD), F32),
                        pltpu.VMEM((UNIT_SKEW + 1, 1, QUAD), F32),
                        pltpu.VMEM((n_quad, hd, QUAD), F32)],
        compiler_params=pltpu.CompilerParams(
            dimension_semantics=("arbitrary", "arbitrary"), vmem_limit_bytes=VMEM_LIMIT),
        name="chunk_attn",
    )(qT, k, k, vT, vT, bias)


def kernel(x, positions, attn_pre_g, attn_post_g, ffn_pre_g, ffn_post_g, ffn_w_gate,
           ffn_w_up, ffn_w_down, mla_w_a, mla_g_q, mla_w_uq, mla_g_kv, mla_w_ukv, mla_w_o,
           kv_src_g, w_kv_shared, b_w_q, b_rel_table, b_w_o):
    batch, seq, d = x.shape
    T = batch * seq
    assert attn_pre_g.shape[0] == 2 and mla_w_a.shape[0] == 1 and b_w_q.shape[0] == 1
    assert seq % GROUP == 0 and seq % MLA_TQ == 0 and T % PROJ_ROWS == 0
    row = lambda g: g.reshape(1, -1).astype(F32)

    x2 = x.reshape(T, d)
    pos_row = positions.reshape(1, T).astype(F32)
    inv = 1.0 / (ROPE_THETA ** (jnp.arange(0, ROPE_DIM, 2, dtype=F32) / ROPE_DIM))
    inv_col = inv.reshape(ROPE_DIM // 2, 1)

    wa = mla_w_a[0][:, :Q_LORA + KV_LORA].astype(BF16)
    wkrT = jnp.pad(mla_w_a[0][:, Q_LORA + KV_LORA:].T, ((0, LANES - ROPE_DIM), (0, 0))).astype(BF16)
    wuqT = mla_w_uq[0].astype(BF16).T
    wukv = mla_w_ukv[0].reshape(KV_LORA, MLA_HEADS, NOPE_DIM + V_DIM)
    wuk = wukv[:, :, :NOPE_DIM].reshape(KV_LORA, MLA_HEADS * NOPE_DIM).astype(BF16)
    wuvT = wukv[:, :, NOPE_DIM:].reshape(KV_LORA, MLA_HEADS * V_DIM).astype(BF16).T
    hd = B_HEADS * B_HEAD_DIM
    wk = w_kv_shared[:, :hd].astype(BF16)
    wvT = w_kv_shared[:, hd:].astype(BF16).T
    wqT = b_w_q[0].astype(BF16).T

    qT0, kn0, kr0, vT0 = _mla_proj(x2, pos_row, inv_col, row(attn_pre_g[0]), wa, wkrT,
                                   row(mla_g_q[0]), wuqT, row(mla_g_kv[0]), wuk, wuvT)
    o0 = _mla_attn(qT0, kn0, kr0, vT0, batch, seq)
    wg, wu, wd = ffn_w_gate.astype(BF16), ffn_w_up.astype(BF16), ffn_w_down.astype(BF16)
    x2, qT1, k1, vT1, bias = _block_tail(
        o0, x2, mla_w_o[0].astype(BF16), row(attn_post_g[0]), row(ffn_pre_g[0]),
        wg, wu, wd, row(ffn_post_g[0]), 0,
        kvq=(row(kv_src_g), row(attn_pre_g[1]), wqT, wk, wvT, b_rel_table[0]))

    o1 = _chunk_attn(qT1, k1, vT1, bias, batch, seq)
    (x2,) = _block_tail(o1, x2, b_w_o[0].astype(BF16), row(attn_post_g[1]), row(ffn_pre_g[1]),
                        wg, wu, wd, row(ffn_post_g[1]), 1)
    return x2.reshape(batch, seq, d)
```
